```python
import math
import jax, jax.numpy as jnp
from jax import lax
import numpy as np

D_MODEL = 1024
BATCH = 16
SEQ = 4096
DEPTH = 2

CHUNK = 64
N_LEFT_CHUNKS = 8
D_S5 = D_MODEL // 4
S5_GROUP = 16
S5_GROUPS = D_S5 // S5_GROUP
S5_STATE = 64
D_ATT = D_MODEL // 2
ATT_HEAD_DIM = 64
ATT_HEADS = D_ATT // ATT_HEAD_DIM
MAX_REL = 128
D_CONV = D_MODEL // 4
CONV_WIDTH = 31
N_BRANCHES = 3
D_FF = 2816
EPS = 1e-6
DT_MIN = 1e-3
DT_MAX = 1e-1

SPLIT_POINTS = [D_S5, D_S5 + D_ATT, D_S5 + 2 * D_ATT, D_S5 + 3 * D_ATT,
                D_S5 + 3 * D_ATT + 2 * D_CONV]
IN_COLS = D_S5 + 3 * D_ATT + 2 * D_CONV + N_BRANCHES * D_MODEL

kernel_name = "hybrid_s5_chunkattn_conformer_conv_gated"


def rmsnorm(x, g):
    xf = x.astype(jnp.float32)
    y = xf * lax.rsqrt(jnp.mean(xf * xf, axis=-1, keepdims=True) + EPS)
    return (y * g.astype(jnp.float32)).astype(x.dtype)


def layernorm(x, g, b):
    xf = x.astype(jnp.float32)
    mu = jnp.mean(xf, axis=-1, keepdims=True)
    var = jnp.mean(jnp.square(xf - mu), axis=-1, keepdims=True)
    y = (xf - mu) * lax.rsqrt(var + EPS)
    return (y * g.astype(jnp.float32) + b.astype(jnp.float32)).astype(x.dtype)


def swiglu_ffn(x, w_up, w_down):
    a, b = jnp.split(x @ w_up, 2, axis=-1)
    return (jax.nn.silu(a) * b) @ w_down


def _complex_linear_combine(e1, e2):
    a1r, a1i, b1r, b1i = e1
    a2r, a2i, b2r, b2i = e2
    ar = a2r * a1r - a2i * a1i
    ai = a2r * a1i + a2i * a1r
    br = a2r * b1r - a2i * b1i + b2r
    bi = a2r * b1i + a2i * b1r + b2i
    return (ar, ai, br, bi)


def s5_mixer(u, lambda_re, lambda_im, log_dt, b_re, b_im, c_re, c_im, d_skip, w_glu):
    bsz, seq, _ = u.shape
    f32 = jnp.float32
    uf = u.astype(f32).reshape(bsz, seq, S5_GROUPS, S5_GROUP)
    lr = jnp.minimum(lambda_re.astype(f32), -1e-4)
    li = lambda_im.astype(f32)
    dt = jnp.exp(log_dt.astype(f32))[:, None]
    mag = jnp.exp(lr * dt)
    ar = mag * jnp.cos(li * dt)
    ai = mag * jnp.sin(li * dt)
    den = lr * lr + li * li
    coef_r = ((ar - 1.0) * lr + ai * li) / den
    coef_i = (ai * lr - (ar - 1.0) * li) / den
    br = b_re.astype(f32)
    bi = b_im.astype(f32)
    bbar_r = coef_r[..., None] * br - coef_i[..., None] * bi
    bbar_i = coef_r[..., None] * bi + coef_i[..., None] * br
    bu_r = jnp.einsum('bsgc,gpc->bsgp', uf, bbar_r)
    bu_i = jnp.einsum('bsgc,gpc->bsgp', uf, bbar_i)
    a_r = jnp.broadcast_to(ar, bu_r.shape)
    a_i = jnp.broadcast_to(ai, bu_i.shape)
    _, _, xr, xi = lax.associative_scan(_complex_linear_combine, (a_r, a_i, bu_r, bu_i), axis=1)
    y = (jnp.einsum('bsgp,gcp->bsgc', xr, c_re.astype(f32))
         - jnp.einsum('bsgp,gcp->bsgc', xi, c_im.astype(f32))
         + d_skip.astype(f32).reshape(S5_GROUPS, S5_GROUP) * uf)
    y = jax.nn.gelu(y.reshape(bsz, seq, D_S5)).astype(u.dtype)
    a, g = jnp.split(y @ w_glu, 2, axis=-1)
    return a * jax.nn.sigmoid(g)


def chunked_attention(q, k, v, q_gain, k_gain, rel_bias):
    bsz, seq, _ = q.shape
    n_chunks = seq // CHUNK
    q = rmsnorm(q.reshape(bsz, seq, ATT_HEADS, ATT_HEAD_DIM), q_gain)
    k = rmsnorm(k.reshape(bsz, seq, ATT_HEADS, ATT_HEAD_DIM), k_gain)
    v = v.reshape(bsz, seq, ATT_HEADS, ATT_HEAD_DIM)
    pad = N_LEFT_CHUNKS * CHUNK
    band = (N_LEFT_CHUNKS + 1) * CHUNK
    k_pad = jnp.pad(k, ((0, 0), (pad, 0), (0, 0), (0, 0)))
    v_pad = jnp.pad(v, ((0, 0), (pad, 0), (0, 0), (0, 0)))
    qi = jnp.arange(CHUNK)[:, None]
    kj = jnp.arange(band)[None, :]
    rel = jnp.clip(pad + qi - kj, -MAX_REL, MAX_REL) + MAX_REL
    bias = rel_bias.astype(jnp.float32)[:, rel]
    scale = ATT_HEAD_DIM ** -0.5

    def one_chunk(c):
        start = c * CHUNK
        qc = lax.dynamic_slice_in_dim(q, start, CHUNK, axis=1)
        kc = lax.dynamic_slice_in_dim(k_pad, start, band, axis=1)
        vc = lax.dynamic_slice_in_dim(v_pad, start, band, axis=1)
        s = jnp.einsum('bqhd,bkhd->bhqk', qc, kc).astype(jnp.float32) * scale + bias[None]
        valid = kj >= (pad - start)
        s = jnp.where(valid[None, None], s, -1e30)
        p = jax.nn.softmax(s, axis=-1).astype(vc.dtype)
        return jnp.einsum('bhqk,bkhd->bqhd', p, vc)

    out = lax.map(one_chunk, jnp.arange(n_chunks))
    return out.transpose(1, 0, 2, 3, 4).reshape(bsz, seq, D_ATT)


def conv_module(z, w_dw, b_dw, ln_g, ln_b, w_pw):
    a, g = jnp.split(z, 2, axis=-1)
    h = a * jax.nn.sigmoid(g)
    h = lax.conv_general_dilated(
        h, w_dw[:, None, :], window_strides=(1,), padding=[(CONV_WIDTH - 1, 0)],
        dimension_numbers=('NWC', 'WIO', 'NWC'), feature_group_count=D_CONV) + b_dw
    h = jax.nn.silu(layernorm(h, ln_g, ln_b))
    return h @ w_pw


def setup_inputs(seed: int = 0) -> dict:
    key = jax.random.key(seed)
    ks = jax.random.split(key, 32)
    L = DEPTH

    def nrm(k, shape, scale):
        return jax.random.normal(k, shape, jnp.float32) * scale

    n_idx = jnp.arange(S5_STATE, dtype=jnp.float32)
    lam_im = jnp.broadcast_to(math.pi * n_idx, (L, S5_GROUPS, S5_STATE))
    return {
        "x": nrm(ks[0], (BATCH, SEQ, D_MODEL), 1.0),
        "ffn1_norm": 1.0 + nrm(ks[1], (L, D_MODEL), 0.02),
        "ffn1_w_up": nrm(ks[2], (L, D_MODEL, 2 * D_FF), D_MODEL ** -0.5),
        "ffn1_w_down": nrm(ks[3], (L, D_FF, D_MODEL), D_FF ** -0.5),
        "mix_norm": 1.0 + nrm(ks[4], (L, D_MODEL), 0.02),
        "w_in": nrm(ks[5], (L, D_MODEL, IN_COLS), D_MODEL ** -0.5),
        "b_gate": nrm(ks[6], (L, N_BRANCHES * D_MODEL), 0.01),
        "s5_lambda_re": -0.5 + nrm(ks[7], (L, S5_GROUPS, S5_STATE), 0.01),
        "s5_lambda_im": lam_im + nrm(ks[8], (L, S5_GROUPS, S5_STATE), 0.01),
        "s5_log_dt": jax.random.uniform(ks[9], (L, S5_GROUPS), jnp.float32,
                                        math.log(DT_MIN), math.log(DT_MAX)),
        "s5_b_re": nrm(ks[10], (L, S5_GROUPS, S5_STATE, S5_GROUP), (2 * S5_GROUP) ** -0.5),
        "s5_b_im": nrm(ks[11], (L, S5_GROUPS, S5_STATE, S5_GROUP), (2 * S5_GROUP) ** -0.5),
        "s5_c_re": nrm(ks[12], (L, S5_GROUPS, S5_GROUP, S5_STATE), (2 * S5_STATE) ** -0.5),
        "s5_c_im": nrm(ks[13], (L, S5_GROUPS, S5_GROUP, S5_STATE), (2 * S5_STATE) ** -0.5),
        "s5_d": nrm(ks[14], (L, D_S5), 1.0),
        "s5_w_glu": nrm(ks[15], (L, D_S5, 2 * D_S5), D_S5 ** -0.5),
        "w_br_s5": nrm(ks[16], (L, D_S5, D_MODEL), D_S5 ** -0.5),
        "attn_q_gain": 1.0 + nrm(ks[17], (L, ATT_HEAD_DIM), 0.02),
        "attn_k_gain": 1.0 + nrm(ks[18], (L, ATT_HEAD_DIM), 0.02),
        "attn_rel_bias": nrm(ks[19], (L, ATT_HEADS, 2 * MAX_REL + 1), 0.1),
        "w_br_attn": nrm(ks[20], (L, D_ATT, D_MODEL), D_ATT ** -0.5),
        "conv_w_dw": nrm(ks[21], (L, CONV_WIDTH, D_CONV), CONV_WIDTH ** -0.5),
        "conv_b_dw": nrm(ks[22], (L, D_CONV), 0.01),
        "conv_ln_g": 1.0 + nrm(ks[23], (L, D_CONV), 0.02),
        "conv_ln_b": nrm(ks[24], (L, D_CONV), 0.01),
        "w_br_conv": nrm(ks[25], (L, D_CONV, D_MODEL), D_CONV ** -0.5),
        "w_out": nrm(ks[26], (L, D_MODEL, D_MODEL), D_MODEL ** -0.5),
        "ffn2_norm": 1.0 + nrm(ks[27], (L, D_MODEL), 0.02),
        "ffn2_w_up": nrm(ks[28], (L, D_MODEL, 2 * D_FF), D_MODEL ** -0.5),
        "ffn2_w_down": nrm(ks[29], (L, D_FF, D_MODEL), D_FF ** -0.5),
    }


def reference(x, ffn1_norm, ffn1_w_up, ffn1_w_down, mix_norm, w_in, b_gate,
              s5_lambda_re, s5_lambda_im, s5_log_dt, s5_b_re, s5_b_im, s5_c_re, s5_c_im,
              s5_d, s5_w_glu, w_br_s5, attn_q_gain, attn_k_gain, attn_rel_bias, w_br_attn,
              conv_w_dw, conv_b_dw, conv_ln_g, conv_ln_b, w_br_conv, w_out,
              ffn2_norm, ffn2_w_up, ffn2_w_down):
    for l in range(DEPTH):
        x = x + 0.5 * swiglu_ffn(rmsnorm(x, ffn1_norm[l]), ffn1_w_up[l], ffn1_w_down[l])

        h = rmsnorm(x, mix_norm[l])
        proj = h @ w_in[l]
        u_s5, q, k, v, z_conv, gate_logits = jnp.split(proj, SPLIT_POINTS, axis=-1)

        y_s5 = s5_mixer(u_s5, s5_lambda_re[l], s5_lambda_im[l], s5_log_dt[l],
                        s5_b_re[l], s5_b_im[l], s5_c_re[l], s5_c_im[l],
                        s5_d[l], s5_w_glu[l]) @ w_br_s5[l]
        y_attn = chunked_attention(q, k, v, attn_q_gain[l], attn_k_gain[l],
                                   attn_rel_bias[l]) @ w_br_attn[l]
        y_conv = conv_module(z_conv, conv_w_dw[l], conv_b_dw[l], conv_ln_g[l],
                             conv_ln_b[l], w_br_conv[l])

        gates = jax.nn.sigmoid(gate_logits + b_gate[l])
        g_s5, g_attn, g_conv = jnp.split(gates, N_BRANCHES, axis=-1)
        merged = g_s5 * y_s5 + g_attn * y_attn + g_conv * y_conv
        x = x + merged @ w_out[l]

        x = x + 0.5 * swiglu_ffn(rmsnorm(x, ffn2_norm[l]), ffn2_w_up[l], ffn2_w_down[l])
    return x
```

```python
import functools
import math

import jax
import jax.numpy as jnp
from jax import lax
from jax.experimental import pallas as pl
from jax.experimental.pallas import tpu as pltpu

F32 = jnp.float32
BF16 = jnp.bfloat16

D_MODEL = 1024
D_S5 = 256
S5_GROUP = 16
S5_GROUPS = 16
S5_STATE = 64
D_ATT = 512
HEAD_DIM = 64
HEADS = 8
CHUNK = 64
N_LEFT = 8
MAX_REL = 128
D_CONV = 256
CONV_WIDTH = 31
D_FF = 2816
EPS = 1e-6
N_SMALL = D_S5 + 3 * D_ATT + 2 * D_CONV

LANES = 128
VMEM_LIMIT = 56 * 1024 * 1024

TM = 512
S5_L = 8
S5_ROW = S5_L * D_S5
S5_NSTATE = 2 * S5_GROUPS * S5_STATE
S5_TOK = 256
ATT_QB = 4 * CHUNK
ATT_W = ATT_QB + N_LEFT * CHUNK
CONV_TR = 128
CONV_PAD = 32


def _resident(shape):
    nd = len(shape)
    return pl.BlockSpec(shape, lambda *_: (0,) * nd, pipeline_mode=pl.Buffered(1))


def _rmsnorm_bf16(x, g):
    ms = jnp.mean(x * x, axis=-1, keepdims=True)
    return ((x * lax.rsqrt(ms + EPS)) * g).astype(BF16)


def _sigmoid(x):
    return jax.nn.sigmoid(x)


def _ffn_kernel(x_ref, g_ref, wa_ref, wb_ref, wd_ref, o_ref, *, tf):
    x = x_ref[...]
    h = _rmsnorm_bf16(x, g_ref[...])
    y = jnp.zeros_like(x)
    for f0 in range(0, D_FF, tf):
        a = jnp.dot(h, wa_ref[:, f0:f0 + tf], preferred_element_type=F32)
        b = jnp.dot(h, wb_ref[:, f0:f0 + tf], preferred_element_type=F32)
        act = ((a * _sigmoid(a)) * b).astype(BF16)
        y = y + jnp.dot(act, wd_ref[f0:f0 + tf, :], preferred_element_type=F32)
    o_ref[...] = x + 0.5 * y


def _ffn(x, g, wa, wb, wd):
    n = x.shape[0]
    tok = pl.BlockSpec((TM, D_MODEL), lambda i: (i, 0))
    return pl.pallas_call(
        functools.partial(_ffn_kernel, tf=D_FF // 2),
        out_shape=jax.ShapeDtypeStruct((n, D_MODEL), F32),
        grid=(n // TM,),
        in_specs=[tok, _resident((1, D_MODEL)), _resident((D_MODEL, D_FF)),
                  _resident((D_MODEL, D_FF)), _resident((D_FF, D_MODEL))],
        out_specs=tok,
        compiler_params=pltpu.CompilerParams(
            dimension_semantics=("arbitrary",), vmem_limit_bytes=VMEM_LIMIT),
        name="ffn",
    )(x, g, wa, wb, wd)


def _head_pair_norm(x2, gain2):
    lane = lax.broadcasted_iota(jnp.int32, (1, LANES), 1)
    first = lane < HEAD_DIM
    sq = x2 * x2
    s_a = jnp.sum(jnp.where(first, sq, 0.0), axis=-1, keepdims=True)
    s_b = jnp.sum(jnp.where(first, 0.0, sq), axis=-1, keepdims=True)
    ms = jnp.where(first, s_a, s_b) * (1.0 / HEAD_DIM)
    return (x2 * lax.rsqrt(ms + EPS)) * gain2


def _inproj_kernel(x_ref, g_ref, w_ref, qg_ref, kg_ref, u_ref, q_ref, k_ref, v_ref, hc_ref):
    h = _rmsnorm_bf16(x_ref[...], g_ref[...])
    proj = jnp.dot(h, w_ref[...], preferred_element_type=F32)
    u_ref[...] = proj[:, :D_S5]
    q0, k0, v0, z0 = D_S5, D_S5 + D_ATT, D_S5 + 2 * D_ATT, D_S5 + 3 * D_ATT
    scale = HEAD_DIM ** -0.5
    for p in range(D_ATT // LANES):
        lo = p * LANES
        qn = _head_pair_norm(proj[:, q0 + lo:q0 + lo + LANES], qg_ref[...])
        kn = _head_pair_norm(proj[:, k0 + lo:k0 + lo + LANES], kg_ref[...])
        q_ref[:, lo:lo + LANES] = (qn * scale).astype(BF16)
        k_ref[:, lo:lo + LANES] = kn.astype(BF16)
    v_ref[...] = proj[:, v0:v0 + D_ATT].astype(BF16)
    a = proj[:, z0:z0 + D_CONV]
    gt = proj[:, z0 + D_CONV:z0 + 2 * D_CONV]
    hc_ref[...] = a * _sigmoid(gt)


def _inproj(x, g, w, qg2, kg2):
    n = x.shape[0]
    def tok(width):
        return pl.BlockSpec((TM, width), lambda i: (i, 0))
    return pl.pallas_call(
        _inproj_kernel,
        out_shape=(jax.ShapeDtypeStruct((n, D_S5), F32),
                   jax.ShapeDtypeStruct((n, D_ATT), BF16),
                   jax.ShapeDtypeStruct((n, D_ATT), BF16),
                   jax.ShapeDtypeStruct((n, D_ATT), BF16),
                   jax.ShapeDtypeStruct((n, D_CONV), F32)),
        grid=(n // TM,),
        in_specs=[tok(D_MODEL), _resident((1, D_MODEL)), _resident((D_MODEL, N_SMALL)),
                  _resident((1, LANES)), _resident((1, LANES))],
        out_specs=(tok(D_S5), tok(D_ATT), tok(D_ATT), tok(D_ATT), tok(D_CONV)),
        compiler_params=pltpu.CompilerParams(
            dimension_semantics=("arbitrary",), vmem_limit_bytes=VMEM_LIMIT),
        name="inproj",
    )(x, g, w, qg2, kg2)


def _s5_tables(lambda_re, lambda_im, log_dt, b_re, b_im, c_re, c_im, d_skip):
    lr = jnp.minimum(lambda_re.astype(F32), -1e-4)
    li = lambda_im.astype(F32)
    dt = jnp.exp(log_dt.astype(F32))[:, None]
    mag = jnp.exp(lr * dt)
    ar = mag * jnp.cos(li * dt)
    ai = mag * jnp.sin(li * dt)
    den = lr * lr + li * li
    coef_r = ((ar - 1.0) * lr + ai * li) / den
    coef_i = (ai * lr - (ar - 1.0) * li) / den
    br = b_re.astype(F32)
    bi = b_im.astype(F32)
    bbar_r = coef_r[..., None] * br - coef_i[..., None] * bi
    bbar_i = coef_r[..., None] * bi + coef_i[..., None] * br
    cr = c_re.astype(F32)
    ci = c_im.astype(F32)

    pr, pi = [jnp.ones_like(ar)], [jnp.zeros_like(ai)]
    for _ in range(S5_L):
        r, i = pr[-1], pi[-1]
        pr.append(r * ar - i * ai)
        pi.append(r * ai + i * ar)
    pr = jnp.stack(pr)
    pi = jnp.stack(pi)

    eye_g = jnp.eye(S5_GROUPS, dtype=F32)

    pw_r = pr[S5_L - 1::-1][:S5_L]
    pw_i = pi[S5_L - 1::-1][:S5_L]
    e_r = pw_r[..., None] * bbar_r[None] - pw_i[..., None] * bbar_i[None]
    e_i = pw_r[..., None] * bbar_i[None] + pw_i[..., None] * bbar_r[None]
    def to_p(e):
        e = jnp.transpose(e, (0, 1, 3, 2))
        return e[:, :, :, None, :] * eye_g[None, :, None, :, None]
    p_mat = jnp.stack([to_p(e_r), to_p(e_i)], axis=3)
    p_mat = p_mat.reshape(S5_ROW, S5_NSTATE)

    m_r = pr[:S5_L, :, None, :] * cr[None] - pi[:S5_L, :, None, :] * ci[None]
    m_i = pr[:S5_L, :, None, :] * ci[None] + pi[:S5_L, :, None, :] * cr[None]
    kern = (jnp.sum(m_r[:, :, :, :, None] * bbar_r[None, :, None, :, :], axis=3)
            - jnp.sum(m_i[:, :, :, :, None] * bbar_i[None, :, None, :, :], axis=3))
    t_idx = jnp.arange(S5_L)
    lag = t_idx[None, :] - t_idx[:, None]
    k_st = jnp.where((lag >= 0)[:, :, None, None, None],
                     kern[jnp.clip(lag, 0, S5_L - 1)], 0.0)
    t_mat = jnp.transpose(k_st, (0, 2, 4, 1, 3))
    t_mat = t_mat[:, :, :, :, None, :] * eye_g[None, :, None, None, :, None]
    t_mat = t_mat.reshape(S5_ROW, S5_ROW)

    q_r = pr[1:, :, None, :] * cr[None] - pi[1:, :, None, :] * ci[None]
    q_i = -(pr[1:, :, None, :] * ci[None] + pi[1:, :, None, :] * cr[None])
    def to_q(m):
        m = jnp.transpose(m, (1, 3, 0, 2))
        return m[:, :, :, None, :] * eye_g[:, None, None, :, None]
    q_mat = jnp.stack([to_q(q_r), to_q(q_i)], axis=0).reshape(S5_NSTATE, S5_ROW)

    half_blocks = S5_NSTATE // 2 // LANES
    a_r = pr[S5_L].reshape(half_blocks, 1, LANES)
    a_i = pi[S5_L].reshape(half_blocks, 1, LANES)
    d_row = jnp.tile(d_skip.astype(F32).reshape(1, D_S5), (1, S5_L))
    pt = jnp.concatenate([p_mat, t_mat], axis=1).astype(BF16)
    return pt, q_mat.astype(BF16), a_r, a_i, d_row


def _s5_kernel(ua_ref, ub_ref, pt_ref, q_ref, ar_ref, ai_ref, d_ref, wg_ref, o_ref,
               sr_ref, si_ref, u8_ref, es_ref, ot_ref):
    nb = ua_ref.shape[0]
    ncl = S5_TOK // S5_L
    rows = nb * ncl
    nlb = S5_NSTATE // LANES
    hlb = nlb // 2

    @pl.when(pl.program_id(0) == 0)
    def _():
        sr_ref[...] = jnp.zeros_like(sr_ref)
        si_ref[...] = jnp.zeros_like(si_ref)

    for t in range(S5_L):
        for hh, u_ref in enumerate((ua_ref, ub_ref)):
            blk = u_ref[:, pl.ds(t, ncl, stride=S5_L), :]
            lo = t * D_S5 + hh * LANES
            u8_ref[:, lo:lo + LANES] = blk.reshape(rows, LANES)

    u8 = u8_ref[...]
    u8b = u8.astype(BF16)
    e = jnp.dot(u8b, pt_ref[:, :S5_NSTATE], preferred_element_type=F32)
    for j in range(nlb):
        es_ref[j] = e[:, j * LANES:(j + 1) * LANES]
    y = jnp.dot(u8b, pt_ref[:, S5_NSTATE:], preferred_element_type=F32) + d_ref[...] * u8

    a_r = ar_ref[...]
    a_i = ai_ref[...]
    sr = sr_ref[...]
    si = si_ref[...]
    for cl in range(ncl):
        step = pl.ds(cl, nb, stride=ncl)
        e_r = es_ref[:hlb, step, :]
        e_i = es_ref[hlb:, step, :]
        es_ref[:hlb, step, :] = sr
        es_ref[hlb:, step, :] = si
        sr, si = a_r * sr - a_i * si + e_r, a_r * si + a_i * sr + e_i
    sr_ref[...] = sr
    si_ref[...] = si

    s_all = jnp.concatenate([es_ref[j] for j in range(nlb)], axis=1).astype(BF16)
    y = y + jnp.dot(s_all, q_ref[...], preferred_element_type=F32)
    y = jax.nn.gelu(y).astype(BF16)
    for t in range(S5_L):
        ag = jnp.dot(y[:, t * D_S5:(t + 1) * D_S5], wg_ref[...], preferred_element_type=F32)
        out = ag[:, :D_S5] * _sigmoid(ag[:, D_S5:])
        for hh in range(D_S5 // LANES):
            ot_ref[hh, :, pl.ds(t, ncl, stride=S5_L), :] = (
                out[:, hh * LANES:(hh + 1) * LANES].reshape(nb, ncl, LANES))
    o_ref[...] = jnp.concatenate([ot_ref[0], ot_ref[1]], axis=-1).astype(BF16)


def _s5(u, pt, q_mat, a_r, a_i, d_row, w_glu):
    nb, seq, _ = u.shape
    rows = nb * (S5_TOK // S5_L)
    nlb = S5_NSTATE // LANES
    return pl.pallas_call(
        _s5_kernel,
        out_shape=jax.ShapeDtypeStruct((nb, seq, D_S5), BF16),
        grid=(seq // S5_TOK,),
        in_specs=[pl.BlockSpec((nb, S5_TOK, LANES), lambda i: (0, i, 0)),
                  pl.BlockSpec((nb, S5_TOK, LANES), lambda i: (0, i, 1)),
                  _resident(pt.shape), _resident(q_mat.shape), _resident(a_r.shape),
                  _resident(a_i.shape), _resident(d_row.shape), _resident(w_glu.shape)],
        out_specs=pl.BlockSpec((nb, S5_TOK, D_S5), lambda i: (0, i, 0)),
        scratch_shapes=[pltpu.VMEM((nlb // 2, nb, LANES), F32),
                        pltpu.VMEM((nlb // 2, nb, LANES), F32),
                        pltpu.VMEM((rows, S5_ROW), F32),
                        pltpu.VMEM((nlb, rows, LANES), F32),
                        pltpu.VMEM((D_S5 // LANES, nb, S5_TOK, LANES), F32)],
        compiler_params=pltpu.CompilerParams(
            dimension_semantics=("arbitrary",), vmem_limit_bytes=VMEM_LIMIT),
        name="s5",
    )(u, u, pt, q_mat, a_r, a_i, d_row, w_glu)


def _attn_bias(rel_bias):
    r = jnp.arange(ATT_QB)[:, None]
    c = jnp.arange(ATT_W)[None, :] - N_LEFT * CHUNK
    rel = jnp.clip(r - c, -MAX_REL, MAX_REL) + MAX_REL
    dchunk = r // CHUNK - jnp.floor_divide(c, CHUNK)
    ok = (dchunk >= 0) & (dchunk <= N_LEFT)
    bias = rel_bias.astype(F32)[:, rel]
    bias = jnp.where(ok[None], bias, -1e30)
    return bias.reshape(HEADS // 2, 2, ATT_QB, ATT_W)


def _attn_kernel(q_ref, k_ref, v_ref, bm_ref, o_ref):
    seq = q_ref.shape[1]
    lane = lax.broadcasted_iota(jnp.int32, (1, LANES), 1)
    first = lane < HEAD_DIM

    def block(q0, k0, width):
        q2 = q_ref[0, pl.ds(q0, ATT_QB), :]
        kw = k_ref[0, pl.ds(k0, width), :]
        vw = v_ref[0, pl.ds(k0, width), :]
        out = jnp.zeros((ATT_QB, LANES), F32)
        for hh in range(2):
            sel = first if hh == 0 else jnp.logical_not(first)
            qh = jnp.where(sel, q2, jnp.zeros_like(q2))
            s = lax.dot_general(qh, kw, (((1,), (1,)), ((), ())),
                                preferred_element_type=F32)
            s = s + bm_ref[0, hh, :, ATT_W - width:]
            mx = jnp.max(s, axis=-1, keepdims=True)
            p = jnp.exp(s - mx)
            den = jnp.sum(p, axis=-1, keepdims=True)
            vh = jnp.where(sel, vw, jnp.zeros_like(vw))
            o = jnp.dot(p.astype(BF16), vh, preferred_element_type=F32)
            out = out + o / den
        o_ref[0, pl.ds(q0, ATT_QB), :] = out.astype(BF16)

    n_short = N_LEFT * CHUNK // ATT_QB
    for qb in range(n_short):
        block(qb * ATT_QB, 0, (qb + 1) * ATT_QB)

    def body(qb, carry):
        q0 = pl.multiple_of(qb * ATT_QB, ATT_QB)
        k0 = pl.multiple_of(q0 - N_LEFT * CHUNK, ATT_QB)
        block(q0, k0, ATT_W)
        return carry
    lax.fori_loop(n_short, seq // ATT_QB, body, 0)


def _attn(q, k, v, bm):
    nb, seq, _ = q.shape
    seqblk = pl.BlockSpec((1, seq, LANES), lambda p, b: (b, 0, p))
    return pl.pallas_call(
        _attn_kernel,
        out_shape=jax.ShapeDtypeStruct((nb, seq, D_ATT), BF16),
        grid=(D_ATT // LANES, nb),
        in_specs=[seqblk, seqblk, seqblk,
                  pl.BlockSpec((1, 2, ATT_QB, ATT_W), lambda p, b: (p, 0, 0, 0))],
        out_specs=seqblk,
        compiler_params=pltpu.CompilerParams(
            dimension_semantics=("arbitrary", "arbitrary"), vmem_limit_bytes=VMEM_LIMIT),
        name="attn",
    )(q, k, v, bm)


def _conv_kernel(h_ref, w_ref, b_ref, g_ref, beta_ref, o_ref, buf_ref):
    seq = h_ref.shape[1]
    buf_ref[:CONV_PAD, :] = jnp.zeros((CONV_PAD, D_CONV), F32)
    buf_ref[CONV_PAD:, :] = h_ref[0]
    shift = CONV_PAD - (CONV_WIDTH - 1)
    sub = 8

    def body(i, carry):
        r0 = pl.multiple_of(i * CONV_TR, CONV_TR)
        win = buf_ref[pl.ds(r0, CONV_TR + CONV_PAD), :]
        shifted = [win] + [win[r:r + CONV_TR + CONV_PAD - sub, :] for r in range(1, sub)]
        acc = jnp.zeros((CONV_TR, D_CONV), F32) + b_ref[...]
        for j in range(CONV_WIDTH):
            a, r = divmod(shift + j, sub)
            acc = acc + w_ref[j:j + 1, :] * shifted[r][a * sub:a * sub + CONV_TR, :]
        mu = jnp.mean(acc, axis=-1, keepdims=True)
        cen = acc - mu
        var = jnp.mean(cen * cen, axis=-1, keepdims=True)
        y = (cen * lax.rsqrt(var + EPS)) * g_ref[...] + beta_ref[...]
        o_ref[0, pl.ds(r0, CONV_TR), :] = (y * _sigmoid(y)).astype(BF16)
        return carry
    lax.fori_loop(0, seq // CONV_TR, body, 0)


def _conv(hc, w_dw, b_dw, ln_g, ln_b):
    nb, seq, _ = hc.shape
    blk = pl.BlockSpec((1, seq, D_CONV), lambda b: (b, 0, 0))
    return pl.pallas_call(
        _conv_kernel,
        out_shape=jax.ShapeDtypeStruct((nb, seq, D_CONV), BF16),
        grid=(nb,),
        in_specs=[blk, _resident((CONV_WIDTH, D_CONV)), _resident((1, D_CONV)),
                  _resident((1, D_CONV)), _resident((1, D_CONV))],
        out_specs=blk,
        scratch_shapes=[pltpu.VMEM((seq + CONV_PAD, D_CONV), F32)],
        compiler_params=pltpu.CompilerParams(
            dimension_semantics=("arbitrary",), vmem_limit_bytes=VMEM_LIMIT),
        name="conv",
    )(hc, w_dw, b_dw, ln_g, ln_b)


def _merge_kernel(x_ref, s5_ref, at_ref, cv_ref, g_ref, wg_ref, bg_ref,
                  ws_ref, wa_ref, wc_ref, wo_ref, o_ref):
    x = x_ref[...]
    h = _rmsnorm_bf16(x, g_ref[...])
    branches = ((s5_ref, ws_ref), (at_ref, wa_ref), (cv_ref, wc_ref))
    merged = jnp.zeros_like(x)
    for i, (br_ref, w_ref) in enumerate(branches):
        cols = slice(i * D_MODEL, (i + 1) * D_MODEL)
        logits = jnp.dot(h, wg_ref[:, cols], preferred_element_type=F32) + bg_ref[:, cols]
        y = jnp.dot(br_ref[...], w_ref[...], preferred_element_type=F32)
        merged = merged + _sigmoid(logits) * y
    o_ref[...] = x + jnp.dot(merged.astype(BF16), wo_ref[...], preferred_element_type=F32)


def _merge(x, s5o, ato, cvo, g, w_gate, b_gate, w_s5, w_at, w_cv, w_out):
    n = x.shape[0]
    def tok(width):
        return pl.BlockSpec((TM, width), lambda i: (i, 0))
    return pl.pallas_call(
        _merge_kernel,
        out_shape=jax.ShapeDtypeStruct((n, D_MODEL), F32),
        grid=(n // TM,),
        in_specs=[tok(D_MODEL), tok(D_S5), tok(D_ATT), tok(D_CONV),
                  _resident((1, D_MODEL)), _resident((D_MODEL, 3 * D_MODEL)),
                  _resident((1, 3 * D_MODEL)), _resident((D_S5, D_MODEL)),
                  _resident((D_ATT, D_MODEL)), _resident((D_CONV, D_MODEL)),
                  _resident((D_MODEL, D_MODEL))],
        out_specs=tok(D_MODEL),
        compiler_params=pltpu.CompilerParams(
            dimension_semantics=("arbitrary",), vmem_limit_bytes=VMEM_LIMIT),
        name="merge",
    )(x, s5o, ato, cvo, g, w_gate, b_gate, w_s5, w_at, w_cv, w_out)


def kernel(x, ffn1_norm, ffn1_w_up, ffn1_w_down, mix_norm, w_in, b_gate, s5_lambda_re, s5_lambda_im, s5_log_dt, s5_b_re, s5_b_im, s5_c_re, s5_c_im, s5_d, s5_w_glu, w_br_s5, attn_q_gain, attn_k_gain, attn_rel_bias, w_br_attn, conv_w_dw, conv_b_dw, conv_ln_g, conv_ln_b, w_br_conv, w_out, ffn2_norm, ffn2_w_up, ffn2_w_down):
    nb, seq, d = x.shape
    n = nb * seq
    depth = ffn1_norm.shape[0]
    xt = x.reshape(n, d)
    row = lambda v: v.reshape(1, -1).astype(F32)
    for l in range(depth):
        xt = _ffn(xt, row(ffn1_norm[l]), ffn1_w_up[l, :, :D_FF].astype(BF16),
                  ffn1_w_up[l, :, D_FF:].astype(BF16), ffn1_w_down[l].astype(BF16))

        qg2 = jnp.tile(row(attn_q_gain[l]), (1, LANES // HEAD_DIM))
        kg2 = jnp.tile(row(attn_k_gain[l]), (1, LANES // HEAD_DIM))
        u, q, k, v, hc = _inproj(xt, row(mix_norm[l]), w_in[l, :, :N_SMALL].astype(BF16), qg2, kg2)

        pt, q_mat, a_r, a_i, d_row = _s5_tables(
            s5_lambda_re[l], s5_lambda_im[l], s5_log_dt[l], s5_b_re[l], s5_b_im[l],
            s5_c_re[l], s5_c_im[l], s5_d[l])
        s5o = _s5(u.reshape(nb, seq, D_S5), pt, q_mat, a_r, a_i, d_row, s5_w_glu[l].astype(BF16))

        ato = _attn(q.reshape(nb, seq, D_ATT), k.reshape(nb, seq, D_ATT),
                    v.reshape(nb, seq, D_ATT), _attn_bias(attn_rel_bias[l]))

        cvo = _conv(hc.reshape(nb, seq, D_CONV), conv_w_dw[l].astype(F32), row(conv_b_dw[l]),
                    row(conv_ln_g[l]), row(conv_ln_b[l]))

        xt = _merge(xt, s5o.reshape(n, D_S5), ato.reshape(n, D_ATT), cvo.reshape(n, D_CONV),
                    row(mix_norm[l]), w_in[l, :, N_SMALL:].astype(BF16), row(b_gate[l]),
                    w_br_s5[l].astype(BF16), w_br_attn[l].astype(BF16),
                    w_br_conv[l].astype(BF16), w_out[l].astype(BF16))

        xt = _ffn(xt, row(ffn2_norm[l]), ffn2_w_up[l, :, :D_FF].astype(BF16),
                  ffn2_w_up[l, :, D_FF:].astype(BF16), ffn2_w_down[l].astype(BF16))
    return xt.reshape(nb, seq, d)
```

```python
import functools
import math

import jax
import jax.numpy as jnp
from jax import lax
from jax.experimental import pallas as pl
from jax.experimental.pallas import tpu as pltpu

F32 = jnp.float32
BF16 = jnp.bfloat16

D_MODEL = 1024
D_S5 = 256
S5_GROUP = 16
S5_GROUPS = 16
S5_STATE = 64
D_ATT = 512
HEAD_DIM = 64
HEADS = 8
CHUNK = 64
N_LEFT = 8
MAX_REL = 128
D_CONV = 256
CONV_WIDTH = 31
D_FF = 2816
EPS = 1e-6
LOG2E = math.log2(math.e)
N_SMALL = D_S5 + 3 * D_ATT + 2 * D_CONV

LANES = 128
MXU_DIM = 256
VMEM_LIMIT = 56 * 1024 * 1024

TM = 512
S5_L = 8
S5_ROW = S5_L * D_S5
S5_NSTATE = 2 * S5_GROUPS * S5_STATE
S5_TOK = 256
ATT_QB = 4 * CHUNK
ATT_W = ATT_QB + N_LEFT * CHUNK
CONV_TR = 128
CONV_NORM_TR = 512
CONV_PAD = 32


def _resident(shape):
    nd = len(shape)
    return pl.BlockSpec(shape, lambda *_: (0,) * nd, pipeline_mode=pl.Buffered(1))


def _rmsnorm_bf16(x, g):
    ms = jnp.mean(x * x, axis=-1, keepdims=True)
    return ((x * lax.rsqrt(ms + EPS)) * g).astype(BF16)


def _sigmoid(x):
    return jax.nn.sigmoid(x)


def _ffn_kernel(x_ref, g_ref, wa_ref, wb_ref, wd_ref, o_ref, *, bounds):
    x = x_ref[...]
    h = _rmsnorm_bf16(x, g_ref[...])
    y = jnp.zeros_like(x)
    for f0, f1 in zip(bounds[:-1], bounds[1:]):
        a = jnp.dot(h, wa_ref[:, f0:f1], preferred_element_type=F32)
        b = jnp.dot(h, wb_ref[:, f0:f1], preferred_element_type=F32)
        act = ((a * _sigmoid(a)) * b).astype(BF16)
        y = y + jnp.dot(act, wd_ref[f0:f1, :], preferred_element_type=F32)
    o_ref[...] = x + 0.5 * y


def _ffn_bounds():
    tiles = D_FF // MXU_DIM
    assert tiles * MXU_DIM == D_FF
    return (0, (tiles + 1) // 2 * MXU_DIM, D_FF)


def _ffn(x, g, wa, wb, wd):
    n = x.shape[0]
    tok = pl.BlockSpec((TM, D_MODEL), lambda i: (i, 0))
    return pl.pallas_call(
        functools.partial(_ffn_kernel, bounds=_ffn_bounds()),
        out_shape=jax.ShapeDtypeStruct((n, D_MODEL), F32),
        grid=(n // TM,),
        in_specs=[tok, _resident((1, D_MODEL)), _resident((D_MODEL, D_FF)),
                  _resident((D_MODEL, D_FF)), _resident((D_FF, D_MODEL))],
        out_specs=tok,
        compiler_params=pltpu.CompilerParams(
            dimension_semantics=("arbitrary",), vmem_limit_bytes=VMEM_LIMIT),
        name="ffn",
    )(x, g, wa, wb, wd)


def _head_pair_norm(x2, gain2):
    lane = lax.broadcasted_iota(jnp.int32, (1, LANES), 1)
    first = lane < HEAD_DIM
    sq = x2 * x2
    s_a = jnp.sum(jnp.where(first, sq, 0.0), axis=-1, keepdims=True)
    s_b = jnp.sum(jnp.where(first, 0.0, sq), axis=-1, keepdims=True)
    ms = jnp.where(first, s_a, s_b) * (1.0 / HEAD_DIM)
    return (x2 * lax.rsqrt(ms + EPS)) * gain2


def _inproj_kernel(x_ref, g_ref, w_ref, qg_ref, kg_ref, u_ref, q_ref, k_ref, v_ref, hc_ref):
    h = _rmsnorm_bf16(x_ref[...], g_ref[...])
    proj = jnp.dot(h, w_ref[...], preferred_element_type=F32)
    u_ref[...] = proj[:, :D_S5]
    q0, k0, v0, z0 = D_S5, D_S5 + D_ATT, D_S5 + 2 * D_ATT, D_S5 + 3 * D_ATT
    scale = HEAD_DIM ** -0.5 * LOG2E
    for p in range(D_ATT // LANES):
        lo = p * LANES
        qn = _head_pair_norm(proj[:, q0 + lo:q0 + lo + LANES], qg_ref[...])
        kn = _head_pair_norm(proj[:, k0 + lo:k0 + lo + LANES], kg_ref[...])
        q_ref[:, lo:lo + LANES] = (qn * scale).astype(BF16)
        k_ref[:, lo:lo + LANES] = kn.astype(BF16)
    v_ref[...] = proj[:, v0:v0 + D_ATT].astype(BF16)
    a = proj[:, z0:z0 + D_CONV]
    gt = proj[:, z0 + D_CONV:z0 + 2 * D_CONV]
    hc_ref[...] = a * _sigmoid(gt)


def _inproj(x, g, w, qg2, kg2):
    n = x.shape[0]
    def tok(width):
        return pl.BlockSpec((TM, width), lambda i: (i, 0))
    return pl.pallas_call(
        _inproj_kernel,
        out_shape=(jax.ShapeDtypeStruct((n, D_S5), F32),
                   jax.ShapeDtypeStruct((n, D_ATT), BF16),
                   jax.ShapeDtypeStruct((n, D_ATT), BF16),
                   jax.ShapeDtypeStruct((n, D_ATT), BF16),
                   jax.ShapeDtypeStruct((n, D_CONV), F32)),
        grid=(n // TM,),
        in_specs=[tok(D_MODEL), _resident((1, D_MODEL)), _resident((D_MODEL, N_SMALL)),
                  _resident((1, LANES)), _resident((1, LANES))],
        out_specs=(tok(D_S5), tok(D_ATT), tok(D_ATT), tok(D_ATT), tok(D_CONV)),
        compiler_params=pltpu.CompilerParams(
            dimension_semantics=("arbitrary",), vmem_limit_bytes=VMEM_LIMIT),
        name="inproj",
    )(x, g, w, qg2, kg2)


def _s5_tables(lambda_re, lambda_im, log_dt, b_re, b_im, c_re, c_im, d_skip):
    lr = jnp.minimum(lambda_re.astype(F32), -1e-4)
    li = lambda_im.astype(F32)
    dt = jnp.exp(log_dt.astype(F32))[:, None]
    mag = jnp.exp(lr * dt)
    ar = mag * jnp.cos(li * dt)
    ai = mag * jnp.sin(li * dt)
    den = lr * lr + li * li
    coef_r = ((ar - 1.0) * lr + ai * li) / den
    coef_i = (ai * lr - (ar - 1.0) * li) / den
    br = b_re.astype(F32)
    bi = b_im.astype(F32)
    bbar_r = coef_r[..., None] * br - coef_i[..., None] * bi
    bbar_i = coef_r[..., None] * bi + coef_i[..., None] * br
    cr = c_re.astype(F32)
    ci = c_im.astype(F32)

    pr, pi = [jnp.ones_like(ar)], [jnp.zeros_like(ai)]
    for _ in range(S5_L):
        r, i = pr[-1], pi[-1]
        pr.append(r * ar - i * ai)
        pi.append(r * ai + i * ar)
    pr = jnp.stack(pr)
    pi = jnp.stack(pi)

    eye_g = jnp.eye(S5_GROUPS, dtype=F32)

    pw_r = pr[S5_L - 1::-1][:S5_L]
    pw_i = pi[S5_L - 1::-1][:S5_L]
    e_r = pw_r[..., None] * bbar_r[None] - pw_i[..., None] * bbar_i[None]
    e_i = pw_r[..., None] * bbar_i[None] + pw_i[..., None] * bbar_r[None]
    def to_p(e):
        e = jnp.transpose(e, (0, 1, 3, 2))
        return e[:, :, :, None, :] * eye_g[None, :, None, :, None]
    p_mat = jnp.stack([to_p(e_r), to_p(e_i)], axis=3)
    p_mat = p_mat.reshape(S5_ROW, S5_NSTATE)

    m_r = pr[:S5_L, :, None, :] * cr[None] - pi[:S5_L, :, None, :] * ci[None]
    m_i = pr[:S5_L, :, None, :] * ci[None] + pi[:S5_L, :, None, :] * cr[None]
    kern = (jnp.sum(m_r[:, :, :, :, None] * bbar_r[None, :, None, :, :], axis=3)
            - jnp.sum(m_i[:, :, :, :, None] * bbar_i[None, :, None, :, :], axis=3))
    t_idx = jnp.arange(S5_L)
    lag = t_idx[None, :] - t_idx[:, None]
    k_st = jnp.where((lag >= 0)[:, :, None, None, None],
                     kern[jnp.clip(lag, 0, S5_L - 1)], 0.0)
    t_mat = jnp.transpose(k_st, (0, 2, 4, 1, 3))
    t_mat = t_mat[:, :, :, :, None, :] * eye_g[None, :, None, None, :, None]
    t_mat = t_mat.reshape(S5_ROW, S5_ROW)

    q_r = pr[1:, :, None, :] * cr[None] - pi[1:, :, None, :] * ci[None]
    q_i = -(pr[1:, :, None, :] * ci[None] + pi[1:, :, None, :] * cr[None])
    def to_q(m):
        m = jnp.transpose(m, (1, 3, 0, 2))
        return m[:, :, :, None, :] * eye_g[:, None, None, :, None]
    q_mat = jnp.stack([to_q(q_r), to_q(q_i)], axis=0).reshape(S5_NSTATE, S5_ROW)

    half_blocks = S5_NSTATE // 2 // LANES
    a_r = pr[S5_L].reshape(half_blocks, 1, LANES)
    a_i = pi[S5_L].reshape(half_blocks, 1, LANES)
    d_row = jnp.tile(d_skip.astype(F32).reshape(1, D_S5), (1, S5_L))
    pt = jnp.concatenate([p_mat, t_mat], axis=1).astype(BF16)
    return pt, q_mat.astype(BF16), a_r, a_i, d_row


def _s5_kernel(ua_ref, ub_ref, pt_ref, q_ref, ar_ref, ai_ref, d_ref, wg_ref, o_ref,
               sr_ref, si_ref, u8_ref, es_ref, ot_ref):
    nb = ua_ref.shape[0]
    ncl = S5_TOK // S5_L
    rows = nb * ncl
    nlb = S5_NSTATE // LANES
    hlb = nlb // 2

    @pl.when(pl.program_id(0) == 0)
    def _():
        sr_ref[...] = jnp.zeros_like(sr_ref)
        si_ref[...] = jnp.zeros_like(si_ref)

    sub = 8
    ncg = ncl // sub
    grp = nb * sub
    for t in range(S5_L):
        for hh, u_ref in enumerate((ua_ref, ub_ref)):
            lo = t * D_S5 + hh * LANES
            for cg in range(ncg):
                blk = u_ref[:, pl.ds(cg * sub * S5_L + t, sub, stride=S5_L), :]
                u8_ref[cg * grp:(cg + 1) * grp, lo:lo + LANES] = blk.reshape(grp, LANES)

    u8 = u8_ref[...]
    u8b = u8.astype(BF16)
    e = jnp.dot(u8b, pt_ref[:, :S5_NSTATE], preferred_element_type=F32)
    for j in range(nlb):
        es_ref[j] = e[:, j * LANES:(j + 1) * LANES]
    y = jnp.dot(u8b, pt_ref[:, S5_NSTATE:], preferred_element_type=F32) + d_ref[...] * u8

    a_r = ar_ref[...]
    a_i = ai_ref[...]
    sr = sr_ref[...]
    si = si_ref[...]
    for cl in range(ncl):
        step = pl.ds((cl // sub) * grp + cl % sub, nb, stride=sub)
        e_r = es_ref[:hlb, step, :]
        e_i = es_ref[hlb:, step, :]
        es_ref[:hlb, step, :] = sr
        es_ref[hlb:, step, :] = si
        sr, si = a_r * sr - a_i * si + e_r, a_r * si + a_i * sr + e_i
    sr_ref[...] = sr
    si_ref[...] = si

    s_all = jnp.concatenate([es_ref[j] for j in range(nlb)], axis=1).astype(BF16)
    y = y + jnp.dot(s_all, q_ref[...], preferred_element_type=F32)
    y = jax.nn.gelu(y).astype(BF16)
    for t in range(S5_L):
        ag = jnp.dot(y[:, t * D_S5:(t + 1) * D_S5], wg_ref[...], preferred_element_type=F32)
        out = ag[:, :D_S5] * _sigmoid(ag[:, D_S5:])
        for hh in range(D_S5 // LANES):
            for cg in range(ncg):
                ot_ref[hh, :, pl.ds(cg * sub * S5_L + t, sub, stride=S5_L), :] = (
                    out[cg * grp:(cg + 1) * grp, hh * LANES:(hh + 1) * LANES].reshape(nb, sub, LANES))
    o_ref[...] = jnp.concatenate([ot_ref[0], ot_ref[1]], axis=-1).astype(BF16)


def _s5(u, pt, q_mat, a_r, a_i, d_row, w_glu):
    nb, seq, _ = u.shape
    rows = nb * (S5_TOK // S5_L)
    nlb = S5_NSTATE // LANES
    return pl.pallas_call(
        _s5_kernel,
        out_shape=jax.ShapeDtypeStruct((nb, seq, D_S5), BF16),
        grid=(seq // S5_TOK,),
        in_specs=[pl.BlockSpec((nb, S5_TOK, LANES), lambda i: (0, i, 0)),
                  pl.BlockSpec((nb, S5_TOK, LANES), lambda i: (0, i, 1)),
                  _resident(pt.shape), _resident(q_mat.shape), _resident(a_r.shape),
                  _resident(a_i.shape), _resident(d_row.shape), _resident(w_glu.shape)],
        out_specs=pl.BlockSpec((nb, S5_TOK, D_S5), lambda i: (0, i, 0)),
        scratch_shapes=[pltpu.VMEM((nlb // 2, nb, LANES), F32),
                        pltpu.VMEM((nlb // 2, nb, LANES), F32),
                        pltpu.VMEM((rows, S5_ROW), F32),
                        pltpu.VMEM((nlb, rows, LANES), F32),
                        pltpu.VMEM((D_S5 // LANES, nb, S5_TOK, LANES), F32)],
        compiler_params=pltpu.CompilerParams(
            dimension_semantics=("arbitrary",), vmem_limit_bytes=VMEM_LIMIT),
        name="s5",
    )(u, u, pt, q_mat, a_r, a_i, d_row, w_glu)


def _attn_bias(rel_bias):
    rb = rel_bias.astype(F32) * LOG2E
    far_past, far_future = rb[:, 2 * MAX_REL:], rb[:, :1]
    period = ATT_QB + ATT_W
    n_const = N_LEFT * CHUNK - MAX_REL
    vec = jnp.concatenate([
        jnp.broadcast_to(far_past, (HEADS, n_const)),
        rb[:, ::-1],
        jnp.broadcast_to(far_future, (HEADS, ATT_W - n_const - 2 * MAX_REL - 1)),
        jnp.broadcast_to(far_past, (HEADS, ATT_QB)),
    ], axis=1)
    toep = jnp.tile(vec, (1, ATT_QB))[:, :ATT_QB * (period - 1)]
    toep = toep.reshape(HEADS, ATT_QB, period - 1)[:, :, :ATT_W]
    r = jnp.arange(ATT_QB)[:, None]
    c = jnp.arange(ATT_W)[None, :] - N_LEFT * CHUNK
    dchunk = r // CHUNK - jnp.floor_divide(c, CHUNK)
    ok = (dchunk >= 0) & (dchunk <= N_LEFT)
    bias = jnp.where(ok[None], toep, -1e30)
    return bias.reshape(HEADS // 2, 2, ATT_QB, ATT_W)


def _attn_kernel(q_ref, k_ref, v_ref, bm_ref, o_ref):
    seq = q_ref.shape[1]
    lane = lax.broadcasted_iota(jnp.int32, (1, LANES), 1)
    first = lane < HEAD_DIM

    def block(q0, k0, width):
        for pr in range(D_ATT // LANES):
            cols = slice(pr * LANES, (pr + 1) * LANES)
            q2 = q_ref[0, pl.ds(q0, ATT_QB), cols]
            kw = k_ref[0, pl.ds(k0, width), cols]
            vw = v_ref[0, pl.ds(k0, width), cols]
            outs = []
            for hh in range(2):
                sel = first if hh == 0 else jnp.logical_not(first)
                qh = jnp.where(sel, q2, jnp.zeros_like(q2))
                s = lax.dot_general(qh, kw, (((1,), (1,)), ((), ())),
                                    preferred_element_type=F32)
                s = s + bm_ref[pr, hh, :, ATT_W - width:]
                mx = jnp.max(s, axis=-1, keepdims=True)
                p = jnp.exp2(s - mx).astype(BF16)
                vh = jnp.where(sel, vw, jnp.ones_like(vw))
                outs.append(jnp.dot(p, vh, preferred_element_type=F32))
            num = jnp.where(first, outs[0], outs[1])
            den = pltpu.roll(jnp.where(first, outs[1], outs[0]), HEAD_DIM, axis=1)
            o_ref[0, pl.ds(q0, ATT_QB), cols] = (num / den).astype(BF16)

    n_short = N_LEFT * CHUNK // ATT_QB
    for qb in range(n_short):
        block(qb * ATT_QB, 0, (qb + 1) * ATT_QB)

    def body(qb, carry):
        q0 = pl.multiple_of(qb * ATT_QB, ATT_QB)
        k0 = pl.multiple_of(q0 - N_LEFT * CHUNK, ATT_QB)
        block(q0, k0, ATT_W)
        return carry
    lax.fori_loop(n_short, seq // ATT_QB, body, 0)


def _attn(q, k, v, bm):
    nb, seq, _ = q.shape
    seqblk = pl.BlockSpec((1, seq, D_ATT), lambda b: (b, 0, 0))
    return pl.pallas_call(
        _attn_kernel,
        out_shape=jax.ShapeDtypeStruct((nb, seq, D_ATT), BF16),
        grid=(nb,),
        in_specs=[seqblk, seqblk, seqblk, _resident(bm.shape)],
        out_specs=seqblk,
        compiler_params=pltpu.CompilerParams(
            dimension_semantics=("arbitrary",), vmem_limit_bytes=VMEM_LIMIT),
        name="attn",
    )(q, k, v, bm)


def _conv_kernel(h_ref, w_ref, b_ref, g_ref, beta_ref, o_ref, buf_ref, cv_ref):
    seq = h_ref.shape[1]
    nh = D_CONV // LANES
    ntile = seq // CONV_TR
    for hh in range(nh):
        buf_ref[hh, :CONV_PAD, :] = jnp.zeros((CONV_PAD, LANES), F32)
        buf_ref[hh, CONV_PAD:, :] = h_ref[0, :, hh * LANES:(hh + 1) * LANES]
    shift = CONV_PAD - (CONV_WIDTH - 1)

    def taps(i, carry):
        hh = i // ntile
        r0 = pl.multiple_of((i - hh * ntile) * CONV_TR, CONV_TR)
        acc = jnp.zeros((CONV_TR, LANES), F32) + b_ref[hh]
        for j in range(CONV_WIDTH):
            acc = acc + w_ref[hh, j:j + 1, :] * buf_ref[hh, pl.ds(r0 + shift + j, CONV_TR), :]
        cv_ref[hh, pl.ds(r0, CONV_TR), :] = acc
        return carry
    lax.fori_loop(0, nh * ntile, taps, 0)

    def norm(i, carry):
        r0 = pl.multiple_of(i * CONV_NORM_TR, CONV_NORM_TR)
        acc = jnp.concatenate([cv_ref[hh, pl.ds(r0, CONV_NORM_TR), :] for hh in range(nh)], axis=-1)
        mu = jnp.mean(acc, axis=-1, keepdims=True)
        cen = acc - mu
        var = jnp.mean(cen * cen, axis=-1, keepdims=True)
        y = (cen * lax.rsqrt(var + EPS)) * g_ref[...] + beta_ref[...]
        o_ref[0, pl.ds(r0, CONV_NORM_TR), :] = (y * _sigmoid(y)).astype(BF16)
        return carry
    lax.fori_loop(0, seq // CONV_NORM_TR, norm, 0)


def _conv(hc, w_dw, b_dw, ln_g, ln_b):
    nb, seq, _ = hc.shape
    nh = D_CONV // LANES
    blk = pl.BlockSpec((1, seq, D_CONV), lambda b: (b, 0, 0))
    return pl.pallas_call(
        _conv_kernel,
        out_shape=jax.ShapeDtypeStruct((nb, seq, D_CONV), BF16),
        grid=(nb,),
        in_specs=[blk, _resident((nh, CONV_WIDTH, LANES)), _resident((nh, 1, LANES)),
                  _resident((1, D_CONV)), _resident((1, D_CONV))],
        out_specs=blk,
        scratch_shapes=[pltpu.VMEM((nh, seq + CONV_PAD, LANES), F32),
                        pltpu.VMEM((nh, seq, LANES), F32)],
        compiler_params=pltpu.CompilerParams(
            dimension_semantics=("arbitrary",), vmem_limit_bytes=VMEM_LIMIT),
        name="conv",
    )(hc, w_dw, b_dw, ln_g, ln_b)


def _merge_kernel(x_ref, s5_ref, at_ref, cv_ref, g_ref, wg_ref, bg_ref,
                  ws_ref, wa_ref, wc_ref, wo_ref, o_ref):
    x = x_ref[...]
    h = _rmsnorm_bf16(x, g_ref[...])
    branches = ((s5_ref, ws_ref), (at_ref, wa_ref), (cv_ref, wc_ref))
    merged = jnp.zeros_like(x)
    for i, (br_ref, w_ref) in enumerate(branches):
        cols = slice(i * D_MODEL, (i + 1) * D_MODEL)
        logits = jnp.dot(h, wg_ref[:, cols], preferred_element_type=F32) + bg_ref[:, cols]
        y = jnp.dot(br_ref[...], w_ref[...], preferred_element_type=F32)
        merged = merged + _sigmoid(logits) * y
    o_ref[...] = x + jnp.dot(merged.astype(BF16), wo_ref[...], preferred_element_type=F32)


def _merge(x, s5o, ato, cvo, g, w_gate, b_gate, w_s5, w_at, w_cv, w_out):
    n = x.shape[0]
    def tok(width):
        return pl.BlockSpec((TM, width), lambda i: (i, 0))
    return pl.pallas_call(
        _merge_kernel,
        out_shape=jax.ShapeDtypeStruct((n, D_MODEL), F32),
        grid=(n // TM,),
        in_specs=[tok(D_MODEL), tok(D_S5), tok(D_ATT), tok(D_CONV),
                  _resident((1, D_MODEL)), _resident((D_MODEL, 3 * D_MODEL)),
                  _resident((1, 3 * D_MODEL)), _resident((D_S5, D_MODEL)),
                  _resident((D_ATT, D_MODEL)), _resident((D_CONV, D_MODEL)),
                  _resident((D_MODEL, D_MODEL))],
        out_specs=tok(D_MODEL),
        compiler_params=pltpu.CompilerParams(
            dimension_semantics=("arbitrary",), vmem_limit_bytes=VMEM_LIMIT),
        name="merge",
    )(x, s5o, ato, cvo, g, w_gate, b_gate, w_s5, w_at, w_cv, w_out)


def kernel(x, ffn1_norm, ffn1_w_up, ffn1_w_down, mix_norm, w_in, b_gate, s5_lambda_re, s5_lambda_im, s5_log_dt, s5_b_re, s5_b_im, s5_c_re, s5_c_im, s5_d, s5_w_glu, w_br_s5, attn_q_gain, attn_k_gain, attn_rel_bias, w_br_attn, conv_w_dw, conv_b_dw, conv_ln_g, conv_ln_b, w_br_conv, w_out, ffn2_norm, ffn2_w_up, ffn2_w_down):
    nb, seq, d = x.shape
    n = nb * seq
    depth = ffn1_norm.shape[0]
    xt = x.reshape(n, d)
    row = lambda v: v.reshape(1, -1).astype(F32)
    for l in range(depth):
        xt = _ffn(xt, row(ffn1_norm[l]), ffn1_w_up[l, :, :D_FF].astype(BF16),
                  ffn1_w_up[l, :, D_FF:].astype(BF16), ffn1_w_down[l].astype(BF16))

        qg2 = jnp.tile(row(attn_q_gain[l]), (1, LANES // HEAD_DIM))
        kg2 = jnp.tile(row(attn_k_gain[l]), (1, LANES // HEAD_DIM))
        u, q, k, v, hc = _inproj(xt, row(mix_norm[l]), w_in[l, :, :N_SMALL].astype(BF16), qg2, kg2)

        pt, q_mat, a_r, a_i, d_row = _s5_tables(
            s5_lambda_re[l], s5_lambda_im[l], s5_log_dt[l], s5_b_re[l], s5_b_im[l],
            s5_c_re[l], s5_c_im[l], s5_d[l])
        s5o = _s5(u.reshape(nb, seq, D_S5), pt, q_mat, a_r, a_i, d_row, s5_w_glu[l].astype(BF16))

        ato = _attn(q.reshape(nb, seq, D_ATT), k.reshape(nb, seq, D_ATT),
                    v.reshape(nb, seq, D_ATT), _attn_bias(attn_rel_bias[l]))

        nh = D_CONV // LANES
        w_dw = conv_w_dw[l].astype(F32).reshape(CONV_WIDTH, nh, LANES).transpose(1, 0, 2)
        cvo = _conv(hc.reshape(nb, seq, D_CONV), w_dw, conv_b_dw[l].astype(F32).reshape(nh, 1, LANES),
                    row(conv_ln_g[l]), row(conv_ln_b[l]))

        xt = _merge(xt, s5o.reshape(n, D_S5), ato.reshape(n, D_ATT), cvo.reshape(n, D_CONV),
                    row(mix_norm[l]), w_in[l, :, N_SMALL:].astype(BF16), row(b_gate[l]),
                    w_br_s5[l].astype(BF16), w_br_attn[l].astype(BF16),
                    w_br_conv[l].astype(BF16), w_out[l].astype(BF16))

        xt = _ffn(xt, row(ffn2_norm[l]), ffn2_w_up[l, :, :D_FF].astype(BF16),
                  ffn2_w_up[l, :, D_FF:].astype(BF16), ffn2_w_down[l].astype(BF16))
    return xt.reshape(nb, seq, d)
```

```python
import functools
import math

import jax
import jax.numpy as jnp
from jax import lax
from jax.experimental import pallas as pl
from jax.experimental.pallas import tpu as pltpu

F32 = jnp.float32
BF16 = jnp.bfloat16

D_MODEL = 1024
D_S5 = 256
S5_GROUP = 16
S5_GROUPS = 16
S5_STATE = 64
D_ATT = 512
HEAD_DIM = 64
HEADS = 8
CHUNK = 64
N_LEFT = 8
MAX_REL = 128
D_CONV = 256
CONV_WIDTH = 31
D_FF = 2816
EPS = 1e-6
LOG2E = math.log2(math.e)
N_SMALL = D_S5 + 3 * D_ATT + 2 * D_CONV

LANES = 128
MXU_DIM = 256
VMEM_LIMIT = 56 * 1024 * 1024

TM = 512
S5_L = 8
S5_ROW = S5_L * D_S5
S5_NSTATE = 2 * S5_GROUPS * S5_STATE
S5_TOK = 256
ATT_QB = 4 * CHUNK
ATT_W = ATT_QB + N_LEFT * CHUNK
ATT_KT = MXU_DIM
CONV_TR = 128
CONV_NORM_TR = 512
CONV_PAD = 32


def _resident(shape):
    nd = len(shape)
    return pl.BlockSpec(shape, lambda *_: (0,) * nd, pipeline_mode=pl.Buffered(1))


def _rmsnorm_bf16(x, g):
    ms = jnp.mean(x * x, axis=-1, keepdims=True)
    return ((x * lax.rsqrt(ms + EPS)) * g).astype(BF16)


def _sigmoid(x):
    return jax.nn.sigmoid(x)


def _ffn_kernel(x_ref, g_ref, wa_ref, wb_ref, wd_ref, o_ref, *, bounds):
    x = x_ref[...]
    h = _rmsnorm_bf16(x, g_ref[...])
    y = jnp.zeros_like(x)
    for f0, f1 in zip(bounds[:-1], bounds[1:]):
        a = jnp.dot(h, wa_ref[:, f0:f1], preferred_element_type=F32)
        b = jnp.dot(h, wb_ref[:, f0:f1], preferred_element_type=F32)
        act = ((a * _sigmoid(a)) * b).astype(BF16)
        y = y + jnp.dot(act, wd_ref[f0:f1, :], preferred_element_type=F32)
    o_ref[...] = x + 0.5 * y


def _ffn_bounds():
    tiles = D_FF // MXU_DIM
    assert tiles * MXU_DIM == D_FF
    return (0, (tiles + 1) // 2 * MXU_DIM, D_FF)


def _ffn(x, g, wa, wb, wd):
    n = x.shape[0]
    tok = pl.BlockSpec((TM, D_MODEL), lambda i: (i, 0))
    return pl.pallas_call(
        functools.partial(_ffn_kernel, bounds=_ffn_bounds()),
        out_shape=jax.ShapeDtypeStruct((n, D_MODEL), F32),
        grid=(n // TM,),
        in_specs=[tok, _resident((1, D_MODEL)), _resident((D_MODEL, D_FF)),
                  _resident((D_MODEL, D_FF)), _resident((D_FF, D_MODEL))],
        out_specs=tok,
        compiler_params=pltpu.CompilerParams(
            dimension_semantics=("arbitrary",), vmem_limit_bytes=VMEM_LIMIT),
        name="ffn",
    )(x, g, wa, wb, wd)


def _head_pair_norm(x2, gain2):
    lane = lax.broadcasted_iota(jnp.int32, (1, LANES), 1)
    first = lane < HEAD_DIM
    sq = x2 * x2
    s_a = jnp.sum(jnp.where(first, sq, 0.0), axis=-1, keepdims=True)
    s_b = jnp.sum(jnp.where(first, 0.0, sq), axis=-1, keepdims=True)
    ms = jnp.where(first, s_a, s_b) * (1.0 / HEAD_DIM)
    return (x2 * lax.rsqrt(ms + EPS)) * gain2


def _inproj_kernel(x_ref, g_ref, w_ref, qg_ref, kg_ref, u_ref, q_ref, k_ref, v_ref, hc_ref):
    h = _rmsnorm_bf16(x_ref[...], g_ref[...])
    proj = jnp.dot(h, w_ref[...], preferred_element_type=F32)
    u_ref[...] = proj[:, :D_S5]
    q0, k0, v0, z0 = D_S5, D_S5 + D_ATT, D_S5 + 2 * D_ATT, D_S5 + 3 * D_ATT
    scale = HEAD_DIM ** -0.5 * LOG2E
    for p in range(D_ATT // LANES):
        lo = p * LANES
        qn = _head_pair_norm(proj[:, q0 + lo:q0 + lo + LANES], qg_ref[...])
        kn = _head_pair_norm(proj[:, k0 + lo:k0 + lo + LANES], kg_ref[...])
        q_ref[:, lo:lo + LANES] = (qn * scale).astype(BF16)
        k_ref[:, lo:lo + LANES] = kn.astype(BF16)
    v_ref[...] = proj[:, v0:v0 + D_ATT].astype(BF16)
    a = proj[:, z0:z0 + D_CONV]
    gt = proj[:, z0 + D_CONV:z0 + 2 * D_CONV]
    hc_ref[...] = a * _sigmoid(gt)


def _inproj(x, g, w, qg2, kg2):
    n = x.shape[0]
    def tok(width):
        return pl.BlockSpec((TM, width), lambda i: (i, 0))
    return pl.pallas_call(
        _inproj_kernel,
        out_shape=(jax.ShapeDtypeStruct((n, D_S5), F32),
                   jax.ShapeDtypeStruct((n, D_ATT), BF16),
                   jax.ShapeDtypeStruct((n, D_ATT), BF16),
                   jax.ShapeDtypeStruct((n, D_ATT), BF16),
                   jax.ShapeDtypeStruct((n, D_CONV), F32)),
        grid=(n // TM,),
        in_specs=[tok(D_MODEL), _resident((1, D_MODEL)), _resident((D_MODEL, N_SMALL)),
                  _resident((1, LANES)), _resident((1, LANES))],
        out_specs=(tok(D_S5), tok(D_ATT), tok(D_ATT), tok(D_ATT), tok(D_CONV)),
        compiler_params=pltpu.CompilerParams(
            dimension_semantics=("arbitrary",), vmem_limit_bytes=VMEM_LIMIT),
        name="inproj",
    )(x, g, w, qg2, kg2)


def _s5_tables(lambda_re, lambda_im, log_dt, b_re, b_im, c_re, c_im, d_skip):
    lr = jnp.minimum(lambda_re.astype(F32), -1e-4)
    li = lambda_im.astype(F32)
    dt = jnp.exp(log_dt.astype(F32))[:, None]
    mag = jnp.exp(lr * dt)
    ar = mag * jnp.cos(li * dt)
    ai = mag * jnp.sin(li * dt)
    den = lr * lr + li * li
    coef_r = ((ar - 1.0) * lr + ai * li) / den
    coef_i = (ai * lr - (ar - 1.0) * li) / den
    br = b_re.astype(F32)
    bi = b_im.astype(F32)
    bbar_r = coef_r[..., None] * br - coef_i[..., None] * bi
    bbar_i = coef_r[..., None] * bi + coef_i[..., None] * br
    cr = c_re.astype(F32)
    ci = c_im.astype(F32)

    pr, pi = [jnp.ones_like(ar)], [jnp.zeros_like(ai)]
    for _ in range(S5_L):
        r, i = pr[-1], pi[-1]
        pr.append(r * ar - i * ai)
        pi.append(r * ai + i * ar)
    pr = jnp.stack(pr)
    pi = jnp.stack(pi)

    pw_r = pr[S5_L - 1::-1]
    pw_i = pi[S5_L - 1::-1]
    e_r = pw_r[..., None] * bbar_r[None] - pw_i[..., None] * bbar_i[None]
    e_i = pw_r[..., None] * bbar_i[None] + pw_i[..., None] * bbar_r[None]
    p_blk = jnp.stack([jnp.transpose(e_r, (0, 1, 3, 2)), jnp.transpose(e_i, (0, 1, 3, 2))], axis=3)
    p_blk = p_blk.reshape(S5_ROW, 2 * S5_STATE)

    m_r = pr[:S5_L, :, None, :] * cr[None] - pi[:S5_L, :, None, :] * ci[None]
    m_i = pr[:S5_L, :, None, :] * ci[None] + pi[:S5_L, :, None, :] * cr[None]
    kern = (jnp.sum(m_r[:, :, :, :, None] * bbar_r[None, :, None, :, :], axis=3)
            - jnp.sum(m_i[:, :, :, :, None] * bbar_i[None, :, None, :, :], axis=3))
    t_blk = jnp.transpose(kern, (0, 1, 3, 2)).reshape(S5_L, D_S5, S5_GROUP)

    q_r = pr[1:, :, None, :] * cr[None] - pi[1:, :, None, :] * ci[None]
    q_i = -(pr[1:, :, None, :] * ci[None] + pi[1:, :, None, :] * cr[None])
    q_blk = jnp.stack([jnp.transpose(q_r, (0, 1, 3, 2)), jnp.transpose(q_i, (0, 1, 3, 2))], axis=1)
    q_blk = q_blk.reshape(S5_L, S5_NSTATE, S5_GROUP)

    half_blocks = S5_NSTATE // 2 // LANES
    a_r = pr[S5_L].reshape(half_blocks, 1, LANES)
    a_i = pi[S5_L].reshape(half_blocks, 1, LANES)
    d_row = jnp.tile(d_skip.astype(F32).reshape(1, D_S5), (1, S5_L))
    pt, q_mat = _s5_expand(p_blk.astype(BF16), t_blk.astype(BF16), q_blk.astype(BF16))
    return pt, q_mat, a_r, a_i, d_row


def _s5_expand_kernel(p_ref, t_ref, q_ref, pt_ref, qm_ref):
    s = pl.program_id(0)

    def iota(shape, dim):
        return lax.broadcasted_iota(jnp.int32, shape, dim)

    def onehot(rows, cols, row_key, col_key):
        r, c = iota((rows, cols), 0), iota((rows, cols), 1)
        return jnp.where(row_key(r) == col_key(c), 1.0, 0.0)

    half = S5_NSTATE // 2
    rep = onehot(2 * S5_STATE, S5_NSTATE, lambda r: r, lambda c: (c // half) * S5_STATE + c % S5_STATE)
    same = onehot(D_S5, S5_NSTATE, lambda r: r // S5_GROUP, lambda c: (c % half) // S5_STATE)
    p_full = jnp.dot(p_ref[...], rep.astype(BF16), preferred_element_type=F32) * same
    pt_ref[:, :S5_NSTATE] = p_full.astype(BF16)

    rep_c = onehot(S5_GROUP, D_S5, lambda r: r, lambda c: c % S5_GROUP).astype(BF16)
    same_g = onehot(D_S5, D_S5, lambda r: r // S5_GROUP, lambda c: c // S5_GROUP)
    for t in range(S5_L):
        d = jnp.clip(t - s, 0, S5_L - 1)
        blk = jnp.dot(t_ref[d], rep_c, preferred_element_type=F32) * same_g
        blk = jnp.where(t >= s, blk, 0.0)
        pt_ref[:, S5_NSTATE + t * D_S5:S5_NSTATE + (t + 1) * D_S5] = blk.astype(BF16)

    same_q = onehot(S5_NSTATE, D_S5, lambda r: (r % half) // S5_STATE, lambda c: c // S5_GROUP)
    q_full = jnp.dot(q_ref[0], rep_c, preferred_element_type=F32) * same_q
    qm_ref[...] = q_full.astype(BF16)


def _s5_expand(p_blk, t_blk, q_blk):
    return pl.pallas_call(
        _s5_expand_kernel,
        out_shape=(jax.ShapeDtypeStruct((S5_ROW, S5_NSTATE + S5_ROW), BF16),
                   jax.ShapeDtypeStruct((S5_NSTATE, S5_ROW), BF16)),
        grid=(S5_L,),
        in_specs=[pl.BlockSpec((D_S5, 2 * S5_STATE), lambda s: (s, 0)),
                  _resident(t_blk.shape),
                  pl.BlockSpec((1, S5_NSTATE, S5_GROUP), lambda s: (s, 0, 0))],
        out_specs=(pl.BlockSpec((D_S5, S5_NSTATE + S5_ROW), lambda s: (s, 0)),
                   pl.BlockSpec((S5_NSTATE, D_S5), lambda s: (0, s))),
        compiler_params=pltpu.CompilerParams(
            dimension_semantics=("arbitrary",), vmem_limit_bytes=VMEM_LIMIT),
        name="s5_tables",
    )(p_blk, t_blk, q_blk)


def _s5_kernel(ua_ref, ub_ref, pt_ref, q_ref, ar_ref, ai_ref, d_ref, wg_ref, o_ref,
               sr_ref, si_ref, u8_ref, es_ref, ot_ref):
    nb = ua_ref.shape[0]
    ncl = S5_TOK // S5_L
    rows = nb * ncl
    nlb = S5_NSTATE // LANES
    hlb = nlb // 2

    @pl.when(pl.program_id(0) == 0)
    def _():
        sr_ref[...] = jnp.zeros_like(sr_ref)
        si_ref[...] = jnp.zeros_like(si_ref)

    sub = 8
    ncg = ncl // sub
    grp = nb * sub
    for t in range(S5_L):
        for hh, u_ref in enumerate((ua_ref, ub_ref)):
            lo = t * D_S5 + hh * LANES
            for cg in range(ncg):
                blk = u_ref[:, pl.ds(cg * sub * S5_L + t, sub, stride=S5_L), :]
                u8_ref[cg * grp:(cg + 1) * grp, lo:lo + LANES] = blk.reshape(grp, LANES)

    u8 = u8_ref[...]
    u8b = u8.astype(BF16)
    e = jnp.dot(u8b, pt_ref[:, :S5_NSTATE], preferred_element_type=F32)
    for j in range(nlb):
        es_ref[j] = e[:, j * LANES:(j + 1) * LANES]
    y = jnp.dot(u8b, pt_ref[:, S5_NSTATE:], preferred_element_type=F32) + d_ref[...] * u8

    a_r = ar_ref[...]
    a_i = ai_ref[...]
    sr = sr_ref[...]
    si = si_ref[...]
    for cl in range(ncl):
        step = pl.ds((cl // sub) * grp + cl % sub, nb, stride=sub)
        e_r = es_ref[:hlb, step, :]
        e_i = es_ref[hlb:, step, :]
        es_ref[:hlb, step, :] = sr
        es_ref[hlb:, step, :] = si
        sr, si = a_r * sr - a_i * si + e_r, a_r * si + a_i * sr + e_i
    sr_ref[...] = sr
    si_ref[...] = si

    s_all = jnp.concatenate([es_ref[j] for j in range(nlb)], axis=1).astype(BF16)
    y = y + jnp.dot(s_all, q_ref[...], preferred_element_type=F32)
    y = jax.nn.gelu(y).astype(BF16)
    for t in range(S5_L):
        ag = jnp.dot(y[:, t * D_S5:(t + 1) * D_S5], wg_ref[...], preferred_element_type=F32)
        out = ag[:, :D_S5] * _sigmoid(ag[:, D_S5:])
        for hh in range(D_S5 // LANES):
            for cg in range(ncg):
                ot_ref[hh, :, pl.ds(cg * sub * S5_L + t, sub, stride=S5_L), :] = (
                    out[cg * grp:(cg + 1) * grp, hh * LANES:(hh + 1) * LANES].reshape(nb, sub, LANES))
    o_ref[...] = jnp.concatenate([ot_ref[0], ot_ref[1]], axis=-1).astype(BF16)


def _s5(u, pt, q_mat, a_r, a_i, d_row, w_glu):
    nb, seq, _ = u.shape
    rows = nb * (S5_TOK // S5_L)
    nlb = S5_NSTATE // LANES
    return pl.pallas_call(
        _s5_kernel,
        out_shape=jax.ShapeDtypeStruct((nb, seq, D_S5), BF16),
        grid=(seq // S5_TOK,),
        in_specs=[pl.BlockSpec((nb, S5_TOK, LANES), lambda i: (0, i, 0)),
                  pl.BlockSpec((nb, S5_TOK, LANES), lambda i: (0, i, 1)),
                  _resident(pt.shape), _resident(q_mat.shape), _resident(a_r.shape),
                  _resident(a_i.shape), _resident(d_row.shape), _resident(w_glu.shape)],
        out_specs=pl.BlockSpec((nb, S5_TOK, D_S5), lambda i: (0, i, 0)),
        scratch_shapes=[pltpu.VMEM((nlb // 2, nb, LANES), F32),
                        pltpu.VMEM((nlb // 2, nb, LANES), F32),
                        pltpu.VMEM((rows, S5_ROW), F32),
                        pltpu.VMEM((nlb, rows, LANES), F32),
                        pltpu.VMEM((D_S5 // LANES, nb, S5_TOK, LANES), F32)],
        compiler_params=pltpu.CompilerParams(
            dimension_semantics=("arbitrary",), vmem_limit_bytes=VMEM_LIMIT),
        name="s5",
    )(u, u, pt, q_mat, a_r, a_i, d_row, w_glu)


def _attn_bias(rel_bias):
    rb = rel_bias.astype(F32) * LOG2E
    far_past, far_future = rb[:, 2 * MAX_REL:], rb[:, :1]
    period = ATT_QB + ATT_W
    n_const = N_LEFT * CHUNK - MAX_REL
    vec = jnp.concatenate([
        jnp.broadcast_to(far_past, (HEADS, n_const)),
        rb[:, ::-1],
        jnp.broadcast_to(far_future, (HEADS, ATT_W - n_const - 2 * MAX_REL - 1)),
        jnp.broadcast_to(far_past, (HEADS, ATT_QB)),
    ], axis=1)
    toep = jnp.tile(vec, (1, ATT_QB))[:, :ATT_QB * (period - 1)]
    toep = toep.reshape(HEADS, ATT_QB, period - 1)[:, :, :ATT_W]
    r = jnp.arange(ATT_QB)[:, None]
    c = jnp.arange(ATT_W)[None, :] - N_LEFT * CHUNK
    dchunk = r // CHUNK - jnp.floor_divide(c, CHUNK)
    ok = (dchunk >= 0) & (dchunk <= N_LEFT)
    bias = jnp.where(ok[None], toep, -1e30)
    return bias.reshape(HEADS // 2, 2, ATT_QB, ATT_W)


def _attn_kernel(q_ref, k_ref, v_ref, bm_ref, o_ref):
    seq = q_ref.shape[1]
    lane = lax.broadcasted_iota(jnp.int32, (1, LANES), 1)
    first = lane < HEAD_DIM

    def block(q0, k0, width):
        for pr in range(D_ATT // LANES):
            cols = slice(pr * LANES, (pr + 1) * LANES)
            q2 = q_ref[0, pl.ds(q0, ATT_QB), cols]
            outs = []
            for hh in range(2):
                sel = first if hh == 0 else jnp.logical_not(first)
                qh = jnp.where(sel, q2, jnp.zeros_like(q2))
                for c0 in range(0, width, ATT_KT):
                    kw = k_ref[0, pl.ds(k0 + c0, ATT_KT), cols]
                    vw = v_ref[0, pl.ds(k0 + c0, ATT_KT), cols]
                    s = lax.dot_general(qh, kw, (((1,), (1,)), ((), ())),
                                        preferred_element_type=F32)
                    b0 = ATT_W - width + c0
                    s = s + bm_ref[pr, hh, :, b0:b0 + ATT_KT]
                    vh = jnp.where(sel, vw, jnp.ones_like(vw))
                    if c0 == 0:
                        mx = jnp.max(s, axis=-1, keepdims=True)
                        acc = jnp.dot(jnp.exp2(s - mx).astype(BF16), vh, preferred_element_type=F32)
                    else:
                        mx_new = jnp.maximum(mx, jnp.max(s, axis=-1, keepdims=True))
                        p = jnp.exp2(s - mx_new).astype(BF16)
                        acc = acc * jnp.exp2(mx - mx_new) + jnp.dot(p, vh, preferred_element_type=F32)
                        mx = mx_new
                outs.append(acc)
            num = jnp.where(first, outs[0], outs[1])
            den = pltpu.roll(jnp.where(first, outs[1], outs[0]), HEAD_DIM, axis=1)
            o_ref[0, pl.ds(q0, ATT_QB), cols] = (num / den).astype(BF16)

    n_short = N_LEFT * CHUNK // ATT_QB
    for qb in range(n_short):
        block(qb * ATT_QB, 0, (qb + 1) * ATT_QB)

    def body(qb, carry):
        q0 = pl.multiple_of(qb * ATT_QB, ATT_QB)
        k0 = pl.multiple_of(q0 - N_LEFT * CHUNK, ATT_QB)
        block(q0, k0, ATT_W)
        return carry
    lax.fori_loop(n_short, seq // ATT_QB, body, 0)


def _attn(q, k, v, bm):
    nb, seq, _ = q.shape
    seqblk = pl.BlockSpec((1, seq, D_ATT), lambda b: (b, 0, 0))
    return pl.pallas_call(
        _attn_kernel,
        out_shape=jax.ShapeDtypeStruct((nb, seq, D_ATT), BF16),
        grid=(nb,),
        in_specs=[seqblk, seqblk, seqblk, _resident(bm.shape)],
        out_specs=seqblk,
        compiler_params=pltpu.CompilerParams(
            dimension_semantics=("arbitrary",), vmem_limit_bytes=VMEM_LIMIT),
        name="attn",
    )(q, k, v, bm)


def _conv_kernel(h_ref, w_ref, b_ref, g_ref, beta_ref, o_ref, buf_ref, cv_ref):
    seq = h_ref.shape[1]
    nh = D_CONV // LANES
    ntile = seq // CONV_TR
    for hh in range(nh):
        buf_ref[hh, :CONV_PAD, :] = jnp.zeros((CONV_PAD, LANES), F32)
        buf_ref[hh, CONV_PAD:, :] = h_ref[0, :, hh * LANES:(hh + 1) * LANES]
    shift = CONV_PAD - (CONV_WIDTH - 1)

    def taps(i, carry):
        hh = i // ntile
        r0 = pl.multiple_of((i - hh * ntile) * CONV_TR, CONV_TR)
        acc = jnp.zeros((CONV_TR, LANES), F32) + b_ref[hh]
        for j in range(CONV_WIDTH):
            acc = acc + w_ref[hh, j:j + 1, :] * buf_ref[hh, pl.ds(r0 + shift + j, CONV_TR), :]
        cv_ref[hh, pl.ds(r0, CONV_TR), :] = acc
        return carry
    lax.fori_loop(0, nh * ntile, taps, 0)

    def norm(i, carry):
        r0 = pl.multiple_of(i * CONV_NORM_TR, CONV_NORM_TR)
        acc = jnp.concatenate([cv_ref[hh, pl.ds(r0, CONV_NORM_TR), :] for hh in range(nh)], axis=-1)
        mu = jnp.mean(acc, axis=-1, keepdims=True)
        cen = acc - mu
        var = jnp.mean(cen * cen, axis=-1, keepdims=True)
        y = (cen * lax.rsqrt(var + EPS)) * g_ref[...] + beta_ref[...]
        o_ref[0, pl.ds(r0, CONV_NORM_TR), :] = (y * _sigmoid(y)).astype(BF16)
        return carry
    lax.fori_loop(0, seq // CONV_NORM_TR, norm, 0)


def _conv(hc, w_dw, b_dw, ln_g, ln_b):
    nb, seq, _ = hc.shape
    nh = D_CONV // LANES
    blk = pl.BlockSpec((1, seq, D_CONV), lambda b: (b, 0, 0))
    return pl.pallas_call(
        _conv_kernel,
        out_shape=jax.ShapeDtypeStruct((nb, seq, D_CONV), BF16),
        grid=(nb,),
        in_specs=[blk, _resident((nh, CONV_WIDTH, LANES)), _resident((nh, 1, LANES)),
                  _resident((1, D_CONV)), _resident((1, D_CONV))],
        out_specs=blk,
        scratch_shapes=[pltpu.VMEM((nh, seq + CONV_PAD, LANES), F32),
                        pltpu.VMEM((nh, seq, LANES), F32)],
        compiler_params=pltpu.CompilerParams(
            dimension_semantics=("arbitrary",), vmem_limit_bytes=VMEM_LIMIT),
        name="conv",
    )(hc, w_dw, b_dw, ln_g, ln_b)


def _merge_kernel(x_ref, s5_ref, at_ref, cv_ref, g_ref, wg_ref, bg_ref,
                  ws_ref, wa_ref, wc_ref, wo_ref, o_ref):
    x = x_ref[...]
    h = _rmsnorm_bf16(x, g_ref[...])
    branches = ((s5_ref, ws_ref), (at_ref, wa_ref), (cv_ref, wc_ref))
    merged = jnp.zeros_like(x)
    for i, (br_ref, w_ref) in enumerate(branches):
        cols = slice(i * D_MODEL, (i + 1) * D_MODEL)
        logits = jnp.dot(h, wg_ref[:, cols], preferred_element_type=F32) + bg_ref[:, cols]
        y = jnp.dot(br_ref[...], w_ref[...], preferred_element_type=F32)
        merged = merged + _sigmoid(logits) * y
    o_ref[...] = x + jnp.dot(merged.astype(BF16), wo_ref[...], preferred_element_type=F32)


def _merge(x, s5o, ato, cvo, g, w_gate, b_gate, w_s5, w_at, w_cv, w_out):
    n = x.shape[0]
    def tok(width):
        return pl.BlockSpec((TM, width), lambda i: (i, 0))
    return pl.pallas_call(
        _merge_kernel,
        out_shape=jax.ShapeDtypeStruct((n, D_MODEL), F32),
        grid=(n // TM,),
        in_specs=[tok(D_MODEL), tok(D_S5), tok(D_ATT), tok(D_CONV),
                  _resident((1, D_MODEL)), _resident((D_MODEL, 3 * D_MODEL)),
                  _resident((1, 3 * D_MODEL)), _resident((D_S5, D_MODEL)),
                  _resident((D_ATT, D_MODEL)), _resident((D_CONV, D_MODEL)),
                  _resident((D_MODEL, D_MODEL))],
        out_specs=tok(D_MODEL),
        compiler_params=pltpu.CompilerParams(
            dimension_semantics=("arbitrary",), vmem_limit_bytes=VMEM_LIMIT),
        name="merge",
    )(x, s5o, ato, cvo, g, w_gate, b_gate, w_s5, w_at, w_cv, w_out)


def kernel(x, ffn1_norm, ffn1_w_up, ffn1_w_down, mix_norm, w_in, b_gate, s5_lambda_re, s5_lambda_im, s5_log_dt, s5_b_re, s5_b_im, s5_c_re, s5_c_im, s5_d, s5_w_glu, w_br_s5, attn_q_gain, attn_k_gain, attn_rel_bias, w_br_attn, conv_w_dw, conv_b_dw, conv_ln_g, conv_ln_b, w_br_conv, w_out, ffn2_norm, ffn2_w_up, ffn2_w_down):
    nb, seq, d = x.shape
    n = nb * seq
    depth = ffn1_norm.shape[0]
    xt = x.reshape(n, d)
    row = lambda v: v.reshape(1, -1).astype(F32)
    for l in range(depth):
        xt = _ffn(xt, row(ffn1_norm[l]), ffn1_w_up[l, :, :D_FF].astype(BF16),
                  ffn1_w_up[l, :, D_FF:].astype(BF16), ffn1_w_down[l].astype(BF16))

        qg2 = jnp.tile(row(attn_q_gain[l]), (1, LANES // HEAD_DIM))
        kg2 = jnp.tile(row(attn_k_gain[l]), (1, LANES // HEAD_DIM))
        u, q, k, v, hc = _inproj(xt, row(mix_norm[l]), w_in[l, :, :N_SMALL].astype(BF16), qg2, kg2)

        pt, q_mat, a_r, a_i, d_row = _s5_tables(
            s5_lambda_re[l], s5_lambda_im[l], s5_log_dt[l], s5_b_re[l], s5_b_im[l],
            s5_c_re[l], s5_c_im[l], s5_d[l])
        s5o = _s5(u.reshape(nb, seq, D_S5), pt, q_mat, a_r, a_i, d_row, s5_w_glu[l].astype(BF16))

        ato = _attn(q.reshape(nb, seq, D_ATT), k.reshape(nb, seq, D_ATT),
                    v.reshape(nb, seq, D_ATT), _attn_bias(attn_rel_bias[l]))

        nh = D_CONV // LANES
        w_dw = conv_w_dw[l].astype(F32).reshape(CONV_WIDTH, nh, LANES).transpose(1, 0, 2)
        cvo = _conv(hc.reshape(nb, seq, D_CONV), w_dw, conv_b_dw[l].astype(F32).reshape(nh, 1, LANES),
                    row(conv_ln_g[l]), row(conv_ln_b[l]))

        xt = _merge(xt, s5o.reshape(n, D_S5), ato.reshape(n, D_ATT), cvo.reshape(n, D_CONV),
                    row(mix_norm[l]), w_in[l, :, N_SMALL:].astype(BF16), row(b_gate[l]),
                    w_br_s5[l].astype(BF16), w_br_attn[l].astype(BF16),
                    w_br_conv[l].astype(BF16), w_out[l].astype(BF16))

        xt = _ffn(xt, row(ffn2_norm[l]), ffn2_w_up[l, :, :D_FF].astype(BF16),
                  ffn2_w_up[l, :, D_FF:].astype(BF16), ffn2_w_down[l].astype(BF16))
    return xt.reshape(nb, seq, d)
```

```python
import functools
import math

import jax
import jax.numpy as jnp
from jax import lax
from jax.experimental import pallas as pl
from jax.experimental.pallas import tpu as pltpu

F32 = jnp.float32
BF16 = jnp.bfloat16

D_MODEL = 1024
D_S5 = 256
S5_GROUP = 16
S5_GROUPS = 16
S5_STATE = 64
D_ATT = 512
HEAD_DIM = 64
HEADS = 8
CHUNK = 64
N_LEFT = 8
MAX_REL = 128
D_CONV = 256
CONV_WIDTH = 31
D_FF = 2816
EPS = 1e-6
LOG2E = math.log2(math.e)
N_SMALL = D_S5 + 3 * D_ATT + 2 * D_CONV

LANES = 128
MXU_DIM = 256
VMEM_LIMIT = 56 * 1024 * 1024

TM = 512
TM_WIDE = 1024
S5_L = 8
S5_CB = LANES
S5_GB = S5_CB // S5_GROUP
S5_NCB = D_S5 // S5_CB
S5_BROW = S5_L * S5_CB
S5_BSTATE = 2 * S5_GB * S5_STATE
S5_TOK = 256
ATT_QB = 4 * CHUNK
ATT_W = ATT_QB + N_LEFT * CHUNK
ATT_KT = MXU_DIM
CONV_TR = 128
CONV_PAD = 32


def _resident(shape):
    nd = len(shape)
    return pl.BlockSpec(shape, lambda *_: (0,) * nd, pipeline_mode=pl.Buffered(1))


def _rmsnorm_bf16(x, g):
    ms = jnp.mean(x * x, axis=-1, keepdims=True)
    return ((x * lax.rsqrt(ms + EPS)) * g).astype(BF16)


def _sigmoid(x):
    return jax.nn.sigmoid(x)


def _ffn_kernel(x_ref, g_ref, wa_ref, wb_ref, wd_ref, o_ref, *, bounds):
    x = x_ref[...]
    h = _rmsnorm_bf16(x, g_ref[...])
    y = jnp.zeros_like(x)
    for f0, f1 in zip(bounds[:-1], bounds[1:]):
        a = jnp.dot(h, wa_ref[:, f0:f1], preferred_element_type=F32)
        b = jnp.dot(h, wb_ref[:, f0:f1], preferred_element_type=F32)
        act = ((a * _sigmoid(a)) * b).astype(BF16)
        y = y + jnp.dot(act, wd_ref[f0:f1, :], preferred_element_type=F32)
    o_ref[...] = x + 0.5 * y


def _ffn_bounds():
    tiles = D_FF // MXU_DIM
    assert tiles * MXU_DIM == D_FF
    return (0, (tiles + 1) // 2 * MXU_DIM, D_FF)


def _ffn(x, g, wa, wb, wd):
    n = x.shape[0]
    tok = pl.BlockSpec((TM_WIDE, D_MODEL), lambda i: (i, 0))
    return pl.pallas_call(
        functools.partial(_ffn_kernel, bounds=_ffn_bounds()),
        out_shape=jax.ShapeDtypeStruct((n, D_MODEL), F32),
        grid=(n // TM_WIDE,),
        in_specs=[tok, _resident((1, D_MODEL)), _resident((D_MODEL, D_FF)),
                  _resident((D_MODEL, D_FF)), _resident((D_FF, D_MODEL))],
        out_specs=tok,
        compiler_params=pltpu.CompilerParams(
            dimension_semantics=("arbitrary",), vmem_limit_bytes=VMEM_LIMIT),
        name="ffn",
    )(x, g, wa, wb, wd)


def _head_pair_norm(x2, gain2):
    lane = lax.broadcasted_iota(jnp.int32, (1, LANES), 1)
    first = lane < HEAD_DIM
    sq = x2 * x2
    s_a = jnp.sum(jnp.where(first, sq, 0.0), axis=-1, keepdims=True)
    s_b = jnp.sum(jnp.where(first, 0.0, sq), axis=-1, keepdims=True)
    ms = jnp.where(first, s_a, s_b) * (1.0 / HEAD_DIM)
    return (x2 * lax.rsqrt(ms + EPS)) * gain2


def _inproj_kernel(x_ref, g_ref, w_ref, qg_ref, kg_ref, u_ref, q_ref, k_ref, v_ref, hc_ref):
    h = _rmsnorm_bf16(x_ref[...], g_ref[...])
    proj = jnp.dot(h, w_ref[...], preferred_element_type=F32)
    u_ref[...] = proj[:, :D_S5]
    q0, k0, v0, z0 = D_S5, D_S5 + D_ATT, D_S5 + 2 * D_ATT, D_S5 + 3 * D_ATT
    scale = HEAD_DIM ** -0.5 * LOG2E
    for p in range(D_ATT // LANES):
        lo = p * LANES
        qn = _head_pair_norm(proj[:, q0 + lo:q0 + lo + LANES], qg_ref[...])
        kn = _head_pair_norm(proj[:, k0 + lo:k0 + lo + LANES], kg_ref[...])
        q_ref[:, lo:lo + LANES] = (qn * scale).astype(BF16)
        k_ref[:, lo:lo + LANES] = kn.astype(BF16)
    v_ref[...] = proj[:, v0:v0 + D_ATT].astype(BF16)
    a = proj[:, z0:z0 + D_CONV]
    gt = proj[:, z0 + D_CONV:z0 + 2 * D_CONV]
    hc_ref[...] = a * _sigmoid(gt)


def _inproj(x, g, w, qg2, kg2):
    n = x.shape[0]
    def tok(width):
        return pl.BlockSpec((TM_WIDE, width), lambda i: (i, 0))
    return pl.pallas_call(
        _inproj_kernel,
        out_shape=(jax.ShapeDtypeStruct((n, D_S5), F32),
                   jax.ShapeDtypeStruct((n, D_ATT), BF16),
                   jax.ShapeDtypeStruct((n, D_ATT), BF16),
                   jax.ShapeDtypeStruct((n, D_ATT), BF16),
                   jax.ShapeDtypeStruct((n, D_CONV), F32)),
        grid=(n // TM_WIDE,),
        in_specs=[tok(D_MODEL), _resident((1, D_MODEL)), _resident((D_MODEL, N_SMALL)),
                  _resident((1, LANES)), _resident((1, LANES))],
        out_specs=(tok(D_S5), tok(D_ATT), tok(D_ATT), tok(D_ATT), tok(D_CONV)),
        compiler_params=pltpu.CompilerParams(
            dimension_semantics=("arbitrary",), vmem_limit_bytes=VMEM_LIMIT),
        name="inproj",
    )(x, g, w, qg2, kg2)


def _s5_tables(lambda_re, lambda_im, log_dt, b_re, b_im, c_re, c_im, d_skip):
    lr = jnp.minimum(lambda_re.astype(F32), -1e-4)
    li = lambda_im.astype(F32)
    dt = jnp.exp(log_dt.astype(F32))[:, None]
    mag = jnp.exp(lr * dt)
    ar = mag * jnp.cos(li * dt)
    ai = mag * jnp.sin(li * dt)
    den = lr * lr + li * li
    coef_r = ((ar - 1.0) * lr + ai * li) / den
    coef_i = (ai * lr - (ar - 1.0) * li) / den
    br = b_re.astype(F32)
    bi = b_im.astype(F32)
    bbar_r = coef_r[..., None] * br - coef_i[..., None] * bi
    bbar_i = coef_r[..., None] * bi + coef_i[..., None] * br
    cr = c_re.astype(F32)
    ci = c_im.astype(F32)

    pr, pi = [jnp.ones_like(ar)], [jnp.zeros_like(ai)]
    for _ in range(S5_L):
        r, i = pr[-1], pi[-1]
        pr.append(r * ar - i * ai)
        pi.append(r * ai + i * ar)
    pr = jnp.stack(pr)
    pi = jnp.stack(pi)

    ncb, gb = S5_NCB, S5_GB
    pw_r = pr[S5_L - 1::-1]
    pw_i = pi[S5_L - 1::-1]
    e_r = pw_r[..., None] * bbar_r[None] - pw_i[..., None] * bbar_i[None]
    e_i = pw_r[..., None] * bbar_i[None] + pw_i[..., None] * bbar_r[None]
    p_blk = jnp.stack([jnp.transpose(e_r, (0, 1, 3, 2)), jnp.transpose(e_i, (0, 1, 3, 2))], axis=3)
    p_blk = p_blk.reshape(S5_L, ncb, gb * S5_GROUP, 2 * S5_STATE)
    p_blk = jnp.transpose(p_blk, (1, 0, 2, 3)).reshape(ncb, S5_BROW, 2 * S5_STATE)

    m_r = pr[:S5_L, :, None, :] * cr[None] - pi[:S5_L, :, None, :] * ci[None]
    m_i = pr[:S5_L, :, None, :] * ci[None] + pi[:S5_L, :, None, :] * cr[None]
    kern = (jnp.sum(m_r[:, :, :, :, None] * bbar_r[None, :, None, :, :], axis=3)
            - jnp.sum(m_i[:, :, :, :, None] * bbar_i[None, :, None, :, :], axis=3))
    t_blk = jnp.transpose(kern, (0, 1, 3, 2)).reshape(S5_L, ncb, S5_CB, S5_GROUP)
    t_blk = jnp.transpose(t_blk, (1, 0, 2, 3))

    q_r = pr[1:, :, None, :] * cr[None] - pi[1:, :, None, :] * ci[None]
    q_i = -(pr[1:, :, None, :] * ci[None] + pi[1:, :, None, :] * cr[None])
    q_blk = jnp.stack([jnp.transpose(q_r, (0, 1, 3, 2)), jnp.transpose(q_i, (0, 1, 3, 2))], axis=1)
    q_blk = q_blk.reshape(S5_L, 2, ncb, gb * S5_STATE, S5_GROUP)
    q_blk = jnp.transpose(q_blk, (2, 0, 1, 3, 4)).reshape(ncb, S5_L, S5_BSTATE, S5_GROUP)

    half_blocks = S5_BSTATE // 2 // LANES
    a_r = pr[S5_L].reshape(ncb, half_blocks, 1, LANES)
    a_i = pi[S5_L].reshape(ncb, half_blocks, 1, LANES)
    d_row = jnp.tile(d_skip.astype(F32).reshape(ncb, 1, S5_CB), (1, 1, S5_L))
    pt, q_mat = _s5_expand(p_blk.astype(BF16), t_blk.astype(BF16), q_blk.astype(BF16))
    return pt, q_mat, a_r, a_i, d_row


def _s5_expand_kernel(p_ref, t_ref, q_ref, pt_ref, qm_ref):
    s = pl.program_id(1)

    def iota(shape, dim):
        return lax.broadcasted_iota(jnp.int32, shape, dim)

    def onehot(rows, cols, row_key, col_key):
        r, c = iota((rows, cols), 0), iota((rows, cols), 1)
        return jnp.where(row_key(r) == col_key(c), 1.0, 0.0)

    half = S5_BSTATE // 2
    rep = onehot(2 * S5_STATE, S5_BSTATE, lambda r: r, lambda c: (c // half) * S5_STATE + c % S5_STATE)
    same = onehot(S5_CB, S5_BSTATE, lambda r: r // S5_GROUP, lambda c: (c % half) // S5_STATE)
    p_full = jnp.dot(p_ref[...], rep.astype(BF16), preferred_element_type=F32) * same
    pt_ref[:, :S5_BSTATE] = p_full.astype(BF16)

    rep_c = onehot(S5_GROUP, S5_CB, lambda r: r, lambda c: c % S5_GROUP).astype(BF16)
    same_g = onehot(S5_CB, S5_CB, lambda r: r // S5_GROUP, lambda c: c // S5_GROUP)
    for t in range(S5_L):
        d = jnp.clip(t - s, 0, S5_L - 1)
        blk = jnp.dot(t_ref[d], rep_c, preferred_element_type=F32) * same_g
        blk = jnp.where(t >= s, blk, 0.0)
        pt_ref[:, S5_BSTATE + t * S5_CB:S5_BSTATE + (t + 1) * S5_CB] = blk.astype(BF16)

    same_q = onehot(S5_BSTATE, S5_CB, lambda r: (r % half) // S5_STATE, lambda c: c // S5_GROUP)
    q_full = jnp.dot(q_ref[...], rep_c, preferred_element_type=F32) * same_q
    qm_ref[...] = q_full.astype(BF16)


def _s5_expand(p_blk, t_blk, q_blk):
    return pl.pallas_call(
        _s5_expand_kernel,
        out_shape=(jax.ShapeDtypeStruct((S5_NCB, S5_BROW, S5_BSTATE + S5_BROW), BF16),
                   jax.ShapeDtypeStruct((S5_NCB, S5_BSTATE, S5_BROW), BF16)),
        grid=(S5_NCB, S5_L),
        in_specs=[pl.BlockSpec((None, S5_CB, 2 * S5_STATE), lambda cb, s: (cb, s, 0)),
                  pl.BlockSpec((None, S5_L, S5_CB, S5_GROUP), lambda cb, s: (cb, 0, 0, 0)),
                  pl.BlockSpec((None, None, S5_BSTATE, S5_GROUP), lambda cb, s: (cb, s, 0, 0))],
        out_specs=(pl.BlockSpec((None, S5_CB, S5_BSTATE + S5_BROW), lambda cb, s: (cb, s, 0)),
                   pl.BlockSpec((None, S5_BSTATE, S5_CB), lambda cb, s: (cb, 0, s))),
        compiler_params=pltpu.CompilerParams(
            dimension_semantics=("arbitrary", "arbitrary"), vmem_limit_bytes=VMEM_LIMIT),
        name="s5_tables",
    )(p_blk, t_blk, q_blk)


def _s5_kernel(ua_ref, ub_ref, pt_ref, q_ref, ar_ref, ai_ref, d_ref, wg_ref, o_ref,
               sr_ref, si_ref, u8_ref, es_ref, ot_ref):
    nb = ua_ref.shape[0]
    ncl = S5_TOK // S5_L
    nlb = S5_BSTATE // LANES
    hlb = nlb // 2
    u_refs = (ua_ref, ub_ref)

    @pl.when(pl.program_id(0) == 0)
    def _():
        sr_ref[...] = jnp.zeros_like(sr_ref)
        si_ref[...] = jnp.zeros_like(si_ref)

    sub = 8
    ncg = ncl // sub
    grp = nb * sub
    for cb in range(S5_NCB):
        for t in range(S5_L):
            for cg in range(ncg):
                blk = u_refs[cb][:, pl.ds(cg * sub * S5_L + t, sub, stride=S5_L), :]
                u8_ref[cb, cg * grp:(cg + 1) * grp, t * S5_CB:(t + 1) * S5_CB] = blk.reshape(grp, LANES)

    ys = []
    for cb in range(S5_NCB):
        u8 = u8_ref[cb]
        u8b = u8.astype(BF16)
        e = jnp.dot(u8b, pt_ref[cb, :, :S5_BSTATE], preferred_element_type=F32)
        for j in range(nlb):
            es_ref[cb, j] = e[:, j * LANES:(j + 1) * LANES]
        ys.append(jnp.dot(u8b, pt_ref[cb, :, S5_BSTATE:], preferred_element_type=F32) + d_ref[cb] * u8)

    a_r = ar_ref[...]
    a_i = ai_ref[...]
    sr = sr_ref[...]
    si = si_ref[...]
    for cl in range(ncl):
        step = pl.ds((cl // sub) * grp + cl % sub, nb, stride=sub)
        e_r = es_ref[:, :hlb, step, :]
        e_i = es_ref[:, hlb:, step, :]
        es_ref[:, :hlb, step, :] = sr
        es_ref[:, hlb:, step, :] = si
        sr, si = a_r * sr - a_i * si + e_r, a_r * si + a_i * sr + e_i
    sr_ref[...] = sr
    si_ref[...] = si

    for cb in range(S5_NCB):
        s_all = jnp.concatenate([es_ref[cb, j] for j in range(nlb)], axis=1).astype(BF16)
        y = ys[cb] + jnp.dot(s_all, q_ref[cb], preferred_element_type=F32)
        ys[cb] = jax.nn.gelu(y).astype(BF16)
    for t in range(S5_L):
        yt = jnp.concatenate([y[:, t * S5_CB:(t + 1) * S5_CB] for y in ys], axis=1)
        ag = jnp.dot(yt, wg_ref[...], preferred_element_type=F32)
        out = ag[:, :D_S5] * _sigmoid(ag[:, D_S5:])
        for hh in range(D_S5 // LANES):
            for cg in range(ncg):
                ot_ref[hh, :, pl.ds(cg * sub * S5_L + t, sub, stride=S5_L), :] = (
                    out[cg * grp:(cg + 1) * grp, hh * LANES:(hh + 1) * LANES].reshape(nb, sub, LANES))
    o_ref[...] = jnp.concatenate([ot_ref[0], ot_ref[1]], axis=-1).astype(BF16)


def _s5(u, pt, q_mat, a_r, a_i, d_row, w_glu):
    nb, seq, _ = u.shape
    rows = nb * (S5_TOK // S5_L)
    nlb = S5_BSTATE // LANES
    assert S5_CB == LANES and S5_NCB == 2
    return pl.pallas_call(
        _s5_kernel,
        out_shape=jax.ShapeDtypeStruct((nb, seq, D_S5), BF16),
        grid=(seq // S5_TOK,),
        in_specs=[pl.BlockSpec((nb, S5_TOK, LANES), lambda i: (0, i, 0)),
                  pl.BlockSpec((nb, S5_TOK, LANES), lambda i: (0, i, 1)),
                  _resident(pt.shape), _resident(q_mat.shape), _resident(a_r.shape),
                  _resident(a_i.shape), _resident(d_row.shape), _resident(w_glu.shape)],
        out_specs=pl.BlockSpec((nb, S5_TOK, D_S5), lambda i: (0, i, 0)),
        scratch_shapes=[pltpu.VMEM((S5_NCB, nlb // 2, nb, LANES), F32),
                        pltpu.VMEM((S5_NCB, nlb // 2, nb, LANES), F32),
                        pltpu.VMEM((S5_NCB, rows, S5_BROW), F32),
                        pltpu.VMEM((S5_NCB, nlb, rows, LANES), F32),
                        pltpu.VMEM((D_S5 // LANES, nb, S5_TOK, LANES), F32)],
        compiler_params=pltpu.CompilerParams(
            dimension_semantics=("arbitrary",), vmem_limit_bytes=VMEM_LIMIT),
        name="s5",
    )(u, u, pt, q_mat, a_r, a_i, d_row, w_glu)


def _attn_bias(rel_bias):
    rb = rel_bias.astype(F32) * LOG2E
    far_past, far_future = rb[:, 2 * MAX_REL:], rb[:, :1]
    period = ATT_QB + ATT_W
    n_const = N_LEFT * CHUNK - MAX_REL
    vec = jnp.concatenate([
        jnp.broadcast_to(far_past, (HEADS, n_const)),
        rb[:, ::-1],
        jnp.broadcast_to(far_future, (HEADS, ATT_W - n_const - 2 * MAX_REL - 1)),
        jnp.broadcast_to(far_past, (HEADS, ATT_QB)),
    ], axis=1)
    toep = jnp.tile(vec, (1, ATT_QB))[:, :ATT_QB * (period - 1)]
    toep = toep.reshape(HEADS, ATT_QB, period - 1)[:, :, :ATT_W]
    r = jnp.arange(ATT_QB)[:, None]
    c = jnp.arange(ATT_W)[None, :] - N_LEFT * CHUNK
    dchunk = r // CHUNK - jnp.floor_divide(c, CHUNK)
    ok = (dchunk >= 0) & (dchunk <= N_LEFT)
    bias = jnp.where(ok[None], toep, -1e30)
    return bias.reshape(HEADS // 2, 2, ATT_QB, ATT_W)


def _attn_kernel(q_ref, k_ref, v_ref, bm_ref, o_ref):
    seq = q_ref.shape[1]
    lane = lax.broadcasted_iota(jnp.int32, (1, LANES), 1)
    first = lane < HEAD_DIM

    def block(q0, k0, width):
        for pr in range(D_ATT // LANES):
            cols = slice(pr * LANES, (pr + 1) * LANES)
            q2 = q_ref[0, pl.ds(q0, ATT_QB), cols]
            outs = []
            for hh in range(2):
                sel = first if hh == 0 else jnp.logical_not(first)
                qh = jnp.where(sel, q2, jnp.zeros_like(q2))
                for c0 in range(0, width, ATT_KT):
                    kw = k_ref[0, pl.ds(k0 + c0, ATT_KT), cols]
                    vw = v_ref[0, pl.ds(k0 + c0, ATT_KT), cols]
                    s = lax.dot_general(qh, kw, (((1,), (1,)), ((), ())),
                                        preferred_element_type=F32)
                    b0 = ATT_W - width + c0
                    s = s + bm_ref[pr, hh, :, b0:b0 + ATT_KT]
                    vh = jnp.where(sel, vw, jnp.ones_like(vw))
                    if c0 == 0:
                        mx = jnp.max(s, axis=-1, keepdims=True)
                        acc = jnp.dot(jnp.exp2(s - mx).astype(BF16), vh, preferred_element_type=F32)
                    else:
                        mx_new = jnp.maximum(mx, jnp.max(s, axis=-1, keepdims=True))
                        p = jnp.exp2(s - mx_new).astype(BF16)
                        acc = acc * jnp.exp2(mx - mx_new) + jnp.dot(p, vh, preferred_element_type=F32)
                        mx = mx_new
                outs.append(acc)
            num = jnp.where(first, outs[0], outs[1])
            den = pltpu.roll(jnp.where(first, outs[1], outs[0]), HEAD_DIM, axis=1)
            o_ref[0, pl.ds(q0, ATT_QB), cols] = (num / den).astype(BF16)

    n_short = N_LEFT * CHUNK // ATT_QB
    for qb in range(n_short):
        block(qb * ATT_QB, 0, (qb + 1) * ATT_QB)

    def body(qb, carry):
        q0 = pl.multiple_of(qb * ATT_QB, ATT_QB)
        k0 = pl.multiple_of(q0 - N_LEFT * CHUNK, ATT_QB)
        block(q0, k0, ATT_W)
        return carry
    lax.fori_loop(n_short, seq // ATT_QB, body, 0)


def _attn(q, k, v, bm):
    nb, seq, _ = q.shape
    seqblk = pl.BlockSpec((1, seq, D_ATT), lambda b: (b, 0, 0))
    return pl.pallas_call(
        _attn_kernel,
        out_shape=jax.ShapeDtypeStruct((nb, seq, D_ATT), BF16),
        grid=(nb,),
        in_specs=[seqblk, seqblk, seqblk, _resident(bm.shape)],
        out_specs=seqblk,
        compiler_params=pltpu.CompilerParams(
            dimension_semantics=("arbitrary",), vmem_limit_bytes=VMEM_LIMIT),
        name="attn",
    )(q, k, v, bm)


def _conv_module(hc_ref, cw_ref, cb_ref, lg_ref, lb_ref, cbuf_ref, first_tile):
    nh = D_CONV // LANES
    tm = hc_ref.shape[0]

    @pl.when(first_tile)
    def _():
        cbuf_ref[:, :CONV_PAD, :] = jnp.zeros((nh, CONV_PAD, LANES), F32)
    for hh in range(nh):
        cbuf_ref[hh, CONV_PAD:, :] = hc_ref[:, hh * LANES:(hh + 1) * LANES]
    shift = CONV_PAD - (CONV_WIDTH - 1)
    halves = []
    for hh in range(nh):
        tiles = []
        for r0 in range(0, tm, CONV_TR):
            acc = jnp.zeros((CONV_TR, LANES), F32) + cb_ref[hh]
            for j in range(CONV_WIDTH):
                lo = r0 + shift + j
                acc = acc + cw_ref[hh, j:j + 1, :] * cbuf_ref[hh, lo:lo + CONV_TR, :]
            tiles.append(acc)
        halves.append(jnp.concatenate(tiles, axis=0))
    for hh in range(nh):
        cbuf_ref[hh, :CONV_PAD, :] = cbuf_ref[hh, tm:tm + CONV_PAD, :]
    acc = jnp.concatenate(halves, axis=-1)
    mu = jnp.mean(acc, axis=-1, keepdims=True)
    cen = acc - mu
    var = jnp.mean(cen * cen, axis=-1, keepdims=True)
    y = (cen * lax.rsqrt(var + EPS)) * lg_ref[...] + lb_ref[...]
    return (y * _sigmoid(y)).astype(BF16)


def _merge_kernel(x_ref, s5_ref, at_ref, hc_ref, g_ref, wg_ref, bg_ref, ws_ref, wa_ref, wc_ref, wo_ref,
                  cw_ref, cb_ref, lg_ref, lb_ref, o_ref, cbuf_ref, *, tiles_per_seq):
    cv = _conv_module(hc_ref, cw_ref, cb_ref, lg_ref, lb_ref, cbuf_ref,
                      pl.program_id(0) % tiles_per_seq == 0)
    x = x_ref[...]
    h = _rmsnorm_bf16(x, g_ref[...])
    branches = ((s5_ref[...], ws_ref), (at_ref[...], wa_ref), (cv, wc_ref))
    merged = jnp.zeros_like(x)
    for i, (br, w_ref) in enumerate(branches):
        cols = slice(i * D_MODEL, (i + 1) * D_MODEL)
        logits = jnp.dot(h, wg_ref[:, cols], preferred_element_type=F32) + bg_ref[:, cols]
        y = jnp.dot(br, w_ref[...], preferred_element_type=F32)
        merged = merged + _sigmoid(logits) * y
    o_ref[...] = x + jnp.dot(merged.astype(BF16), wo_ref[...], preferred_element_type=F32)


def _merge(x, s5o, ato, hc, g, w_gate, b_gate, w_s5, w_at, w_cv, w_out, w_dw, b_dw, ln_g, ln_b, seq):
    n = x.shape[0]
    nh = D_CONV // LANES
    assert seq % TM == 0
    def tok(width):
        return pl.BlockSpec((TM, width), lambda i: (i, 0))
    return pl.pallas_call(
        functools.partial(_merge_kernel, tiles_per_seq=seq // TM),
        out_shape=jax.ShapeDtypeStruct((n, D_MODEL), F32),
        grid=(n // TM,),
        in_specs=[tok(D_MODEL), tok(D_S5), tok(D_ATT), tok(D_CONV),
                  _resident((1, D_MODEL)), _resident((D_MODEL, 3 * D_MODEL)),
                  _resident((1, 3 * D_MODEL)), _resident((D_S5, D_MODEL)),
                  _resident((D_ATT, D_MODEL)), _resident((D_CONV, D_MODEL)),
                  _resident((D_MODEL, D_MODEL)), _resident((nh, CONV_WIDTH, LANES)),
                  _resident((nh, 1, LANES)), _resident((1, D_CONV)), _resident((1, D_CONV))],
        out_specs=tok(D_MODEL),
        scratch_shapes=[pltpu.VMEM((nh, CONV_PAD + TM, LANES), F32)],
        compiler_params=pltpu.CompilerParams(
            dimension_semantics=("arbitrary",), vmem_limit_bytes=VMEM_LIMIT),
        name="merge",
    )(x, s5o, ato, hc, g, w_gate, b_gate, w_s5, w_at, w_cv, w_out, w_dw, b_dw, ln_g, ln_b)


def kernel(x, ffn1_norm, ffn1_w_up, ffn1_w_down, mix_norm, w_in, b_gate, s5_lambda_re, s5_lambda_im, s5_log_dt, s5_b_re, s5_b_im, s5_c_re, s5_c_im, s5_d, s5_w_glu, w_br_s5, attn_q_gain, attn_k_gain, attn_rel_bias, w_br_attn, conv_w_dw, conv_b_dw, conv_ln_g, conv_ln_b, w_br_conv, w_out, ffn2_norm, ffn2_w_up, ffn2_w_down):
    nb, seq, d = x.shape
    n = nb * seq
    depth = ffn1_norm.shape[0]
    xt = x.reshape(n, d)
    row = lambda v: v.reshape(1, -1).astype(F32)
    for l in range(depth):
        xt = _ffn(xt, row(ffn1_norm[l]), ffn1_w_up[l, :, :D_FF].astype(BF16),
                  ffn1_w_up[l, :, D_FF:].astype(BF16), ffn1_w_down[l].astype(BF16))

        qg2 = jnp.tile(row(attn_q_gain[l]), (1, LANES // HEAD_DIM))
        kg2 = jnp.tile(row(attn_k_gain[l]), (1, LANES // HEAD_DIM))
        u, q, k, v, hc = _inproj(xt, row(mix_norm[l]), w_in[l, :, :N_SMALL].astype(BF16), qg2, kg2)

        pt, q_mat, a_r, a_i, d_row = _s5_tables(
            s5_lambda_re[l], s5_lambda_im[l], s5_log_dt[l], s5_b_re[l], s5_b_im[l],
            s5_c_re[l], s5_c_im[l], s5_d[l])
        s5o = _s5(u.reshape(nb, seq, D_S5), pt, q_mat, a_r, a_i, d_row, s5_w_glu[l].astype(BF16))

        ato = _attn(q.reshape(nb, seq, D_ATT), k.reshape(nb, seq, D_ATT),
                    v.reshape(nb, seq, D_ATT), _attn_bias(attn_rel_bias[l]))

        nh = D_CONV // LANES
        w_dw = conv_w_dw[l].astype(F32).reshape(CONV_WIDTH, nh, LANES).transpose(1, 0, 2)
        xt = _merge(xt, s5o.reshape(n, D_S5), ato.reshape(n, D_ATT), hc,
                    row(mix_norm[l]), w_in[l, :, N_SMALL:].astype(BF16), row(b_gate[l]),
                    w_br_s5[l].astype(BF16), w_br_attn[l].astype(BF16),
                    w_br_conv[l].astype(BF16), w_out[l].astype(BF16),
                    w_dw, conv_b_dw[l].astype(F32).reshape(nh, 1, LANES),
                    row(conv_ln_g[l]), row(conv_ln_b[l]), seq)

        xt = _ffn(xt, row(ffn2_norm[l]), ffn2_w_up[l, :, :D_FF].astype(BF16),
                  ffn2_w_up[l, :, D_FF:].astype(BF16), ffn2_w_down[l].astype(BF16))
    return xt.reshape(nb, seq, d)
```

```python
import functools
import math

import jax
import jax.numpy as jnp
from jax import lax
from jax.experimental import pallas as pl
from jax.experimental.pallas import tpu as pltpu

F32 = jnp.float32
BF16 = jnp.bfloat16

D_MODEL = 1024
D_S5 = 256
S5_GROUP = 16
S5_GROUPS = 16
S5_STATE = 64
D_ATT = 512
HEAD_DIM = 64
HEADS = 8
CHUNK = 64
N_LEFT = 8
MAX_REL = 128
D_CONV = 256
CONV_WIDTH = 31
D_FF = 2816
EPS = 1e-6
LOG2E = math.log2(math.e)
N_SMALL = D_S5 + 3 * D_ATT + 2 * D_CONV

LANES = 128
MXU_DIM = 256
VMEM_LIMIT = 56 * 1024 * 1024

TM = 512
TM_WIDE = 1024
S5_L = 8
S5_CB = LANES
S5_GB = S5_CB // S5_GROUP
S5_NCB = D_S5 // S5_CB
S5_BROW = S5_L * S5_CB
S5_BSTATE = 2 * S5_GB * S5_STATE
S5_TOK = 256
ATT_QB = 4 * CHUNK
ATT_W = ATT_QB + N_LEFT * CHUNK
ATT_KT = MXU_DIM
CONV_TR = 128
CONV_PAD = 32


def _resident(shape):
    nd = len(shape)
    return pl.BlockSpec(shape, lambda *_: (0,) * nd, pipeline_mode=pl.Buffered(1))


def _rmsnorm_bf16(x, g):
    ms = jnp.mean(x * x, axis=-1, keepdims=True)
    return ((x * lax.rsqrt(ms + EPS)) * g).astype(BF16)


def _sigmoid(x):
    return jax.nn.sigmoid(x)


def _ffn_kernel(x_ref, g_ref, wa_ref, wb_ref, wd_ref, o_ref, *, bounds):
    x = x_ref[...]
    h = _rmsnorm_bf16(x, g_ref[...])
    y = jnp.zeros_like(x)
    for f0, f1 in zip(bounds[:-1], bounds[1:]):
        a = jnp.dot(h, wa_ref[:, f0:f1], preferred_element_type=F32)
        b = jnp.dot(h, wb_ref[:, f0:f1], preferred_element_type=F32)
        act = ((a * _sigmoid(a)) * b).astype(BF16)
        y = y + jnp.dot(act, wd_ref[f0:f1, :], preferred_element_type=F32)
    o_ref[...] = x + 0.5 * y


def _ffn_bounds():
    tiles = D_FF // MXU_DIM
    assert tiles * MXU_DIM == D_FF
    return (0, (tiles + 1) // 2 * MXU_DIM, D_FF)


def _ffn(x, g, wa, wb, wd):
    n = x.shape[0]
    tok = pl.BlockSpec((TM_WIDE, D_MODEL), lambda i: (i, 0))
    return pl.pallas_call(
        functools.partial(_ffn_kernel, bounds=_ffn_bounds()),
        out_shape=jax.ShapeDtypeStruct((n, D_MODEL), F32),
        grid=(n // TM_WIDE,),
        in_specs=[tok, _resident((1, D_MODEL)), _resident((D_MODEL, D_FF)),
                  _resident((D_MODEL, D_FF)), _resident((D_FF, D_MODEL))],
        out_specs=tok,
        compiler_params=pltpu.CompilerParams(
            dimension_semantics=("arbitrary",), vmem_limit_bytes=VMEM_LIMIT),
        name="ffn",
    )(x, g, wa, wb, wd)


def _head_pair_norm(x2, gain2):
    lane = lax.broadcasted_iota(jnp.int32, (1, LANES), 1)
    first = lane < HEAD_DIM
    sq = x2 * x2
    s_a = jnp.sum(jnp.where(first, sq, 0.0), axis=-1, keepdims=True)
    s_b = jnp.sum(jnp.where(first, 0.0, sq), axis=-1, keepdims=True)
    ms = jnp.where(first, s_a, s_b) * (1.0 / HEAD_DIM)
    return (x2 * lax.rsqrt(ms + EPS)) * gain2


def _inproj_kernel(x_ref, g_ref, w_ref, qg_ref, kg_ref, u_ref, q_ref, k_ref, v_ref, hc_ref):
    h = _rmsnorm_bf16(x_ref[...], g_ref[...])
    proj = jnp.dot(h, w_ref[...], preferred_element_type=F32)
    u_ref[...] = proj[:, :D_S5]
    q0, k0, v0, z0 = D_S5, D_S5 + D_ATT, D_S5 + 2 * D_ATT, D_S5 + 3 * D_ATT
    scale = HEAD_DIM ** -0.5 * LOG2E
    for p in range(D_ATT // LANES):
        lo = p * LANES
        qn = _head_pair_norm(proj[:, q0 + lo:q0 + lo + LANES], qg_ref[...])
        kn = _head_pair_norm(proj[:, k0 + lo:k0 + lo + LANES], kg_ref[...])
        q_ref[:, lo:lo + LANES] = (qn * scale).astype(BF16)
        k_ref[:, lo:lo + LANES] = kn.astype(BF16)
    v_ref[...] = proj[:, v0:v0 + D_ATT].astype(BF16)
    a = proj[:, z0:z0 + D_CONV]
    gt = proj[:, z0 + D_CONV:z0 + 2 * D_CONV]
    hc_ref[...] = a * _sigmoid(gt)


def _inproj(x, g, w, qg2, kg2):
    n = x.shape[0]
    def tok(width):
        return pl.BlockSpec((TM_WIDE, width), lambda i: (i, 0))
    return pl.pallas_call(
        _inproj_kernel,
        out_shape=(jax.ShapeDtypeStruct((n, D_S5), F32),
                   jax.ShapeDtypeStruct((n, D_ATT), BF16),
                   jax.ShapeDtypeStruct((n, D_ATT), BF16),
                   jax.ShapeDtypeStruct((n, D_ATT), BF16),
                   jax.ShapeDtypeStruct((n, D_CONV), F32)),
        grid=(n // TM_WIDE,),
        in_specs=[tok(D_MODEL), _resident((1, D_MODEL)), _resident((D_MODEL, N_SMALL)),
                  _resident((1, LANES)), _resident((1, LANES))],
        out_specs=(tok(D_S5), tok(D_ATT), tok(D_ATT), tok(D_ATT), tok(D_CONV)),
        compiler_params=pltpu.CompilerParams(
            dimension_semantics=("arbitrary",), vmem_limit_bytes=VMEM_LIMIT),
        name="inproj",
    )(x, g, w, qg2, kg2)


def _s5_tables(lambda_re, lambda_im, log_dt, b_re, b_im, c_re, c_im, d_skip):
    lr = jnp.minimum(lambda_re.astype(F32), -1e-4)
    li = lambda_im.astype(F32)
    dt = jnp.exp(log_dt.astype(F32))[:, None]
    mag = jnp.exp(lr * dt)
    ar = mag * jnp.cos(li * dt)
    ai = mag * jnp.sin(li * dt)
    den = lr * lr + li * li
    coef_r = ((ar - 1.0) * lr + ai * li) / den
    coef_i = (ai * lr - (ar - 1.0) * li) / den
    br = b_re.astype(F32)
    bi = b_im.astype(F32)
    bbar_r = coef_r[..., None] * br - coef_i[..., None] * bi
    bbar_i = coef_r[..., None] * bi + coef_i[..., None] * br
    cr = c_re.astype(F32)
    ci = c_im.astype(F32)

    pr, pi = [jnp.ones_like(ar)], [jnp.zeros_like(ai)]
    for _ in range(S5_L):
        r, i = pr[-1], pi[-1]
        pr.append(r * ar - i * ai)
        pi.append(r * ai + i * ar)
    pr = jnp.stack(pr)
    pi = jnp.stack(pi)

    ncb, gb = S5_NCB, S5_GB
    pw_r = pr[S5_L - 1::-1]
    pw_i = pi[S5_L - 1::-1]
    e_r = pw_r[..., None] * bbar_r[None] - pw_i[..., None] * bbar_i[None]
    e_i = pw_r[..., None] * bbar_i[None] + pw_i[..., None] * bbar_r[None]
    p_blk = jnp.stack([jnp.transpose(e_r, (0, 1, 3, 2)), jnp.transpose(e_i, (0, 1, 3, 2))], axis=3)
    p_blk = p_blk.reshape(S5_L, ncb, gb * S5_GROUP, 2 * S5_STATE)
    p_blk = jnp.transpose(p_blk, (1, 0, 2, 3)).reshape(ncb, S5_BROW, 2 * S5_STATE)

    m_r = pr[:S5_L, :, None, :] * cr[None] - pi[:S5_L, :, None, :] * ci[None]
    m_i = pr[:S5_L, :, None, :] * ci[None] + pi[:S5_L, :, None, :] * cr[None]
    kern = (jnp.sum(m_r[:, :, :, :, None] * bbar_r[None, :, None, :, :], axis=3)
            - jnp.sum(m_i[:, :, :, :, None] * bbar_i[None, :, None, :, :], axis=3))
    t_blk = jnp.transpose(kern, (0, 1, 3, 2)).reshape(S5_L, ncb, S5_CB, S5_GROUP)
    t_blk = jnp.transpose(t_blk, (1, 0, 2, 3))

    q_r = pr[1:, :, None, :] * cr[None] - pi[1:, :, None, :] * ci[None]
    q_i = -(pr[1:, :, None, :] * ci[None] + pi[1:, :, None, :] * cr[None])
    q_blk = jnp.stack([jnp.transpose(q_r, (0, 1, 3, 2)), jnp.transpose(q_i, (0, 1, 3, 2))], axis=1)
    q_blk = q_blk.reshape(S5_L, 2, ncb, gb * S5_STATE, S5_GROUP)
    q_blk = jnp.transpose(q_blk, (2, 0, 1, 3, 4)).reshape(ncb, S5_L, S5_BSTATE, S5_GROUP)

    half_blocks = S5_BSTATE // 2 // LANES
    a_r = pr[S5_L].reshape(ncb, half_blocks, 1, LANES)
    a_i = pi[S5_L].reshape(ncb, half_blocks, 1, LANES)
    d_row = jnp.tile(d_skip.astype(F32).reshape(ncb, 1, S5_CB), (1, 1, S5_L))
    pt, q_mat = _s5_expand(p_blk.astype(BF16), t_blk.astype(BF16), q_blk.astype(BF16))
    return pt, q_mat, a_r, a_i, d_row


def _s5_expand_kernel(p_ref, t_ref, q_ref, pt_ref, qm_ref):
    s = pl.program_id(1)

    def iota(shape, dim):
        return lax.broadcasted_iota(jnp.int32, shape, dim)

    def onehot(rows, cols, row_key, col_key):
        r, c = iota((rows, cols), 0), iota((rows, cols), 1)
        return jnp.where(row_key(r) == col_key(c), 1.0, 0.0)

    half = S5_BSTATE // 2
    rep = onehot(2 * S5_STATE, S5_BSTATE, lambda r: r, lambda c: (c // half) * S5_STATE + c % S5_STATE)
    same = onehot(S5_CB, S5_BSTATE, lambda r: r // S5_GROUP, lambda c: (c % half) // S5_STATE)
    p_full = jnp.dot(p_ref[...], rep.astype(BF16), preferred_element_type=F32) * same
    pt_ref[:, :S5_BSTATE] = p_full.astype(BF16)

    rep_c = onehot(S5_GROUP, S5_CB, lambda r: r, lambda c: c % S5_GROUP).astype(BF16)
    same_g = onehot(S5_CB, S5_CB, lambda r: r // S5_GROUP, lambda c: c // S5_GROUP)
    for t in range(S5_L):
        d = jnp.clip(t - s, 0, S5_L - 1)
        blk = jnp.dot(t_ref[d], rep_c, preferred_element_type=F32) * same_g
        blk = jnp.where(t >= s, blk, 0.0)
        pt_ref[:, S5_BSTATE + t * S5_CB:S5_BSTATE + (t + 1) * S5_CB] = blk.astype(BF16)

    same_q = onehot(S5_BSTATE, S5_CB, lambda r: (r % half) // S5_STATE, lambda c: c // S5_GROUP)
    q_full = jnp.dot(q_ref[...], rep_c, preferred_element_type=F32) * same_q
    qm_ref[...] = q_full.astype(BF16)


def _s5_expand(p_blk, t_blk, q_blk):
    return pl.pallas_call(
        _s5_expand_kernel,
        out_shape=(jax.ShapeDtypeStruct((S5_NCB, S5_BROW, S5_BSTATE + S5_BROW), BF16),
                   jax.ShapeDtypeStruct((S5_NCB, S5_BSTATE, S5_BROW), BF16)),
        grid=(S5_NCB, S5_L),
        in_specs=[pl.BlockSpec((None, S5_CB, 2 * S5_STATE), lambda cb, s: (cb, s, 0)),
                  pl.BlockSpec((None, S5_L, S5_CB, S5_GROUP), lambda cb, s: (cb, 0, 0, 0)),
                  pl.BlockSpec((None, None, S5_BSTATE, S5_GROUP), lambda cb, s: (cb, s, 0, 0))],
        out_specs=(pl.BlockSpec((None, S5_CB, S5_BSTATE + S5_BROW), lambda cb, s: (cb, s, 0)),
                   pl.BlockSpec((None, S5_BSTATE, S5_CB), lambda cb, s: (cb, 0, s))),
        compiler_params=pltpu.CompilerParams(
            dimension_semantics=("arbitrary", "arbitrary"), vmem_limit_bytes=VMEM_LIMIT),
        name="s5_tables",
    )(p_blk, t_blk, q_blk)


def _s5_kernel(ua_ref, ub_ref, pt_ref, q_ref, ar_ref, ai_ref, d_ref, wg_ref, o_ref,
               sr_ref, si_ref, u8_ref, es_ref, ot_ref):
    nb = ua_ref.shape[0]
    ncl = S5_TOK // S5_L
    nlb = S5_BSTATE // LANES
    hlb = nlb // 2
    u_refs = (ua_ref, ub_ref)

    @pl.when(pl.program_id(0) == 0)
    def _():
        sr_ref[...] = jnp.zeros_like(sr_ref)
        si_ref[...] = jnp.zeros_like(si_ref)

    sub = 8
    ncg = ncl // sub
    grp = nb * sub
    for cb in range(S5_NCB):
        for t in range(S5_L):
            for cg in range(ncg):
                blk = u_refs[cb][:, pl.ds(cg * sub * S5_L + t, sub, stride=S5_L), :]
                u8_ref[cb, cg * grp:(cg + 1) * grp, t * S5_CB:(t + 1) * S5_CB] = blk.reshape(grp, LANES)

    ys = []
    for cb in range(S5_NCB):
        u8 = u8_ref[cb]
        u8b = u8.astype(BF16)
        e = jnp.dot(u8b, pt_ref[cb, :, :S5_BSTATE], preferred_element_type=F32)
        for j in range(nlb):
            es_ref[cb, j] = e[:, j * LANES:(j + 1) * LANES]
        ys.append(jnp.dot(u8b, pt_ref[cb, :, S5_BSTATE:], preferred_element_type=F32) + d_ref[cb] * u8)

    a_r = ar_ref[...]
    a_i = ai_ref[...]
    sr = sr_ref[...]
    si = si_ref[...]
    for cl in range(ncl):
        step = pl.ds((cl // sub) * grp + cl % sub, nb, stride=sub)
        e_r = es_ref[:, :hlb, step, :]
        e_i = es_ref[:, hlb:, step, :]
        es_ref[:, :hlb, step, :] = sr
        es_ref[:, hlb:, step, :] = si
        sr, si = a_r * sr - a_i * si + e_r, a_r * si + a_i * sr + e_i
    sr_ref[...] = sr
    si_ref[...] = si

    for cb in range(S5_NCB):
        s_all = jnp.concatenate([es_ref[cb, j] for j in range(nlb)], axis=1).astype(BF16)
        y = ys[cb] + jnp.dot(s_all, q_ref[cb], preferred_element_type=F32)
        ys[cb] = jax.nn.gelu(y).astype(BF16)
    for t in range(S5_L):
        yt = jnp.concatenate([y[:, t * S5_CB:(t + 1) * S5_CB] for y in ys], axis=1)
        ag = jnp.dot(yt, wg_ref[...], preferred_element_type=F32)
        out = ag[:, :D_S5] * _sigmoid(ag[:, D_S5:])
        for hh in range(D_S5 // LANES):
            for cg in range(ncg):
                ot_ref[hh, :, pl.ds(cg * sub * S5_L + t, sub, stride=S5_L), :] = (
                    out[cg * grp:(cg + 1) * grp, hh * LANES:(hh + 1) * LANES].reshape(nb, sub, LANES))
    o_ref[...] = jnp.concatenate([ot_ref[0], ot_ref[1]], axis=-1).astype(BF16)


def _s5(u, pt, q_mat, a_r, a_i, d_row, w_glu):
    nb, seq, _ = u.shape
    rows = nb * (S5_TOK // S5_L)
    nlb = S5_BSTATE // LANES
    assert S5_CB == LANES and S5_NCB == 2
    return pl.pallas_call(
        _s5_kernel,
        out_shape=jax.ShapeDtypeStruct((nb, seq, D_S5), BF16),
        grid=(seq // S5_TOK,),
        in_specs=[pl.BlockSpec((nb, S5_TOK, LANES), lambda i: (0, i, 0)),
                  pl.BlockSpec((nb, S5_TOK, LANES), lambda i: (0, i, 1)),
                  _resident(pt.shape), _resident(q_mat.shape), _resident(a_r.shape),
                  _resident(a_i.shape), _resident(d_row.shape), _resident(w_glu.shape)],
        out_specs=pl.BlockSpec((nb, S5_TOK, D_S5), lambda i: (0, i, 0)),
        scratch_shapes=[pltpu.VMEM((S5_NCB, nlb // 2, nb, LANES), F32),
                        pltpu.VMEM((S5_NCB, nlb // 2, nb, LANES), F32),
                        pltpu.VMEM((S5_NCB, rows, S5_BROW), F32),
                        pltpu.VMEM((S5_NCB, nlb, rows, LANES), F32),
                        pltpu.VMEM((D_S5 // LANES, nb, S5_TOK, LANES), F32)],
        compiler_params=pltpu.CompilerParams(
            dimension_semantics=("arbitrary",), vmem_limit_bytes=VMEM_LIMIT),
        name="s5",
    )(u, u, pt, q_mat, a_r, a_i, d_row, w_glu)


def _attn_bias(rel_bias):
    rb = rel_bias.astype(F32) * LOG2E
    far_past, far_future = rb[:, 2 * MAX_REL:], rb[:, :1]
    period = ATT_QB + ATT_W
    n_const = N_LEFT * CHUNK - MAX_REL
    vec = jnp.concatenate([
        jnp.broadcast_to(far_past, (HEADS, n_const)),
        rb[:, ::-1],
        jnp.broadcast_to(far_future, (HEADS, ATT_W - n_const - 2 * MAX_REL - 1)),
        jnp.broadcast_to(far_past, (HEADS, ATT_QB)),
    ], axis=1)
    toep = jnp.tile(vec, (1, ATT_QB))[:, :ATT_QB * (period - 1)]
    toep = toep.reshape(HEADS, ATT_QB, period - 1)[:, :, :ATT_W]
    r = jnp.arange(ATT_QB)[:, None]
    c = jnp.arange(ATT_W)[None, :] - N_LEFT * CHUNK
    dchunk = r // CHUNK - jnp.floor_divide(c, CHUNK)
    ok = (dchunk >= 0) & (dchunk <= N_LEFT)
    bias = jnp.where(ok[None], toep, -1e30)
    return bias.reshape(HEADS // 2, 2, ATT_QB, ATT_W)


def _attn_kernel(q_ref, k_ref, v_ref, bm_ref, o_ref):
    seq = q_ref.shape[1]
    lane = lax.broadcasted_iota(jnp.int32, (1, LANES), 1)
    first = lane < HEAD_DIM

    def block(q0, k0, width):
        for pr in range(D_ATT // LANES):
            cols = slice(pr * LANES, (pr + 1) * LANES)
            q2 = q_ref[0, pl.ds(q0, ATT_QB), cols]
            outs = []
            for hh in range(2):
                sel = first if hh == 0 else jnp.logical_not(first)
                qh = jnp.where(sel, q2, jnp.zeros_like(q2))
                for c0 in range(0, width, ATT_KT):
                    kw = k_ref[0, pl.ds(k0 + c0, ATT_KT), cols]
                    vw = v_ref[0, pl.ds(k0 + c0, ATT_KT), cols]
                    s = lax.dot_general(qh, kw, (((1,), (1,)), ((), ())),
                                        preferred_element_type=F32)
                    b0 = ATT_W - width + c0
                    s = s + bm_ref[pr, hh, :, b0:b0 + ATT_KT]
                    vh = jnp.where(sel, vw, jnp.ones_like(vw))
                    if c0 == 0:
                        mx = jnp.max(s, axis=-1, keepdims=True)
                        acc = jnp.dot(jnp.exp2(s - mx).astype(BF16), vh, preferred_element_type=F32)
                    else:
                        mx_new = jnp.maximum(mx, jnp.max(s, axis=-1, keepdims=True))
                        p = jnp.exp2(s - mx_new).astype(BF16)
                        acc = acc * jnp.exp2(mx - mx_new) + jnp.dot(p, vh, preferred_element_type=F32)
                        mx = mx_new
                outs.append(acc)
            num = jnp.where(first, outs[0], outs[1])
            den = pltpu.roll(jnp.where(first, outs[1], outs[0]), HEAD_DIM, axis=1)
            o_ref[0, pl.ds(q0, ATT_QB), cols] = (num / den).astype(BF16)

    n_short = N_LEFT * CHUNK // ATT_QB
    for qb in range(n_short):
        block(qb * ATT_QB, 0, (qb + 1) * ATT_QB)

    def body(qb, carry):
        q0 = pl.multiple_of(qb * ATT_QB, ATT_QB)
        k0 = pl.multiple_of(q0 - N_LEFT * CHUNK, ATT_QB)
        block(q0, k0, ATT_W)
        return carry
    lax.fori_loop(n_short, seq // ATT_QB, body, 0)


def _attn(q, k, v, bm):
    nb, seq, _ = q.shape
    seqblk = pl.BlockSpec((1, seq, D_ATT), lambda b: (b, 0, 0))
    return pl.pallas_call(
        _attn_kernel,
        out_shape=jax.ShapeDtypeStruct((nb, seq, D_ATT), BF16),
        grid=(nb,),
        in_specs=[seqblk, seqblk, seqblk, _resident(bm.shape)],
        out_specs=seqblk,
        compiler_params=pltpu.CompilerParams(
            dimension_semantics=("arbitrary",), vmem_limit_bytes=VMEM_LIMIT),
        name="attn",
    )(q, k, v, bm)


def _conv_taps(hc_ref, cw_ref, cb_ref, cbuf_ref, first_tile):
    nh = D_CONV // LANES
    tm = hc_ref.shape[0]

    @pl.when(first_tile)
    def _():
        cbuf_ref[:, :CONV_PAD, :] = jnp.zeros((nh, CONV_PAD, LANES), F32)
    for hh in range(nh):
        cbuf_ref[hh, CONV_PAD:, :] = hc_ref[:, hh * LANES:(hh + 1) * LANES]
    shift = CONV_PAD - (CONV_WIDTH - 1)
    halves = []
    for hh in range(nh):
        tiles = []
        for r0 in range(0, tm, CONV_TR):
            acc = jnp.zeros((CONV_TR, LANES), F32) + cb_ref[hh]
            for j in range(CONV_WIDTH):
                lo = r0 + shift + j
                acc = acc + cw_ref[hh, j:j + 1, :] * cbuf_ref[hh, lo:lo + CONV_TR, :]
            tiles.append(acc)
        halves.append(jnp.concatenate(tiles, axis=0))
    for hh in range(nh):
        cbuf_ref[hh, :CONV_PAD, :] = cbuf_ref[hh, tm:tm + CONV_PAD, :]
    return halves


def _conv_norm(halves, lg_ref, lb_ref):
    acc = jnp.concatenate(halves, axis=-1)
    mu = jnp.mean(acc, axis=-1, keepdims=True)
    cen = acc - mu
    var = jnp.mean(cen * cen, axis=-1, keepdims=True)
    y = (cen * lax.rsqrt(var + EPS)) * lg_ref[...] + lb_ref[...]
    return (y * _sigmoid(y)).astype(BF16)


def _after(v, zero_ref, width):
    bits = pltpu.bitcast(v, jnp.uint32)
    acc = bits[0:8, :]
    for r0 in range(8, v.shape[0], 8):
        acc = acc | bits[r0:r0 + 8, :]
    zero = pltpu.bitcast(acc[0:1, :] & zero_ref[...], F32)
    return jnp.tile(zero, (1, width // LANES)).astype(BF16)


def _merge_kernel(x_ref, s5_ref, at_ref, hc_ref, g_ref, wg_ref, bg_ref, ws_ref, wa_ref, wc_ref, wo_ref,
                  cw_ref, cb_ref, lg_ref, lb_ref, zero_ref, o_ref, cbuf_ref, *, tiles_per_seq):
    halves = _conv_taps(hc_ref, cw_ref, cb_ref, cbuf_ref, pl.program_id(0) % tiles_per_seq == 0)
    x = x_ref[...]
    h = _rmsnorm_bf16(x, g_ref[...])
    lhs = (h, h + _after(halves[0], zero_ref, D_MODEL), h + _after(halves[1], zero_ref, D_MODEL))
    branches = (lambda: s5_ref[...], lambda: at_ref[...], lambda: _conv_norm(halves, lg_ref, lb_ref))
    merged = jnp.zeros_like(x)
    for i, (branch, w_ref) in enumerate(zip(branches, (ws_ref, wa_ref, wc_ref))):
        cols = slice(i * D_MODEL, (i + 1) * D_MODEL)
        logits = jnp.dot(lhs[i], wg_ref[:, cols], preferred_element_type=F32) + bg_ref[:, cols]
        y = jnp.dot(branch(), w_ref[...], preferred_element_type=F32)
        merged = merged + _sigmoid(logits) * y
    o_ref[...] = x + jnp.dot(merged.astype(BF16), wo_ref[...], preferred_element_type=F32)


def _merge(x, s5o, ato, hc, g, w_gate, b_gate, w_s5, w_at, w_cv, w_out, w_dw, b_dw, ln_g, ln_b, seq):
    n = x.shape[0]
    nh = D_CONV // LANES
    zero = jnp.zeros((1, LANES), jnp.uint32)
    assert seq % TM == 0
    def tok(width):
        return pl.BlockSpec((TM, width), lambda i: (i, 0))
    return pl.pallas_call(
        functools.partial(_merge_kernel, tiles_per_seq=seq // TM),
        out_shape=jax.ShapeDtypeStruct((n, D_MODEL), F32),
        grid=(n // TM,),
        in_specs=[tok(D_MODEL), tok(D_S5), tok(D_ATT), tok(D_CONV),
                  _resident((1, D_MODEL)), _resident((D_MODEL, 3 * D_MODEL)),
                  _resident((1, 3 * D_MODEL)), _resident((D_S5, D_MODEL)),
                  _resident((D_ATT, D_MODEL)), _resident((D_CONV, D_MODEL)),
                  _resident((D_MODEL, D_MODEL)), _resident((nh, CONV_WIDTH, LANES)),
                  _resident((nh, 1, LANES)), _resident((1, D_CONV)), _resident((1, D_CONV)),
                  _resident((1, LANES))],
        out_specs=tok(D_MODEL),
        scratch_shapes=[pltpu.VMEM((nh, CONV_PAD + TM, LANES), F32)],
        compiler_params=pltpu.CompilerParams(
            dimension_semantics=("arbitrary",), vmem_limit_bytes=VMEM_LIMIT),
        name="merge",
    )(x, s5o, ato, hc, g, w_gate, b_gate, w_s5, w_at, w_cv, w_out, w_dw, b_dw, ln_g, ln_b, zero)


def kernel(x, ffn1_norm, ffn1_w_up, ffn1_w_down, mix_norm, w_in, b_gate, s5_lambda_re, s5_lambda_im, s5_log_dt, s5_b_re, s5_b_im, s5_c_re, s5_c_im, s5_d, s5_w_glu, w_br_s5, attn_q_gain, attn_k_gain, attn_rel_bias, w_br_attn, conv_w_dw, conv_b_dw, conv_ln_g, conv_ln_b, w_br_conv, w_out, ffn2_norm, ffn2_w_up, ffn2_w_down):
    nb, seq, d = x.shape
    n = nb * seq
    depth = ffn1_norm.shape[0]
    xt = x.reshape(n, d)
    row = lambda v: v.reshape(1, -1).astype(F32)
    for l in range(depth):
        xt = _ffn(xt, row(ffn1_norm[l]), ffn1_w_up[l, :, :D_FF].astype(BF16),
                  ffn1_w_up[l, :, D_FF:].astype(BF16), ffn1_w_down[l].astype(BF16))

        qg2 = jnp.tile(row(attn_q_gain[l]), (1, LANES // HEAD_DIM))
        kg2 = jnp.tile(row(attn_k_gain[l]), (1, LANES // HEAD_DIM))
        u, q, k, v, hc = _inproj(xt, row(mix_norm[l]), w_in[l, :, :N_SMALL].astype(BF16), qg2, kg2)

        pt, q_mat, a_r, a_i, d_row = _s5_tables(
            s5_lambda_re[l], s5_lambda_im[l], s5_log_dt[l], s5_b_re[l], s5_b_im[l],
            s5_c_re[l], s5_c_im[l], s5_d[l])
        s5o = _s5(u.reshape(nb, seq, D_S5), pt, q_mat, a_r, a_i, d_row, s5_w_glu[l].astype(BF16))

        ato = _attn(q.reshape(nb, seq, D_ATT), k.reshape(nb, seq, D_ATT),
                    v.reshape(nb, seq, D_ATT), _attn_bias(attn_rel_bias[l]))

        nh = D_CONV // LANES
        w_dw = conv_w_dw[l].astype(F32).reshape(CONV_WIDTH, nh, LANES).transpose(1, 0, 2)
        xt = _merge(xt, s5o.reshape(n, D_S5), ato.reshape(n, D_ATT), hc,
                    row(mix_norm[l]), w_in[l, :, N_SMALL:].astype(BF16), row(b_gate[l]),
                    w_br_s5[l].astype(BF16), w_br_attn[l].astype(BF16),
                    w_br_conv[l].astype(BF16), w_out[l].astype(BF16),
                    w_dw, conv_b_dw[l].astype(F32).reshape(nh, 1, LANES),
                    row(conv_ln_g[l]), row(conv_ln_b[l]), seq)

        xt = _ffn(xt, row(ffn2_norm[l]), ffn2_w_up[l, :, :D_FF].astype(BF16),
                  ffn2_w_up[l, :, D_FF:].astype(BF16), ffn2_w_down[l].astype(BF16))
    return xt.reshape(nb, seq, d)
```

```python
import functools
import math

import jax
import jax.numpy as jnp
from jax import lax
from jax.experimental import pallas as pl
from jax.experimental.pallas import tpu as pltpu

F32 = jnp.float32
BF16 = jnp.bfloat16

D_MODEL = 1024
D_S5 = 256
S5_GROUP = 16
S5_GROUPS = 16
S5_STATE = 64
D_ATT = 512
HEAD_DIM = 64
HEADS = 8
CHUNK = 64
N_LEFT = 8
MAX_REL = 128
D_CONV = 256
CONV_WIDTH = 31
D_FF = 2816
EPS = 1e-6
LOG2E = math.log2(math.e)
N_SMALL = D_S5 + 3 * D_ATT + 2 * D_CONV

LANES = 128
MXU_DIM = 256
VMEM_LIMIT = 56 * 1024 * 1024

TM = 512
TM_WIDE = 1024
CAST_ROWS = 64
S5_L = 8
S5_CB = LANES
S5_GB = S5_CB // S5_GROUP
S5_NCB = D_S5 // S5_CB
S5_BROW = S5_L * S5_CB
S5_BSTATE = 2 * S5_GB * S5_STATE
S5_TOK = 256
ATT_QB = 4 * CHUNK
ATT_W = ATT_QB + N_LEFT * CHUNK
ATT_KT = MXU_DIM
CONV_TR = 128
CONV_PAD = 32


def _resident(shape):
    nd = len(shape)
    return pl.BlockSpec(shape, lambda *_: (0,) * nd, pipeline_mode=pl.Buffered(1))


def _rmsnorm_bf16(x, g):
    ms = jnp.mean(x * x, axis=-1, keepdims=True)
    return ((x * lax.rsqrt(ms + EPS)) * g).astype(BF16)


def _sigmoid(x):
    return jax.nn.sigmoid(x)


class _Casts:
    def __init__(self, items, steps):
        self.in_specs, self.out_specs, self.out_shape, self.args, self.nblks = [], [], [], [], []
        for stacked, layer in items:
            _, rows, cols = stacked.shape
            nblk = rows // CAST_ROWS
            assert nblk * CAST_ROWS == rows and nblk <= steps
            self.nblks.append(nblk)
            self.in_specs.append(pl.BlockSpec(
                (None, CAST_ROWS, cols),
                lambda i, layer=layer, nblk=nblk: (layer, jnp.minimum(i, nblk - 1), 0)))
            self.out_specs.append(pl.BlockSpec(
                (CAST_ROWS, cols), lambda i, nblk=nblk: (jnp.minimum(i, nblk - 1), 0)))
            self.out_shape.append(jax.ShapeDtypeStruct((rows, cols), BF16))
            self.args.append(stacked)
        self.n = len(items)

    @staticmethod
    def run(step, nblks, src_refs, dst_refs):
        for nblk in sorted(set(nblks)):
            @pl.when(step < nblk)
            def _():
                for n, src, dst in zip(nblks, src_refs, dst_refs):
                    if n == nblk:
                        dst[...] = src[...].astype(BF16)


def _ffn_kernel(x_ref, g_ref, wa_ref, wb_ref, wd_ref, o_ref, *, bounds):
    x = x_ref[...]
    h = _rmsnorm_bf16(x, g_ref[...])
    y = jnp.zeros_like(x)
    for f0, f1 in zip(bounds[:-1], bounds[1:]):
        a = jnp.dot(h, wa_ref[:, f0:f1], preferred_element_type=F32)
        b = jnp.dot(h, wb_ref[:, f0:f1], preferred_element_type=F32)
        act = ((a * _sigmoid(a)) * b).astype(BF16)
        y = y + jnp.dot(act, wd_ref[f0:f1, :], preferred_element_type=F32)
    o_ref[...] = x + 0.5 * y


def _ffn_bounds():
    tiles = D_FF // MXU_DIM
    assert tiles * MXU_DIM == D_FF
    return (0, (tiles + 1) // 2 * MXU_DIM, D_FF)


def _ffn(x, g, w_up, w_down):
    n = x.shape[0]
    tok = pl.BlockSpec((TM_WIDE, D_MODEL), lambda i: (i, 0))
    def up_half(j):
        return pl.BlockSpec((D_MODEL, D_FF), lambda i: (0, j), pipeline_mode=pl.Buffered(1))
    return pl.pallas_call(
        functools.partial(_ffn_kernel, bounds=_ffn_bounds()),
        out_shape=jax.ShapeDtypeStruct((n, D_MODEL), F32),
        grid=(n // TM_WIDE,),
        in_specs=[tok, _resident((1, D_MODEL)), up_half(0), up_half(1), _resident((D_FF, D_MODEL))],
        out_specs=tok,
        compiler_params=pltpu.CompilerParams(
            dimension_semantics=("arbitrary",), vmem_limit_bytes=VMEM_LIMIT),
        name="ffn",
    )(x, g, w_up, w_up, w_down)


def _head_pair_norm(x2, gain2):
    lane = lax.broadcasted_iota(jnp.int32, (1, LANES), 1)
    first = lane < HEAD_DIM
    sq = x2 * x2
    s_a = jnp.sum(jnp.where(first, sq, 0.0), axis=-1, keepdims=True)
    s_b = jnp.sum(jnp.where(first, 0.0, sq), axis=-1, keepdims=True)
    ms = jnp.where(first, s_a, s_b) * (1.0 / HEAD_DIM)
    return (x2 * lax.rsqrt(ms + EPS)) * gain2


def _inproj_kernel(x_ref, g_ref, w_ref, qg_ref, kg_ref, *refs, cast_nblks):
    n_cast = len(cast_nblks)
    u_ref, q_ref, k_ref, v_ref, hc_ref = refs[n_cast:n_cast + 5]
    _Casts.run(pl.program_id(0), cast_nblks, refs[:n_cast], refs[n_cast + 5:])
    h = _rmsnorm_bf16(x_ref[...], g_ref[...])
    proj = jnp.dot(h, w_ref[...], preferred_element_type=F32)
    u_ref[...] = proj[:, :D_S5]
    q0, k0, v0, z0 = D_S5, D_S5 + D_ATT, D_S5 + 2 * D_ATT, D_S5 + 3 * D_ATT
    scale = HEAD_DIM ** -0.5 * LOG2E
    for p in range(D_ATT // LANES):
        lo = p * LANES
        qn = _head_pair_norm(proj[:, q0 + lo:q0 + lo + LANES], qg_ref[...])
        kn = _head_pair_norm(proj[:, k0 + lo:k0 + lo + LANES], kg_ref[...])
        q_ref[:, lo:lo + LANES] = (qn * scale).astype(BF16)
        k_ref[:, lo:lo + LANES] = kn.astype(BF16)
    v_ref[...] = proj[:, v0:v0 + D_ATT].astype(BF16)
    a = proj[:, z0:z0 + D_CONV]
    gt = proj[:, z0 + D_CONV:z0 + 2 * D_CONV]
    hc_ref[...] = a * _sigmoid(gt)


def _inproj(x, g, w_in, qg2, kg2, cast_items):
    n = x.shape[0]
    steps = n // TM_WIDE
    casts = _Casts(cast_items, steps)
    def tok(width):
        return pl.BlockSpec((TM_WIDE, width), lambda i: (i, 0))
    w_small = pl.BlockSpec((D_MODEL, N_SMALL), lambda i: (0, 0), pipeline_mode=pl.Buffered(1))
    return pl.pallas_call(
        functools.partial(_inproj_kernel, cast_nblks=tuple(casts.nblks)),
        out_shape=(jax.ShapeDtypeStruct((n, D_S5), F32),
                   jax.ShapeDtypeStruct((n, D_ATT), BF16),
                   jax.ShapeDtypeStruct((n, D_ATT), BF16),
                   jax.ShapeDtypeStruct((n, D_ATT), BF16),
                   jax.ShapeDtypeStruct((n, D_CONV), F32), *casts.out_shape),
        grid=(steps,),
        in_specs=[tok(D_MODEL), _resident((1, D_MODEL)), w_small,
                  _resident((1, LANES)), _resident((1, LANES)), *casts.in_specs],
        out_specs=(tok(D_S5), tok(D_ATT), tok(D_ATT), tok(D_ATT), tok(D_CONV), *casts.out_specs),
        compiler_params=pltpu.CompilerParams(
            dimension_semantics=("arbitrary",), vmem_limit_bytes=VMEM_LIMIT),
        name="inproj",
    )(x, g, w_in, qg2, kg2, *casts.args)


def _s5_tables(lambda_re, lambda_im, log_dt, b_re, b_im, c_re, c_im, d_skip):
    lr = jnp.minimum(lambda_re.astype(F32), -1e-4)
    li = lambda_im.astype(F32)
    dt = jnp.exp(log_dt.astype(F32))[:, None]
    mag = jnp.exp(lr * dt)
    ar = mag * jnp.cos(li * dt)
    ai = mag * jnp.sin(li * dt)
    den = lr * lr + li * li
    coef_r = ((ar - 1.0) * lr + ai * li) / den
    coef_i = (ai * lr - (ar - 1.0) * li) / den
    br = b_re.astype(F32)
    bi = b_im.astype(F32)
    bbar_r = coef_r[..., None] * br - coef_i[..., None] * bi
    bbar_i = coef_r[..., None] * bi + coef_i[..., None] * br
    cr = c_re.astype(F32)
    ci = c_im.astype(F32)

    pr, pi = [jnp.ones_like(ar)], [jnp.zeros_like(ai)]
    for _ in range(S5_L):
        r, i = pr[-1], pi[-1]
        pr.append(r * ar - i * ai)
        pi.append(r * ai + i * ar)
    pr = jnp.stack(pr)
    pi = jnp.stack(pi)

    ncb, gb = S5_NCB, S5_GB
    pw_r = pr[S5_L - 1::-1]
    pw_i = pi[S5_L - 1::-1]
    e_r = pw_r[..., None] * bbar_r[None] - pw_i[..., None] * bbar_i[None]
    e_i = pw_r[..., None] * bbar_i[None] + pw_i[..., None] * bbar_r[None]
    p_blk = jnp.stack([jnp.transpose(e_r, (0, 1, 3, 2)), jnp.transpose(e_i, (0, 1, 3, 2))], axis=3)
    p_blk = p_blk.reshape(S5_L, ncb, gb * S5_GROUP, 2 * S5_STATE)
    p_blk = jnp.transpose(p_blk, (1, 0, 2, 3)).reshape(ncb, S5_BROW, 2 * S5_STATE)

    m_r = pr[:S5_L, :, None, :] * cr[None] - pi[:S5_L, :, None, :] * ci[None]
    m_i = pr[:S5_L, :, None, :] * ci[None] + pi[:S5_L, :, None, :] * cr[None]
    kern = (jnp.sum(m_r[:, :, :, :, None] * bbar_r[None, :, None, :, :], axis=3)
            - jnp.sum(m_i[:, :, :, :, None] * bbar_i[None, :, None, :, :], axis=3))
    t_blk = jnp.transpose(kern, (0, 1, 3, 2)).reshape(S5_L, ncb, S5_CB, S5_GROUP)
    t_blk = jnp.transpose(t_blk, (1, 0, 2, 3))

    q_r = pr[1:, :, None, :] * cr[None] - pi[1:, :, None, :] * ci[None]
    q_i = -(pr[1:, :, None, :] * ci[None] + pi[1:, :, None, :] * cr[None])
    q_blk = jnp.stack([jnp.transpose(q_r, (0, 1, 3, 2)), jnp.transpose(q_i, (0, 1, 3, 2))], axis=1)
    q_blk = q_blk.reshape(S5_L, 2, ncb, gb * S5_STATE, S5_GROUP)
    q_blk = jnp.transpose(q_blk, (2, 0, 1, 3, 4)).reshape(ncb, S5_L, S5_BSTATE, S5_GROUP)

    half_blocks = S5_BSTATE // 2 // LANES
    a_r = pr[S5_L].reshape(ncb, half_blocks, 1, LANES)
    a_i = pi[S5_L].reshape(ncb, half_blocks, 1, LANES)
    d_row = jnp.tile(d_skip.astype(F32).reshape(ncb, 1, S5_CB), (1, 1, S5_L))
    pt, q_mat = _s5_expand(p_blk.astype(BF16), t_blk.astype(BF16), q_blk.astype(BF16))
    return pt, q_mat, a_r, a_i, d_row


def _s5_expand_kernel(p_ref, t_ref, q_ref, pt_ref, qm_ref):
    s = pl.program_id(1)

    def iota(shape, dim):
        return lax.broadcasted_iota(jnp.int32, shape, dim)

    def onehot(rows, cols, row_key, col_key):
        r, c = iota((rows, cols), 0), iota((rows, cols), 1)
        return jnp.where(row_key(r) == col_key(c), 1.0, 0.0)

    half = S5_BSTATE // 2
    rep = onehot(2 * S5_STATE, S5_BSTATE, lambda r: r, lambda c: (c // half) * S5_STATE + c % S5_STATE)
    same = onehot(S5_CB, S5_BSTATE, lambda r: r // S5_GROUP, lambda c: (c % half) // S5_STATE)
    p_full = jnp.dot(p_ref[...], rep.astype(BF16), preferred_element_type=F32) * same
    pt_ref[:, :S5_BSTATE] = p_full.astype(BF16)

    rep_c = onehot(S5_GROUP, S5_CB, lambda r: r, lambda c: c % S5_GROUP).astype(BF16)
    same_g = onehot(S5_CB, S5_CB, lambda r: r // S5_GROUP, lambda c: c // S5_GROUP)
    for t in range(S5_L):
        d = jnp.clip(t - s, 0, S5_L - 1)
        blk = jnp.dot(t_ref[d], rep_c, preferred_element_type=F32) * same_g
        blk = jnp.where(t >= s, blk, 0.0)
        pt_ref[:, S5_BSTATE + t * S5_CB:S5_BSTATE + (t + 1) * S5_CB] = blk.astype(BF16)

    same_q = onehot(S5_BSTATE, S5_CB, lambda r: (r % half) // S5_STATE, lambda c: c // S5_GROUP)
    q_full = jnp.dot(q_ref[...], rep_c, preferred_element_type=F32) * same_q
    qm_ref[...] = q_full.astype(BF16)


def _s5_expand(p_blk, t_blk, q_blk):
    return pl.pallas_call(
        _s5_expand_kernel,
        out_shape=(jax.ShapeDtypeStruct((S5_NCB, S5_BROW, S5_BSTATE + S5_BROW), BF16),
                   jax.ShapeDtypeStruct((S5_NCB, S5_BSTATE, S5_BROW), BF16)),
        grid=(S5_NCB, S5_L),
        in_specs=[pl.BlockSpec((None, S5_CB, 2 * S5_STATE), lambda cb, s: (cb, s, 0)),
                  pl.BlockSpec((None, S5_L, S5_CB, S5_GROUP), lambda cb, s: (cb, 0, 0, 0)),
                  pl.BlockSpec((None, None, S5_BSTATE, S5_GROUP), lambda cb, s: (cb, s, 0, 0))],
        out_specs=(pl.BlockSpec((None, S5_CB, S5_BSTATE + S5_BROW), lambda cb, s: (cb, s, 0)),
                   pl.BlockSpec((None, S5_BSTATE, S5_CB), lambda cb, s: (cb, 0, s))),
        compiler_params=pltpu.CompilerParams(
            dimension_semantics=("arbitrary", "arbitrary"), vmem_limit_bytes=VMEM_LIMIT),
        name="s5_tables",
    )(p_blk, t_blk, q_blk)


def _s5_kernel(ua_ref, ub_ref, pt_ref, q_ref, ar_ref, ai_ref, d_ref, wg_ref, o_ref,
               sr_ref, si_ref, u8_ref, es_ref, ot_ref):
    nb = ua_ref.shape[0]
    ncl = S5_TOK // S5_L
    nlb = S5_BSTATE // LANES
    hlb = nlb // 2
    u_refs = (ua_ref, ub_ref)

    @pl.when(pl.program_id(0) == 0)
    def _():
        sr_ref[...] = jnp.zeros_like(sr_ref)
        si_ref[...] = jnp.zeros_like(si_ref)

    sub = 8
    ncg = ncl // sub
    grp = nb * sub
    for cb in range(S5_NCB):
        for t in range(S5_L):
            for cg in range(ncg):
                blk = u_refs[cb][:, pl.ds(cg * sub * S5_L + t, sub, stride=S5_L), :]
                u8_ref[cb, cg * grp:(cg + 1) * grp, t * S5_CB:(t + 1) * S5_CB] = blk.reshape(grp, LANES)

    ys = []
    for cb in range(S5_NCB):
        u8 = u8_ref[cb]
        u8b = u8.astype(BF16)
        e = jnp.dot(u8b, pt_ref[cb, :, :S5_BSTATE], preferred_element_type=F32)
        for j in range(nlb):
            es_ref[cb, j] = e[:, j * LANES:(j + 1) * LANES]
        ys.append(jnp.dot(u8b, pt_ref[cb, :, S5_BSTATE:], preferred_element_type=F32) + d_ref[cb] * u8)

    a_r = ar_ref[...]
    a_i = ai_ref[...]
    sr = sr_ref[...]
    si = si_ref[...]
    for cl in range(ncl):
        step = pl.ds((cl // sub) * grp + cl % sub, nb, stride=sub)
        e_r = es_ref[:, :hlb, step, :]
        e_i = es_ref[:, hlb:, step, :]
        es_ref[:, :hlb, step, :] = sr
        es_ref[:, hlb:, step, :] = si
        sr, si = a_r * sr - a_i * si + e_r, a_r * si + a_i * sr + e_i
    sr_ref[...] = sr
    si_ref[...] = si

    for cb in range(S5_NCB):
        s_all = jnp.concatenate([es_ref[cb, j] for j in range(nlb)], axis=1).astype(BF16)
        y = ys[cb] + jnp.dot(s_all, q_ref[cb], preferred_element_type=F32)
        ys[cb] = jax.nn.gelu(y).astype(BF16)
    for t in range(S5_L):
        yt = jnp.concatenate([y[:, t * S5_CB:(t + 1) * S5_CB] for y in ys], axis=1)
        ag = jnp.dot(yt, wg_ref[...], preferred_element_type=F32)
        out = ag[:, :D_S5] * _sigmoid(ag[:, D_S5:])
        for hh in range(D_S5 // LANES):
            for cg in range(ncg):
                ot_ref[hh, :, pl.ds(cg * sub * S5_L + t, sub, stride=S5_L), :] = (
                    out[cg * grp:(cg + 1) * grp, hh * LANES:(hh + 1) * LANES].reshape(nb, sub, LANES))
    o_ref[...] = jnp.concatenate([ot_ref[0], ot_ref[1]], axis=-1).astype(BF16)


def _s5(u, pt, q_mat, a_r, a_i, d_row, w_glu):
    nb, seq, _ = u.shape
    rows = nb * (S5_TOK // S5_L)
    nlb = S5_BSTATE // LANES
    assert S5_CB == LANES and S5_NCB == 2
    return pl.pallas_call(
        _s5_kernel,
        out_shape=jax.ShapeDtypeStruct((nb, seq, D_S5), BF16),
        grid=(seq // S5_TOK,),
        in_specs=[pl.BlockSpec((nb, S5_TOK, LANES), lambda i: (0, i, 0)),
                  pl.BlockSpec((nb, S5_TOK, LANES), lambda i: (0, i, 1)),
                  _resident(pt.shape), _resident(q_mat.shape), _resident(a_r.shape),
                  _resident(a_i.shape), _resident(d_row.shape), _resident(w_glu.shape)],
        out_specs=pl.BlockSpec((nb, S5_TOK, D_S5), lambda i: (0, i, 0)),
        scratch_shapes=[pltpu.VMEM((S5_NCB, nlb // 2, nb, LANES), F32),
                        pltpu.VMEM((S5_NCB, nlb // 2, nb, LANES), F32),
                        pltpu.VMEM((S5_NCB, rows, S5_BROW), F32),
                        pltpu.VMEM((S5_NCB, nlb, rows, LANES), F32),
                        pltpu.VMEM((D_S5 // LANES, nb, S5_TOK, LANES), F32)],
        compiler_params=pltpu.CompilerParams(
            dimension_semantics=("arbitrary",), vmem_limit_bytes=VMEM_LIMIT),
        name="s5",
    )(u, u, pt, q_mat, a_r, a_i, d_row, w_glu)


def _attn_bias(rel_bias):
    rb = rel_bias.astype(F32) * LOG2E
    far_past, far_future = rb[:, 2 * MAX_REL:], rb[:, :1]
    period = ATT_QB + ATT_W
    n_const = N_LEFT * CHUNK - MAX_REL
    vec = jnp.concatenate([
        jnp.broadcast_to(far_past, (HEADS, n_const)),
        rb[:, ::-1],
        jnp.broadcast_to(far_future, (HEADS, ATT_W - n_const - 2 * MAX_REL - 1)),
        jnp.broadcast_to(far_past, (HEADS, ATT_QB)),
    ], axis=1)
    toep = jnp.tile(vec, (1, ATT_QB))[:, :ATT_QB * (period - 1)]
    toep = toep.reshape(HEADS, ATT_QB, period - 1)[:, :, :ATT_W]
    r = jnp.arange(ATT_QB)[:, None]
    c = jnp.arange(ATT_W)[None, :] - N_LEFT * CHUNK
    dchunk = r // CHUNK - jnp.floor_divide(c, CHUNK)
    ok = (dchunk >= 0) & (dchunk <= N_LEFT)
    bias = jnp.where(ok[None], toep, -1e30)
    return bias.reshape(HEADS // 2, 2, ATT_QB, ATT_W)


def _attn_kernel(q_ref, k_ref, v_ref, bm_ref, o_ref):
    seq = q_ref.shape[1]
    lane = lax.broadcasted_iota(jnp.int32, (1, LANES), 1)
    first = lane < HEAD_DIM

    def block(q0, k0, width):
        for pr in range(D_ATT // LANES):
            cols = slice(pr * LANES, (pr + 1) * LANES)
            q2 = q_ref[0, pl.ds(q0, ATT_QB), cols]
            outs = []
            for hh in range(2):
                sel = first if hh == 0 else jnp.logical_not(first)
                qh = jnp.where(sel, q2, jnp.zeros_like(q2))
                for c0 in range(0, width, ATT_KT):
                    kw = k_ref[0, pl.ds(k0 + c0, ATT_KT), cols]
                    vw = v_ref[0, pl.ds(k0 + c0, ATT_KT), cols]
                    s = lax.dot_general(qh, kw, (((1,), (1,)), ((), ())),
                                        preferred_element_type=F32)
                    b0 = ATT_W - width + c0
                    s = s + bm_ref[pr, hh, :, b0:b0 + ATT_KT]
                    vh = jnp.where(sel, vw, jnp.ones_like(vw))
                    if c0 == 0:
                        mx = jnp.max(s, axis=-1, keepdims=True)
                        acc = jnp.dot(jnp.exp2(s - mx).astype(BF16), vh, preferred_element_type=F32)
                    else:
                        mx_new = jnp.maximum(mx, jnp.max(s, axis=-1, keepdims=True))
                        p = jnp.exp2(s - mx_new).astype(BF16)
                        acc = acc * jnp.exp2(mx - mx_new) + jnp.dot(p, vh, preferred_element_type=F32)
                        mx = mx_new
                outs.append(acc)
            num = jnp.where(first, outs[0], outs[1])
            den = pltpu.roll(jnp.where(first, outs[1], outs[0]), HEAD_DIM, axis=1)
            o_ref[0, pl.ds(q0, ATT_QB), cols] = (num / den).astype(BF16)

    n_short = N_LEFT * CHUNK // ATT_QB
    for qb in range(n_short):
        block(qb * ATT_QB, 0, (qb + 1) * ATT_QB)

    def body(qb, carry):
        q0 = pl.multiple_of(qb * ATT_QB, ATT_QB)
        k0 = pl.multiple_of(q0 - N_LEFT * CHUNK, ATT_QB)
        block(q0, k0, ATT_W)
        return carry
    lax.fori_loop(n_short, seq // ATT_QB, body, 0)


def _attn(q, k, v, bm):
    nb, seq, _ = q.shape
    seqblk = pl.BlockSpec((1, seq, D_ATT), lambda b: (b, 0, 0))
    return pl.pallas_call(
        _attn_kernel,
        out_shape=jax.ShapeDtypeStruct((nb, seq, D_ATT), BF16),
        grid=(nb,),
        in_specs=[seqblk, seqblk, seqblk, _resident(bm.shape)],
        out_specs=seqblk,
        compiler_params=pltpu.CompilerParams(
            dimension_semantics=("arbitrary",), vmem_limit_bytes=VMEM_LIMIT),
        name="attn",
    )(q, k, v, bm)


def _conv_taps(hc_ref, cw_ref, cb_ref, cbuf_ref, first_tile):
    nh = D_CONV // LANES
    tm = hc_ref.shape[0]

    @pl.when(first_tile)
    def _():
        cbuf_ref[:, :CONV_PAD, :] = jnp.zeros((nh, CONV_PAD, LANES), F32)
    for hh in range(nh):
        cbuf_ref[hh, CONV_PAD:, :] = hc_ref[:, hh * LANES:(hh + 1) * LANES]
    shift = CONV_PAD - (CONV_WIDTH - 1)
    halves = []
    for hh in range(nh):
        tiles = []
        for r0 in range(0, tm, CONV_TR):
            acc = jnp.zeros((CONV_TR, LANES), F32) + cb_ref[hh]
            for j in range(CONV_WIDTH):
                lo = r0 + shift + j
                acc = acc + cw_ref[hh, j:j + 1, :] * cbuf_ref[hh, lo:lo + CONV_TR, :]
            tiles.append(acc)
        halves.append(jnp.concatenate(tiles, axis=0))
    for hh in range(nh):
        cbuf_ref[hh, :CONV_PAD, :] = cbuf_ref[hh, tm:tm + CONV_PAD, :]
    return halves


def _conv_norm(halves, lg_ref, lb_ref):
    acc = jnp.concatenate(halves, axis=-1)
    mu = jnp.mean(acc, axis=-1, keepdims=True)
    cen = acc - mu
    var = jnp.mean(cen * cen, axis=-1, keepdims=True)
    y = (cen * lax.rsqrt(var + EPS)) * lg_ref[...] + lb_ref[...]
    return (y * _sigmoid(y)).astype(BF16)


def _after(v, zero_ref, width):
    bits = pltpu.bitcast(v, jnp.uint32)
    acc = bits[0:8, :]
    for r0 in range(8, v.shape[0], 8):
        acc = acc | bits[r0:r0 + 8, :]
    zero = pltpu.bitcast(acc[0:1, :] & zero_ref[...], F32)
    return jnp.tile(zero, (1, width // LANES)).astype(BF16)


def _merge_kernel(x_ref, s5_ref, at_ref, hc_ref, g_ref, wg_ref, bg_ref, ws_ref, wa_ref, wc_ref, wo_ref,
                  cw_ref, cb_ref, lg_ref, lb_ref, zero_ref, *refs, tiles_per_seq, cast_nblks):
    n_cast = len(cast_nblks)
    o_ref, cbuf_ref = refs[n_cast], refs[-1]
    _Casts.run(pl.program_id(0), cast_nblks, refs[:n_cast], refs[n_cast + 1:-1])
    halves = _conv_taps(hc_ref, cw_ref, cb_ref, cbuf_ref, pl.program_id(0) % tiles_per_seq == 0)
    x = x_ref[...]
    h = _rmsnorm_bf16(x, g_ref[...])
    lhs = (h, h + _after(halves[0], zero_ref, D_MODEL), h + _after(halves[1], zero_ref, D_MODEL))
    branches = (lambda: s5_ref[...], lambda: at_ref[...], lambda: _conv_norm(halves, lg_ref, lb_ref))
    merged = jnp.zeros_like(x)
    for i, (branch, w_ref) in enumerate(zip(branches, (ws_ref, wa_ref, wc_ref))):
        cols = slice(i * D_MODEL, (i + 1) * D_MODEL)
        gcols = slice(N_SMALL + i * D_MODEL, N_SMALL + (i + 1) * D_MODEL)
        logits = jnp.dot(lhs[i], wg_ref[:, gcols], preferred_element_type=F32) + bg_ref[:, cols]
        y = jnp.dot(branch(), w_ref[...], preferred_element_type=F32)
        merged = merged + _sigmoid(logits) * y
    o_ref[...] = x + jnp.dot(merged.astype(BF16), wo_ref[...], preferred_element_type=F32)


def _merge(x, s5o, ato, hc, g, w_in, b_gate, w_s5, w_at, w_cv, w_out, w_dw, b_dw, ln_g, ln_b, seq,
           cast_items):
    n = x.shape[0]
    nh = D_CONV // LANES
    zero = jnp.zeros((1, LANES), jnp.uint32)
    assert seq % TM == 0
    steps = n // TM
    casts = _Casts(cast_items, steps)
    def tok(width):
        return pl.BlockSpec((TM, width), lambda i: (i, 0))
    return pl.pallas_call(
        functools.partial(_merge_kernel, tiles_per_seq=seq // TM, cast_nblks=tuple(casts.nblks)),
        out_shape=(jax.ShapeDtypeStruct((n, D_MODEL), F32), *casts.out_shape),
        grid=(steps,),
        in_specs=[tok(D_MODEL), tok(D_S5), tok(D_ATT), tok(D_CONV),
                  _resident((1, D_MODEL)), _resident(w_in.shape),
                  _resident((1, 3 * D_MODEL)), _resident((D_S5, D_MODEL)),
                  _resident((D_ATT, D_MODEL)), _resident((D_CONV, D_MODEL)),
                  _resident((D_MODEL, D_MODEL)), _resident((nh, CONV_WIDTH, LANES)),
                  _resident((nh, 1, LANES)), _resident((1, D_CONV)), _resident((1, D_CONV)),
                  _resident((1, LANES)), *casts.in_specs],
        out_specs=(tok(D_MODEL), *casts.out_specs),
        scratch_shapes=[pltpu.VMEM((nh, CONV_PAD + TM, LANES), F32)],
        compiler_params=pltpu.CompilerParams(
            dimension_semantics=("arbitrary",), vmem_limit_bytes=VMEM_LIMIT),
        name="merge",
    )(x, s5o, ato, hc, g, w_in, b_gate, w_s5, w_at, w_cv, w_out, w_dw, b_dw, ln_g, ln_b, zero,
      *casts.args)


def kernel(x, ffn1_norm, ffn1_w_up, ffn1_w_down, mix_norm, w_in, b_gate, s5_lambda_re, s5_lambda_im, s5_log_dt, s5_b_re, s5_b_im, s5_c_re, s5_c_im, s5_d, s5_w_glu, w_br_s5, attn_q_gain, attn_k_gain, attn_rel_bias, w_br_attn, conv_w_dw, conv_b_dw, conv_ln_g, conv_ln_b, w_br_conv, w_out, ffn2_norm, ffn2_w_up, ffn2_w_down):
    nb, seq, d = x.shape
    n = nb * seq
    depth = ffn1_norm.shape[0]
    xt = x.reshape(n, d)
    row = lambda v: v.reshape(1, -1).astype(F32)
    f1_up, f1_down, w_in_l = (ffn1_w_up[0].astype(BF16), ffn1_w_down[0].astype(BF16),
                              w_in[0].astype(BF16))
    for l in range(depth):
        xt = _ffn(xt, row(ffn1_norm[l]), f1_up, f1_down)

        qg2 = jnp.tile(row(attn_q_gain[l]), (1, LANES // HEAD_DIM))
        kg2 = jnp.tile(row(attn_k_gain[l]), (1, LANES // HEAD_DIM))
        own = (ffn2_w_up, ffn2_w_down, w_br_s5, w_br_attn, w_br_conv, w_out, s5_w_glu)
        u, q, k, v, hc, f2_up, f2_down, w_s5, w_at, w_cv, w_o, w_glu = _inproj(
            xt, row(mix_norm[l]), w_in_l, qg2, kg2, [(w, l) for w in own])

        pt, q_mat, a_r, a_i, d_row = _s5_tables(
            s5_lambda_re[l], s5_lambda_im[l], s5_log_dt[l], s5_b_re[l], s5_b_im[l],
            s5_c_re[l], s5_c_im[l], s5_d[l])
        s5o = _s5(u.reshape(nb, seq, D_S5), pt, q_mat, a_r, a_i, d_row, w_glu)

        ato = _attn(q.reshape(nb, seq, D_ATT), k.reshape(nb, seq, D_ATT),
                    v.reshape(nb, seq, D_ATT), _attn_bias(attn_rel_bias[l]))

        nh = D_CONV // LANES
        w_dw = conv_w_dw[l].astype(F32).reshape(CONV_WIDTH, nh, LANES).transpose(1, 0, 2)
        nxt = (ffn1_w_up, ffn1_w_down, w_in) if l + 1 < depth else ()
        xt, *nxt_bf16 = _merge(
            xt, s5o.reshape(n, D_S5), ato.reshape(n, D_ATT), hc, row(mix_norm[l]), w_in_l,
            row(b_gate[l]), w_s5, w_at, w_cv, w_o, w_dw,
            conv_b_dw[l].astype(F32).reshape(nh, 1, LANES), row(conv_ln_g[l]), row(conv_ln_b[l]),
            seq, [(w, l + 1) for w in nxt])
        if nxt_bf16:
            f1_up, f1_down, w_in_l = nxt_bf16

        xt = _ffn(xt, row(ffn2_norm[l]), f2_up, f2_down)
    return xt.reshape(nb, seq, d)
```

```python
import functools
import math

import jax
import jax.numpy as jnp
from jax import lax
from jax.experimental import pallas as pl
from jax.experimental.pallas import tpu as pltpu

F32 = jnp.float32
BF16 = jnp.bfloat16

D_MODEL = 1024
D_S5 = 256
S5_GROUP = 16
S5_GROUPS = 16
S5_STATE = 64
D_ATT = 512
HEAD_DIM = 64
HEADS = 8
CHUNK = 64
N_LEFT = 8
MAX_REL = 128
D_CONV = 256
CONV_WIDTH = 31
D_FF = 2816
EPS = 1e-6
LOG2E = math.log2(math.e)
N_SMALL = D_S5 + 3 * D_ATT + 2 * D_CONV

LANES = 128
MXU_DIM = 256
VMEM_LIMIT = 56 * 1024 * 1024

TM = 512
TM_WIDE = 1024
CAST_ROWS = 64
S5_L = 8
S5_CB = LANES // 2
S5_GB = S5_CB // S5_GROUP
S5_NCB = D_S5 // S5_CB
S5_BROW = S5_L * S5_CB
S5_BSTATE = 2 * S5_GB * S5_STATE
S5_TOK = 256
ATT_QB = 4 * CHUNK
ATT_W = ATT_QB + N_LEFT * CHUNK
ATT_KT = MXU_DIM
CONV_TR = 128
CONV_PAD = 32


def _resident(shape):
    nd = len(shape)
    return pl.BlockSpec(shape, lambda *_: (0,) * nd, pipeline_mode=pl.Buffered(1))


def _rmsnorm_bf16(x, g):
    ms = jnp.mean(x * x, axis=-1, keepdims=True)
    return ((x * lax.rsqrt(ms + EPS)) * g).astype(BF16)


def _sigmoid(x):
    return jax.nn.sigmoid(x)


class _Casts:
    def __init__(self, items, steps):
        self.in_specs, self.out_specs, self.out_shape, self.args, self.nblks = [], [], [], [], []
        for stacked, layer in items:
            _, rows, cols = stacked.shape
            nblk = rows // CAST_ROWS
            assert nblk * CAST_ROWS == rows and nblk <= steps
            self.nblks.append(nblk)
            self.in_specs.append(pl.BlockSpec(
                (None, CAST_ROWS, cols),
                lambda i, layer=layer, nblk=nblk: (layer, jnp.minimum(i, nblk - 1), 0)))
            self.out_specs.append(pl.BlockSpec(
                (CAST_ROWS, cols), lambda i, nblk=nblk: (jnp.minimum(i, nblk - 1), 0)))
            self.out_shape.append(jax.ShapeDtypeStruct((rows, cols), BF16))
            self.args.append(stacked)
        self.n = len(items)

    @staticmethod
    def run(step, nblks, src_refs, dst_refs):
        for nblk in sorted(set(nblks)):
            @pl.when(step < nblk)
            def _():
                for n, src, dst in zip(nblks, src_refs, dst_refs):
                    if n == nblk:
                        dst[...] = src[...].astype(BF16)


def _ffn_kernel(x_ref, g_ref, wa_ref, wb_ref, wd_ref, o_ref, *, bounds):
    x = x_ref[...]
    h = _rmsnorm_bf16(x, g_ref[...])
    y = jnp.zeros_like(x)
    for f0, f1 in zip(bounds[:-1], bounds[1:]):
        a = jnp.dot(h, wa_ref[:, f0:f1], preferred_element_type=F32)
        b = jnp.dot(h, wb_ref[:, f0:f1], preferred_element_type=F32)
        act = ((a * _sigmoid(a)) * b).astype(BF16)
        y = y + jnp.dot(act, wd_ref[f0:f1, :], preferred_element_type=F32)
    o_ref[...] = x + 0.5 * y


def _ffn_bounds():
    tiles = D_FF // MXU_DIM
    assert tiles * MXU_DIM == D_FF
    return (0, (tiles + 1) // 2 * MXU_DIM, D_FF)


def _ffn(x, g, w_up, w_down):
    n = x.shape[0]
    tok = pl.BlockSpec((TM_WIDE, D_MODEL), lambda i: (i, 0))
    def up_half(j):
        return pl.BlockSpec((D_MODEL, D_FF), lambda i: (0, j), pipeline_mode=pl.Buffered(1))
    return pl.pallas_call(
        functools.partial(_ffn_kernel, bounds=_ffn_bounds()),
        out_shape=jax.ShapeDtypeStruct((n, D_MODEL), F32),
        grid=(n // TM_WIDE,),
        in_specs=[tok, _resident((1, D_MODEL)), up_half(0), up_half(1), _resident((D_FF, D_MODEL))],
        out_specs=tok,
        compiler_params=pltpu.CompilerParams(
            dimension_semantics=("arbitrary",), vmem_limit_bytes=VMEM_LIMIT),
        name="ffn",
    )(x, g, w_up, w_up, w_down)


def _head_pair_norm(x2, gain2):
    lane = lax.broadcasted_iota(jnp.int32, (1, LANES), 1)
    first = lane < HEAD_DIM
    sq = x2 * x2
    s_a = jnp.sum(jnp.where(first, sq, 0.0), axis=-1, keepdims=True)
    s_b = jnp.sum(jnp.where(first, 0.0, sq), axis=-1, keepdims=True)
    ms = jnp.where(first, s_a, s_b) * (1.0 / HEAD_DIM)
    return (x2 * lax.rsqrt(ms + EPS)) * gain2


def _inproj_kernel(x_ref, g_ref, w_ref, qg_ref, kg_ref, *refs, cast_nblks):
    n_cast = len(cast_nblks)
    u_ref, q_ref, k_ref, v_ref, hc_ref = refs[n_cast:n_cast + 5]
    _Casts.run(pl.program_id(0), cast_nblks, refs[:n_cast], refs[n_cast + 5:])
    h = _rmsnorm_bf16(x_ref[...], g_ref[...])
    proj = jnp.dot(h, w_ref[...], preferred_element_type=F32)
    u_ref[...] = proj[:, :D_S5]
    q0, k0, v0, z0 = D_S5, D_S5 + D_ATT, D_S5 + 2 * D_ATT, D_S5 + 3 * D_ATT
    scale = HEAD_DIM ** -0.5 * LOG2E
    for p in range(D_ATT // LANES):
        lo = p * LANES
        qn = _head_pair_norm(proj[:, q0 + lo:q0 + lo + LANES], qg_ref[...])
        kn = _head_pair_norm(proj[:, k0 + lo:k0 + lo + LANES], kg_ref[...])
        q_ref[:, lo:lo + LANES] = (qn * scale).astype(BF16)
        k_ref[:, lo:lo + LANES] = kn.astype(BF16)
    v_ref[...] = proj[:, v0:v0 + D_ATT].astype(BF16)
    a = proj[:, z0:z0 + D_CONV]
    gt = proj[:, z0 + D_CONV:z0 + 2 * D_CONV]
    hc_ref[...] = a * _sigmoid(gt)


def _inproj(x, g, w_in, qg2, kg2, cast_items):
    n = x.shape[0]
    steps = n // TM_WIDE
    casts = _Casts(cast_items, steps)
    def tok(width):
        return pl.BlockSpec((TM_WIDE, width), lambda i: (i, 0))
    w_small = pl.BlockSpec((D_MODEL, N_SMALL), lambda i: (0, 0), pipeline_mode=pl.Buffered(1))
    return pl.pallas_call(
        functools.partial(_inproj_kernel, cast_nblks=tuple(casts.nblks)),
        out_shape=(jax.ShapeDtypeStruct((n, D_S5), F32),
                   jax.ShapeDtypeStruct((n, D_ATT), BF16),
                   jax.ShapeDtypeStruct((n, D_ATT), BF16),
                   jax.ShapeDtypeStruct((n, D_ATT), BF16),
                   jax.ShapeDtypeStruct((n, D_CONV), F32), *casts.out_shape),
        grid=(steps,),
        in_specs=[tok(D_MODEL), _resident((1, D_MODEL)), w_small,
                  _resident((1, LANES)), _resident((1, LANES)), *casts.in_specs],
        out_specs=(tok(D_S5), tok(D_ATT), tok(D_ATT), tok(D_ATT), tok(D_CONV), *casts.out_specs),
        compiler_params=pltpu.CompilerParams(
            dimension_semantics=("arbitrary",), vmem_limit_bytes=VMEM_LIMIT),
        name="inproj",
    )(x, g, w_in, qg2, kg2, *casts.args)


def _s5_tables(lambda_re, lambda_im, log_dt, b_re, b_im, c_re, c_im, d_skip):
    lr = jnp.minimum(lambda_re.astype(F32), -1e-4)
    li = lambda_im.astype(F32)
    dt = jnp.exp(log_dt.astype(F32))[:, None]
    mag = jnp.exp(lr * dt)
    ar = mag * jnp.cos(li * dt)
    ai = mag * jnp.sin(li * dt)
    den = lr * lr + li * li
    coef_r = ((ar - 1.0) * lr + ai * li) / den
    coef_i = (ai * lr - (ar - 1.0) * li) / den
    br = b_re.astype(F32)
    bi = b_im.astype(F32)
    bbar_r = coef_r[..., None] * br - coef_i[..., None] * bi
    bbar_i = coef_r[..., None] * bi + coef_i[..., None] * br
    cr = c_re.astype(F32)
    ci = c_im.astype(F32)

    pr, pi = [jnp.ones_like(ar)], [jnp.zeros_like(ai)]
    for _ in range(S5_L):
        r, i = pr[-1], pi[-1]
        pr.append(r * ar - i * ai)
        pi.append(r * ai + i * ar)
    pr = jnp.stack(pr)
    pi = jnp.stack(pi)

    ncb, gb = S5_NCB, S5_GB
    pw_r = pr[S5_L - 1::-1]
    pw_i = pi[S5_L - 1::-1]
    e_r = pw_r[..., None] * bbar_r[None] - pw_i[..., None] * bbar_i[None]
    e_i = pw_r[..., None] * bbar_i[None] + pw_i[..., None] * bbar_r[None]
    p_blk = jnp.stack([jnp.transpose(e_r, (0, 1, 3, 2)), jnp.transpose(e_i, (0, 1, 3, 2))], axis=3)
    p_blk = p_blk.reshape(S5_L, ncb, gb * S5_GROUP, 2 * S5_STATE)
    p_blk = jnp.transpose(p_blk, (1, 0, 2, 3)).reshape(ncb, S5_BROW, 2 * S5_STATE)

    m_r = pr[:S5_L, :, None, :] * cr[None] - pi[:S5_L, :, None, :] * ci[None]
    m_i = pr[:S5_L, :, None, :] * ci[None] + pi[:S5_L, :, None, :] * cr[None]
    kern = (jnp.sum(m_r[:, :, :, :, None] * bbar_r[None, :, None, :, :], axis=3)
            - jnp.sum(m_i[:, :, :, :, None] * bbar_i[None, :, None, :, :], axis=3))
    t_blk = jnp.transpose(kern, (0, 1, 3, 2)).reshape(S5_L, ncb, S5_CB, S5_GROUP)
    t_blk = jnp.transpose(t_blk, (1, 0, 2, 3))

    q_r = pr[1:, :, None, :] * cr[None] - pi[1:, :, None, :] * ci[None]
    q_i = -(pr[1:, :, None, :] * ci[None] + pi[1:, :, None, :] * cr[None])
    q_blk = jnp.stack([jnp.transpose(q_r, (0, 1, 3, 2)), jnp.transpose(q_i, (0, 1, 3, 2))], axis=1)
    q_blk = q_blk.reshape(S5_L, 2, ncb, gb * S5_STATE, S5_GROUP)
    q_blk = jnp.transpose(q_blk, (2, 0, 1, 3, 4)).reshape(ncb, S5_L, S5_BSTATE, S5_GROUP)

    half_blocks = S5_BSTATE // 2 // LANES
    a_r = pr[S5_L].reshape(ncb, half_blocks, 1, LANES)
    a_i = pi[S5_L].reshape(ncb, half_blocks, 1, LANES)
    d_row = jnp.tile(d_skip.astype(F32).reshape(ncb, 1, S5_CB), (1, 1, S5_L))
    pt, q_mat = _s5_expand(p_blk.astype(BF16), t_blk.astype(BF16), q_blk.astype(BF16))
    return pt, q_mat, a_r, a_i, d_row


def _s5_expand_kernel(p_ref, t_ref, q_ref, pt_ref, qm_ref):
    def iota(shape, dim):
        return lax.broadcasted_iota(jnp.int32, shape, dim)

    def onehot(rows, cols, row_key, col_key):
        r, c = iota((rows, cols), 0), iota((rows, cols), 1)
        return jnp.where(row_key(r) == col_key(c), 1.0, 0.0)

    half = S5_BSTATE // 2
    rep = onehot(2 * S5_STATE, S5_BSTATE, lambda r: r,
                 lambda c: (c // half) * S5_STATE + c % S5_STATE).astype(BF16)
    same = onehot(S5_CB, S5_BSTATE, lambda r: r // S5_GROUP, lambda c: (c % half) // S5_STATE)
    rep_c = onehot(S5_GROUP, S5_CB, lambda r: r, lambda c: c % S5_GROUP).astype(BF16)
    same_g = onehot(S5_CB, S5_CB, lambda r: r // S5_GROUP, lambda c: c // S5_GROUP)
    same_q = onehot(S5_BSTATE, S5_CB, lambda r: (r % half) // S5_STATE, lambda c: c // S5_GROUP)
    lag_blk = [(jnp.dot(t_ref[d], rep_c, preferred_element_type=F32) * same_g).astype(BF16)
               for d in range(S5_L)]
    zero_blk = jnp.zeros((S5_CB, S5_CB), BF16)
    for s in range(S5_L):
        rows = slice(s * S5_CB, (s + 1) * S5_CB)
        p_full = jnp.dot(p_ref[rows, :], rep, preferred_element_type=F32) * same
        pt_ref[rows, :S5_BSTATE] = p_full.astype(BF16)
        for t in range(S5_L):
            cols = slice(S5_BSTATE + t * S5_CB, S5_BSTATE + (t + 1) * S5_CB)
            pt_ref[rows, cols] = lag_blk[t - s] if t >= s else zero_blk
    for t in range(S5_L):
        q_full = jnp.dot(q_ref[t], rep_c, preferred_element_type=F32) * same_q
        qm_ref[:, t * S5_CB:(t + 1) * S5_CB] = q_full.astype(BF16)


def _s5_expand(p_blk, t_blk, q_blk):
    def blk(shape):
        return pl.BlockSpec((None,) + shape, lambda cb: (cb,) + (0,) * len(shape))
    return pl.pallas_call(
        _s5_expand_kernel,
        out_shape=(jax.ShapeDtypeStruct((S5_NCB, S5_BROW, S5_BSTATE + S5_BROW), BF16),
                   jax.ShapeDtypeStruct((S5_NCB, S5_BSTATE, S5_BROW), BF16)),
        grid=(S5_NCB,),
        in_specs=[blk((S5_BROW, 2 * S5_STATE)), blk((S5_L, S5_CB, S5_GROUP)),
                  blk((S5_L, S5_BSTATE, S5_GROUP))],
        out_specs=(blk((S5_BROW, S5_BSTATE + S5_BROW)), blk((S5_BSTATE, S5_BROW))),
        compiler_params=pltpu.CompilerParams(
            dimension_semantics=("arbitrary",), vmem_limit_bytes=VMEM_LIMIT),
        name="s5_tables",
    )(p_blk, t_blk, q_blk)


def _s5_kernel(ua_ref, ub_ref, pt_ref, q_ref, ar_ref, ai_ref, d_ref, wg_ref, o_ref,
               sr_ref, si_ref, u8_ref, es_ref, ot_ref):
    nb = ua_ref.shape[0]
    ncl = S5_TOK // S5_L
    nlb = S5_BSTATE // LANES
    hlb = nlb // 2
    u_refs = (ua_ref, ub_ref)
    low = lax.broadcasted_iota(jnp.int32, (1, LANES), 1) < S5_CB

    def interleave(a, b):
        return (jnp.where(low, a, pltpu.roll(b, S5_CB, axis=1)),
                jnp.where(low, pltpu.roll(a, S5_CB, axis=1), b))

    @pl.when(pl.program_id(0) == 0)
    def _():
        sr_ref[...] = jnp.zeros_like(sr_ref)
        si_ref[...] = jnp.zeros_like(si_ref)

    sub = 8
    ncg = ncl // sub
    grp = nb * sub
    for hh, u_ref in enumerate(u_refs):
        for t2 in range(S5_L // 2):
            for cg in range(ncg):
                tok = [u_ref[:, pl.ds(cg * sub * S5_L + 2 * t2 + k, sub, stride=S5_L), :]
                       .reshape(grp, LANES) for k in range(2)]
                for k, blk in enumerate(interleave(*tok)):
                    u8_ref[2 * hh + k, cg * grp:(cg + 1) * grp, t2 * LANES:(t2 + 1) * LANES] = blk

    ys = []
    for cb in range(S5_NCB):
        u8 = u8_ref[cb]
        u8b = u8.astype(BF16)
        e = jnp.dot(u8b, pt_ref[cb, :, :S5_BSTATE], preferred_element_type=F32)
        for j in range(nlb):
            es_ref[cb, j] = e[:, j * LANES:(j + 1) * LANES]
        ys.append(jnp.dot(u8b, pt_ref[cb, :, S5_BSTATE:], preferred_element_type=F32) + d_ref[cb] * u8)

    a_r = ar_ref[...]
    a_i = ai_ref[...]
    sr = sr_ref[...]
    si = si_ref[...]
    for cl in range(ncl):
        step = pl.ds((cl // sub) * grp + cl % sub, nb, stride=sub)
        e_r = es_ref[:, :hlb, step, :]
        e_i = es_ref[:, hlb:, step, :]
        es_ref[:, :hlb, step, :] = sr
        es_ref[:, hlb:, step, :] = si
        sr, si = a_r * sr - a_i * si + e_r, a_r * si + a_i * sr + e_i
    sr_ref[...] = sr
    si_ref[...] = si

    for cb in range(S5_NCB):
        s_all = jnp.concatenate([es_ref[cb, j] for j in range(nlb)], axis=1).astype(BF16)
        y = ys[cb] + jnp.dot(s_all, q_ref[cb], preferred_element_type=F32)
        ys[cb] = jax.nn.gelu(y)
    for t2 in range(S5_L // 2):
        lanes = slice(t2 * LANES, (t2 + 1) * LANES)
        halves = [interleave(ys[2 * hh][:, lanes], ys[2 * hh + 1][:, lanes]) for hh in range(2)]
        for k in range(2):
            t = 2 * t2 + k
            yt = jnp.concatenate([halves[0][k], halves[1][k]], axis=1).astype(BF16)
            ag = jnp.dot(yt, wg_ref[...], preferred_element_type=F32)
            out = ag[:, :D_S5] * _sigmoid(ag[:, D_S5:])
            for hh in range(D_S5 // LANES):
                for cg in range(ncg):
                    ot_ref[hh, :, pl.ds(cg * sub * S5_L + t, sub, stride=S5_L), :] = (
                        out[cg * grp:(cg + 1) * grp, hh * LANES:(hh + 1) * LANES].reshape(nb, sub, LANES))
    o_ref[...] = jnp.concatenate([ot_ref[0], ot_ref[1]], axis=-1).astype(BF16)


def _s5(u, pt, q_mat, a_r, a_i, d_row, w_glu):
    nb, seq, _ = u.shape
    rows = nb * (S5_TOK // S5_L)
    nlb = S5_BSTATE // LANES
    assert 2 * S5_CB == LANES and D_S5 == 2 * LANES
    return pl.pallas_call(
        _s5_kernel,
        out_shape=jax.ShapeDtypeStruct((nb, seq, D_S5), BF16),
        grid=(seq // S5_TOK,),
        in_specs=[pl.BlockSpec((nb, S5_TOK, LANES), lambda i: (0, i, 0)),
                  pl.BlockSpec((nb, S5_TOK, LANES), lambda i: (0, i, 1)),
                  _resident(pt.shape), _resident(q_mat.shape), _resident(a_r.shape),
                  _resident(a_i.shape), _resident(d_row.shape), _resident(w_glu.shape)],
        out_specs=pl.BlockSpec((nb, S5_TOK, D_S5), lambda i: (0, i, 0)),
        scratch_shapes=[pltpu.VMEM((S5_NCB, nlb // 2, nb, LANES), F32),
                        pltpu.VMEM((S5_NCB, nlb // 2, nb, LANES), F32),
                        pltpu.VMEM((S5_NCB, rows, S5_BROW), F32),
                        pltpu.VMEM((S5_NCB, nlb, rows, LANES), F32),
                        pltpu.VMEM((D_S5 // LANES, nb, S5_TOK, LANES), F32)],
        compiler_params=pltpu.CompilerParams(
            dimension_semantics=("arbitrary",), vmem_limit_bytes=VMEM_LIMIT),
        name="s5",
    )(u, u, pt, q_mat, a_r, a_i, d_row, w_glu)


def _attn_bias(rel_bias):
    rb = rel_bias.astype(F32) * LOG2E
    far_past, far_future = rb[:, 2 * MAX_REL:], rb[:, :1]
    period = ATT_QB + ATT_W
    n_const = N_LEFT * CHUNK - MAX_REL
    vec = jnp.concatenate([
        jnp.broadcast_to(far_past, (HEADS, n_const)),
        rb[:, ::-1],
        jnp.broadcast_to(far_future, (HEADS, ATT_W - n_const - 2 * MAX_REL - 1)),
        jnp.broadcast_to(far_past, (HEADS, ATT_QB)),
    ], axis=1)
    toep = jnp.tile(vec, (1, ATT_QB))[:, :ATT_QB * (period - 1)]
    toep = toep.reshape(HEADS, ATT_QB, period - 1)[:, :, :ATT_W]
    r = jnp.arange(ATT_QB)[:, None]
    c = jnp.arange(ATT_W)[None, :] - N_LEFT * CHUNK
    dchunk = r // CHUNK - jnp.floor_divide(c, CHUNK)
    ok = (dchunk >= 0) & (dchunk <= N_LEFT)
    bias = jnp.where(ok[None], toep, -1e30)
    return bias.reshape(HEADS // 2, 2, ATT_QB, ATT_W)


def _attn_kernel(q_ref, k_ref, v_ref, bm_ref, o_ref):
    seq = q_ref.shape[1]
    lane = lax.broadcasted_iota(jnp.int32, (1, LANES), 1)
    first = lane < HEAD_DIM

    def block(q0, k0, width):
        for pr in range(D_ATT // LANES):
            cols = slice(pr * LANES, (pr + 1) * LANES)
            q2 = q_ref[0, pl.ds(q0, ATT_QB), cols]
            outs = []
            for hh in range(2):
                sel = first if hh == 0 else jnp.logical_not(first)
                qh = jnp.where(sel, q2, jnp.zeros_like(q2))
                for c0 in range(0, width, ATT_KT):
                    kw = k_ref[0, pl.ds(k0 + c0, ATT_KT), cols]
                    vw = v_ref[0, pl.ds(k0 + c0, ATT_KT), cols]
                    s = lax.dot_general(qh, kw, (((1,), (1,)), ((), ())),
                                        preferred_element_type=F32)
                    b0 = ATT_W - width + c0
                    s = s + bm_ref[pr, hh, :, b0:b0 + ATT_KT]
                    vh = jnp.where(sel, vw, jnp.ones_like(vw))
                    if c0 == 0:
                        mx = jnp.max(s, axis=-1, keepdims=True)
                        acc = jnp.dot(jnp.exp2(s - mx).astype(BF16), vh, preferred_element_type=F32)
                    else:
                        mx_new = jnp.maximum(mx, jnp.max(s, axis=-1, keepdims=True))
                        p = jnp.exp2(s - mx_new).astype(BF16)
                        acc = acc * jnp.exp2(mx - mx_new) + jnp.dot(p, vh, preferred_element_type=F32)
                        mx = mx_new
                outs.append(acc)
            num = jnp.where(first, outs[0], outs[1])
            den = pltpu.roll(jnp.where(first, outs[1], outs[0]), HEAD_DIM, axis=1)
            o_ref[0, pl.ds(q0, ATT_QB), cols] = (num / den).astype(BF16)

    n_short = N_LEFT * CHUNK // ATT_QB
    for qb in range(n_short):
        block(qb * ATT_QB, 0, (qb + 1) * ATT_QB)

    def body(qb, carry):
        q0 = pl.multiple_of(qb * ATT_QB, ATT_QB)
        k0 = pl.multiple_of(q0 - N_LEFT * CHUNK, ATT_QB)
        block(q0, k0, ATT_W)
        return carry
    lax.fori_loop(n_short, seq // ATT_QB, body, 0)


def _attn(q, k, v, bm):
    nb, seq, _ = q.shape
    seqblk = pl.BlockSpec((1, seq, D_ATT), lambda b: (b, 0, 0))
    return pl.pallas_call(
        _attn_kernel,
        out_shape=jax.ShapeDtypeStruct((nb, seq, D_ATT), BF16),
        grid=(nb,),
        in_specs=[seqblk, seqblk, seqblk, _resident(bm.shape)],
        out_specs=seqblk,
        compiler_params=pltpu.CompilerParams(
            dimension_semantics=("arbitrary",), vmem_limit_bytes=VMEM_LIMIT),
        name="attn",
    )(q, k, v, bm)


def _conv_taps(hc_ref, cw_ref, cb_ref, cbuf_ref, first_tile):
    nh = D_CONV // LANES
    tm = hc_ref.shape[0]

    @pl.when(first_tile)
    def _():
        cbuf_ref[:, :CONV_PAD, :] = jnp.zeros((nh, CONV_PAD, LANES), F32)
    for hh in range(nh):
        cbuf_ref[hh, CONV_PAD:, :] = hc_ref[:, hh * LANES:(hh + 1) * LANES]
    shift = CONV_PAD - (CONV_WIDTH - 1)
    halves = []
    for hh in range(nh):
        tiles = []
        for r0 in range(0, tm, CONV_TR):
            acc = jnp.zeros((CONV_TR, LANES), F32) + cb_ref[hh]
            for j in range(CONV_WIDTH):
                lo = r0 + shift + j
                acc = acc + cw_ref[hh, j:j + 1, :] * cbuf_ref[hh, lo:lo + CONV_TR, :]
            tiles.append(acc)
        halves.append(jnp.concatenate(tiles, axis=0))
    for hh in range(nh):
        cbuf_ref[hh, :CONV_PAD, :] = cbuf_ref[hh, tm:tm + CONV_PAD, :]
    return halves


def _conv_norm(halves, lg_ref, lb_ref):
    acc = jnp.concatenate(halves, axis=-1)
    mu = jnp.mean(acc, axis=-1, keepdims=True)
    cen = acc - mu
    var = jnp.mean(cen * cen, axis=-1, keepdims=True)
    y = (cen * lax.rsqrt(var + EPS)) * lg_ref[...] + lb_ref[...]
    return (y * _sigmoid(y)).astype(BF16)


def _after(v, zero_ref, width):
    bits = pltpu.bitcast(v, jnp.uint32)
    acc = bits[0:8, :]
    for r0 in range(8, v.shape[0], 8):
        acc = acc | bits[r0:r0 + 8, :]
    zero = pltpu.bitcast(acc[0:1, :] & zero_ref[...], F32)
    return jnp.tile(zero, (1, width // LANES)).astype(BF16)


def _merge_kernel(x_ref, s5_ref, at_ref, hc_ref, g_ref, wg_ref, bg_ref, ws_ref, wa_ref, wc_ref, wo_ref,
                  cw_ref, cb_ref, lg_ref, lb_ref, zero_ref, *refs, tiles_per_seq, cast_nblks):
    n_cast = len(cast_nblks)
    o_ref, cbuf_ref = refs[n_cast], refs[-1]
    _Casts.run(pl.program_id(0), cast_nblks, refs[:n_cast], refs[n_cast + 1:-1])
    halves = _conv_taps(hc_ref, cw_ref, cb_ref, cbuf_ref, pl.program_id(0) % tiles_per_seq == 0)
    x = x_ref[...]
    h = _rmsnorm_bf16(x, g_ref[...])
    lhs = (h, h + _after(halves[0], zero_ref, D_MODEL), h + _after(halves[1], zero_ref, D_MODEL))
    branches = (lambda: s5_ref[...], lambda: at_ref[...], lambda: _conv_norm(halves, lg_ref, lb_ref))
    merged = jnp.zeros_like(x)
    for i, (branch, w_ref) in enumerate(zip(branches, (ws_ref, wa_ref, wc_ref))):
        cols = slice(i * D_MODEL, (i + 1) * D_MODEL)
        gcols = slice(N_SMALL + i * D_MODEL, N_SMALL + (i + 1) * D_MODEL)
        logits = jnp.dot(lhs[i], wg_ref[:, gcols], preferred_element_type=F32) + bg_ref[:, cols]
        y = jnp.dot(branch(), w_ref[...], preferred_element_type=F32)
        merged = merged + _sigmoid(logits) * y
    o_ref[...] = x + jnp.dot(merged.astype(BF16), wo_ref[...], preferred_element_type=F32)


def _merge(x, s5o, ato, hc, g, w_in, b_gate, w_s5, w_at, w_cv, w_out, w_dw, b_dw, ln_g, ln_b, seq,
           cast_items):
    n = x.shape[0]
    nh = D_CONV // LANES
    zero = jnp.zeros((1, LANES), jnp.uint32)
    assert seq % TM == 0
    steps = n // TM
    casts = _Casts(cast_items, steps)
    def tok(width):
        return pl.BlockSpec((TM, width), lambda i: (i, 0))
    return pl.pallas_call(
        functools.partial(_merge_kernel, tiles_per_seq=seq // TM, cast_nblks=tuple(casts.nblks)),
        out_shape=(jax.ShapeDtypeStruct((n, D_MODEL), F32), *casts.out_shape),
        grid=(steps,),
        in_specs=[tok(D_MODEL), tok(D_S5), tok(D_ATT), tok(D_CONV),
                  _resident((1, D_MODEL)), _resident(w_in.shape),
                  _resident((1, 3 * D_MODEL)), _resident((D_S5, D_MODEL)),
                  _resident((D_ATT, D_MODEL)), _resident((D_CONV, D_MODEL)),
                  _resident((D_MODEL, D_MODEL)), _resident((nh, CONV_WIDTH, LANES)),
                  _resident((nh, 1, LANES)), _resident((1, D_CONV)), _resident((1, D_CONV)),
                  _resident((1, LANES)), *casts.in_specs],
        out_specs=(tok(D_MODEL), *casts.out_specs),
        scratch_shapes=[pltpu.VMEM((nh, CONV_PAD + TM, LANES), F32)],
        compiler_params=pltpu.CompilerParams(
            dimension_semantics=("arbitrary",), vmem_limit_bytes=VMEM_LIMIT),
        name="merge",
    )(x, s5o, ato, hc, g, w_in, b_gate, w_s5, w_at, w_cv, w_out, w_dw, b_dw, ln_g, ln_b, zero,
      *casts.args)


def kernel(x, ffn1_norm, ffn1_w_up, ffn1_w_down, mix_norm, w_in, b_gate, s5_lambda_re, s5_lambda_im, s5_log_dt, s5_b_re, s5_b_im, s5_c_re, s5_c_im, s5_d, s5_w_glu, w_br_s5, attn_q_gain, attn_k_gain, attn_rel_bias, w_br_attn, conv_w_dw, conv_b_dw, conv_ln_g, conv_ln_b, w_br_conv, w_out, ffn2_norm, ffn2_w_up, ffn2_w_down):
    nb, seq, d = x.shape
    n = nb * seq
    depth = ffn1_norm.shape[0]
    xt = x.reshape(n, d)
    row = lambda v: v.reshape(1, -1).astype(F32)
    f1_up, f1_down, w_in_l = (ffn1_w_up[0].astype(BF16), ffn1_w_down[0].astype(BF16),
                              w_in[0].astype(BF16))
    for l in range(depth):
        xt = _ffn(xt, row(ffn1_norm[l]), f1_up, f1_down)

        qg2 = jnp.tile(row(attn_q_gain[l]), (1, LANES // HEAD_DIM))
        kg2 = jnp.tile(row(attn_k_gain[l]), (1, LANES // HEAD_DIM))
        own = (ffn2_w_up, ffn2_w_down, w_br_s5, w_br_attn, w_br_conv, w_out, s5_w_glu)
        u, q, k, v, hc, f2_up, f2_down, w_s5, w_at, w_cv, w_o, w_glu = _inproj(
            xt, row(mix_norm[l]), w_in_l, qg2, kg2, [(w, l) for w in own])

        pt, q_mat, a_r, a_i, d_row = _s5_tables(
            s5_lambda_re[l], s5_lambda_im[l], s5_log_dt[l], s5_b_re[l], s5_b_im[l],
            s5_c_re[l], s5_c_im[l], s5_d[l])
        s5o = _s5(u.reshape(nb, seq, D_S5), pt, q_mat, a_r, a_i, d_row, w_glu)

        ato = _attn(q.reshape(nb, seq, D_ATT), k.reshape(nb, seq, D_ATT),
                    v.reshape(nb, seq, D_ATT), _attn_bias(attn_rel_bias[l]))

        nh = D_CONV // LANES
        w_dw = conv_w_dw[l].astype(F32).reshape(CONV_WIDTH, nh, LANES).transpose(1, 0, 2)
        nxt = (ffn1_w_up, ffn1_w_down, w_in) if l + 1 < depth else ()
        xt, *nxt_bf16 = _merge(
            xt, s5o.reshape(n, D_S5), ato.reshape(n, D_ATT), hc, row(mix_norm[l]), w_in_l,
            row(b_gate[l]), w_s5, w_at, w_cv, w_o, w_dw,
            conv_b_dw[l].astype(F32).reshape(nh, 1, LANES), row(conv_ln_g[l]), row(conv_ln_b[l]),
            seq, [(w, l + 1) for w in nxt])
        if nxt_bf16:
            f1_up, f1_down, w_in_l = nxt_bf16

        xt = _ffn(xt, row(ffn2_norm[l]), f2_up, f2_down)
    return xt.reshape(nb, seq, d)
```

```python
import functools
import math

import jax
import jax.numpy as jnp
from jax import lax
from jax.experimental import pallas as pl
from jax.experimental.pallas import tpu as pltpu

F32 = jnp.float32
BF16 = jnp.bfloat16

D_MODEL = 1024
D_S5 = 256
S5_GROUP = 16
S5_GROUPS = 16
S5_STATE = 64
D_ATT = 512
HEAD_DIM = 64
HEADS = 8
CHUNK = 64
N_LEFT = 8
MAX_REL = 128
D_CONV = 256
CONV_WIDTH = 31
D_FF = 2816
EPS = 1e-6
LOG2E = math.log2(math.e)
N_SMALL = D_S5 + 3 * D_ATT + 2 * D_CONV

LANES = 128
MXU_DIM = 256
VMEM_LIMIT = 56 * 1024 * 1024

TM = 512
TM_WIDE = 1024
CAST_MIN_ROWS = 16
CAST_ROWS = 64
S5_L = 8
S5_CB = LANES // 2
S5_GB = S5_CB // S5_GROUP
S5_NCB = D_S5 // S5_CB
S5_BROW = S5_L * S5_CB
S5_BSTATE = 2 * S5_GB * S5_STATE
S5_TOK = 256
ATT_QB = 4 * CHUNK
ATT_W = ATT_QB + N_LEFT * CHUNK
ATT_KT = MXU_DIM
CONV_TR = 128
CONV_PAD = 32


def _resident(shape):
    nd = len(shape)
    return pl.BlockSpec(shape, lambda *_: (0,) * nd, pipeline_mode=pl.Buffered(1))


def _rmsnorm_bf16(x, g):
    ms = jnp.mean(x * x, axis=-1, keepdims=True)
    return ((x * lax.rsqrt(ms + EPS)) * g).astype(BF16)


def _sigmoid(x):
    return jax.nn.sigmoid(x)


class _Casts:
    def __init__(self, items, steps):
        self.in_specs, self.out_specs, self.out_shape, self.args, self.nblks = [], [], [], [], []
        for stacked, layer in items:
            _, rows, cols = stacked.shape
            every_step = rows % CAST_MIN_ROWS == 0 and rows // CAST_MIN_ROWS <= steps
            rb = CAST_MIN_ROWS if every_step else CAST_ROWS
            nblk = rows // rb
            assert nblk * rb == rows and nblk <= steps
            self.nblks.append(None if every_step else nblk)
            self.in_specs.append(pl.BlockSpec(
                (None, rb, cols),
                lambda i, layer=layer, nblk=nblk: (layer, jnp.minimum(i, nblk - 1), 0)))
            self.out_specs.append(pl.BlockSpec(
                (rb, cols), lambda i, nblk=nblk: (jnp.minimum(i, nblk - 1), 0)))
            self.out_shape.append(jax.ShapeDtypeStruct((rows, cols), BF16))
            self.args.append(stacked)

    @staticmethod
    def run_predicated(step, nblks, src_refs, dst_refs):
        for nblk in sorted({n for n in nblks if n is not None}):
            @pl.when(step < nblk)
            def _():
                for n, src, dst in zip(nblks, src_refs, dst_refs):
                    if n == nblk:
                        dst[...] = src[...].astype(BF16)

    @staticmethod
    def run_every_step(nblks, src_refs, dst_refs):
        for n, src, dst in zip(nblks, src_refs, dst_refs):
            if n is None:
                dst[...] = src[...].astype(BF16)


def _ffn_kernel(x_ref, g_ref, wa_ref, wb_ref, wd_ref, o_ref, *, bounds):
    x = x_ref[...]
    h = _rmsnorm_bf16(x, g_ref[...])
    y = jnp.zeros_like(x)
    for f0, f1 in zip(bounds[:-1], bounds[1:]):
        a = jnp.dot(h, wa_ref[:, f0:f1], preferred_element_type=F32)
        b = jnp.dot(h, wb_ref[:, f0:f1], preferred_element_type=F32)
        act = ((a * _sigmoid(a)) * b).astype(BF16)
        y = y + jnp.dot(act, wd_ref[f0:f1, :], preferred_element_type=F32)
    o_ref[...] = x + 0.5 * y


def _ffn_bounds():
    tiles = D_FF // MXU_DIM
    assert tiles * MXU_DIM == D_FF
    return (0, (tiles + 1) // 2 * MXU_DIM, D_FF)


def _ffn(x, g, w_up, w_down):
    n = x.shape[0]
    tok = pl.BlockSpec((TM_WIDE, D_MODEL), lambda i: (i, 0))
    def up_half(j):
        return pl.BlockSpec((D_MODEL, D_FF), lambda i: (0, j), pipeline_mode=pl.Buffered(1))
    return pl.pallas_call(
        functools.partial(_ffn_kernel, bounds=_ffn_bounds()),
        out_shape=jax.ShapeDtypeStruct((n, D_MODEL), F32),
        grid=(n // TM_WIDE,),
        in_specs=[tok, _resident((1, D_MODEL)), up_half(0), up_half(1), _resident((D_FF, D_MODEL))],
        out_specs=tok,
        compiler_params=pltpu.CompilerParams(
            dimension_semantics=("arbitrary",), vmem_limit_bytes=VMEM_LIMIT),
        name="ffn",
    )(x, g, w_up, w_up, w_down)


def _head_pair_norm(x2, gain2):
    lane = lax.broadcasted_iota(jnp.int32, (1, LANES), 1)
    first = lane < HEAD_DIM
    sq = x2 * x2
    s_a = jnp.sum(jnp.where(first, sq, 0.0), axis=-1, keepdims=True)
    s_b = jnp.sum(jnp.where(first, 0.0, sq), axis=-1, keepdims=True)
    ms = jnp.where(first, s_a, s_b) * (1.0 / HEAD_DIM)
    return (x2 * lax.rsqrt(ms + EPS)) * gain2


def _inproj_kernel(x_ref, g_ref, w_ref, qg_ref, kg_ref, *refs, cast_nblks):
    n_cast = len(cast_nblks)
    u_ref, q_ref, k_ref, v_ref, hc_ref = refs[n_cast:n_cast + 5]
    _Casts.run_predicated(pl.program_id(0), cast_nblks, refs[:n_cast], refs[n_cast + 5:])
    _Casts.run_every_step(cast_nblks, refs[:n_cast], refs[n_cast + 5:])
    h = _rmsnorm_bf16(x_ref[...], g_ref[...])
    proj = jnp.dot(h, w_ref[...], preferred_element_type=F32)
    u_ref[...] = proj[:, :D_S5]
    q0, k0, v0, z0 = D_S5, D_S5 + D_ATT, D_S5 + 2 * D_ATT, D_S5 + 3 * D_ATT
    scale = HEAD_DIM ** -0.5 * LOG2E
    for p in range(D_ATT // LANES):
        lo = p * LANES
        qn = _head_pair_norm(proj[:, q0 + lo:q0 + lo + LANES], qg_ref[...])
        kn = _head_pair_norm(proj[:, k0 + lo:k0 + lo + LANES], kg_ref[...])
        q_ref[:, lo:lo + LANES] = (qn * scale).astype(BF16)
        k_ref[:, lo:lo + LANES] = kn.astype(BF16)
    v_ref[...] = proj[:, v0:v0 + D_ATT].astype(BF16)
    a = proj[:, z0:z0 + D_CONV]
    gt = proj[:, z0 + D_CONV:z0 + 2 * D_CONV]
    hc_ref[...] = a * _sigmoid(gt)


def _inproj(x, g, w_in, qg2, kg2, cast_items):
    n = x.shape[0]
    steps = n // TM_WIDE
    casts = _Casts(cast_items, steps)
    def tok(width):
        return pl.BlockSpec((TM_WIDE, width), lambda i: (i, 0))
    w_small = pl.BlockSpec((D_MODEL, N_SMALL), lambda i: (0, 0), pipeline_mode=pl.Buffered(1))
    return pl.pallas_call(
        functools.partial(_inproj_kernel, cast_nblks=tuple(casts.nblks)),
        out_shape=(jax.ShapeDtypeStruct((n, D_S5), F32),
                   jax.ShapeDtypeStruct((n, D_ATT), BF16),
                   jax.ShapeDtypeStruct((n, D_ATT), BF16),
                   jax.ShapeDtypeStruct((n, D_ATT), BF16),
                   jax.ShapeDtypeStruct((n, D_CONV), F32), *casts.out_shape),
        grid=(steps,),
        in_specs=[tok(D_MODEL), _resident((1, D_MODEL)), w_small,
                  _resident((1, LANES)), _resident((1, LANES)), *casts.in_specs],
        out_specs=(tok(D_S5), tok(D_ATT), tok(D_ATT), tok(D_ATT), tok(D_CONV), *casts.out_specs),
        compiler_params=pltpu.CompilerParams(
            dimension_semantics=("arbitrary",), vmem_limit_bytes=VMEM_LIMIT),
        name="inproj",
    )(x, g, w_in, qg2, kg2, *casts.args)


def _s5_tables(lambda_re, lambda_im, log_dt, b_re, b_im, c_re, c_im, d_skip):
    lr = jnp.minimum(lambda_re.astype(F32), -1e-4)
    li = lambda_im.astype(F32)
    dt = jnp.exp(log_dt.astype(F32))[:, None]
    mag = jnp.exp(lr * dt)
    ar = mag * jnp.cos(li * dt)
    ai = mag * jnp.sin(li * dt)
    den = lr * lr + li * li
    coef_r = ((ar - 1.0) * lr + ai * li) / den
    coef_i = (ai * lr - (ar - 1.0) * li) / den
    br = b_re.astype(F32)
    bi = b_im.astype(F32)
    bbar_r = coef_r[..., None] * br - coef_i[..., None] * bi
    bbar_i = coef_r[..., None] * bi + coef_i[..., None] * br
    cr = c_re.astype(F32)
    ci = c_im.astype(F32)

    pr, pi = [jnp.ones_like(ar)], [jnp.zeros_like(ai)]
    for _ in range(S5_L):
        r, i = pr[-1], pi[-1]
        pr.append(r * ar - i * ai)
        pi.append(r * ai + i * ar)
    pr = jnp.stack(pr)
    pi = jnp.stack(pi)

    ncb, gb = S5_NCB, S5_GB
    pw_r = pr[S5_L - 1::-1]
    pw_i = pi[S5_L - 1::-1]
    e_r = pw_r[..., None] * bbar_r[None] - pw_i[..., None] * bbar_i[None]
    e_i = pw_r[..., None] * bbar_i[None] + pw_i[..., None] * bbar_r[None]
    p_blk = jnp.stack([jnp.transpose(e_r, (0, 1, 3, 2)), jnp.transpose(e_i, (0, 1, 3, 2))], axis=3)
    p_blk = p_blk.reshape(S5_L, ncb, gb * S5_GROUP, 2 * S5_STATE)
    p_blk = jnp.transpose(p_blk, (1, 0, 2, 3)).reshape(ncb, S5_BROW, 2 * S5_STATE)

    m_r = pr[:S5_L, :, None, :] * cr[None] - pi[:S5_L, :, None, :] * ci[None]
    m_i = pr[:S5_L, :, None, :] * ci[None] + pi[:S5_L, :, None, :] * cr[None]
    kern = (jnp.sum(m_r[:, :, :, :, None] * bbar_r[None, :, None, :, :], axis=3)
            - jnp.sum(m_i[:, :, :, :, None] * bbar_i[None, :, None, :, :], axis=3))
    t_blk = jnp.transpose(kern, (0, 1, 3, 2)).reshape(S5_L, ncb, S5_CB, S5_GROUP)
    t_blk = jnp.transpose(t_blk, (1, 0, 2, 3))

    q_r = pr[1:, :, None, :] * cr[None] - pi[1:, :, None, :] * ci[None]
    q_i = -(pr[1:, :, None, :] * ci[None] + pi[1:, :, None, :] * cr[None])
    q_blk = jnp.stack([jnp.transpose(q_r, (0, 1, 3, 2)), jnp.transpose(q_i, (0, 1, 3, 2))], axis=1)
    q_blk = q_blk.reshape(S5_L, 2, ncb, gb * S5_STATE, S5_GROUP)
    q_blk = jnp.transpose(q_blk, (2, 0, 1, 3, 4)).reshape(ncb, S5_L, S5_BSTATE, S5_GROUP)

    half_blocks = S5_BSTATE // 2 // LANES
    a_r = pr[S5_L].reshape(ncb, half_blocks, 1, LANES)
    a_i = pi[S5_L].reshape(ncb, half_blocks, 1, LANES)
    d_row = jnp.tile(d_skip.astype(F32).reshape(ncb, 1, S5_CB), (1, 1, S5_L))
    pt, q_mat = _s5_expand(p_blk.astype(BF16), t_blk.astype(BF16), q_blk.astype(BF16))
    return pt, q_mat, a_r, a_i, d_row


def _s5_expand_kernel(p_ref, t_ref, q_ref, pt_ref, qm_ref):
    def iota(shape, dim):
        return lax.broadcasted_iota(jnp.int32, shape, dim)

    def onehot(rows, cols, row_key, col_key):
        r, c = iota((rows, cols), 0), iota((rows, cols), 1)
        return jnp.where(row_key(r) == col_key(c), 1.0, 0.0)

    half = S5_BSTATE // 2
    rep = onehot(2 * S5_STATE, S5_BSTATE, lambda r: r,
                 lambda c: (c // half) * S5_STATE + c % S5_STATE).astype(BF16)
    same = onehot(S5_CB, S5_BSTATE, lambda r: r // S5_GROUP, lambda c: (c % half) // S5_STATE)
    rep_c = onehot(S5_GROUP, S5_CB, lambda r: r, lambda c: c % S5_GROUP).astype(BF16)
    same_g = onehot(S5_CB, S5_CB, lambda r: r // S5_GROUP, lambda c: c // S5_GROUP)
    same_q = onehot(S5_BSTATE, S5_CB, lambda r: (r % half) // S5_STATE, lambda c: c // S5_GROUP)
    lag_blk = [(jnp.dot(t_ref[d], rep_c, preferred_element_type=F32) * same_g).astype(BF16)
               for d in range(S5_L)]
    zero_blk = jnp.zeros((S5_CB, S5_CB), BF16)
    for s in range(S5_L):
        rows = slice(s * S5_CB, (s + 1) * S5_CB)
        p_full = jnp.dot(p_ref[rows, :], rep, preferred_element_type=F32) * same
        pt_ref[rows, :S5_BSTATE] = p_full.astype(BF16)
        for t in range(S5_L):
            cols = slice(S5_BSTATE + t * S5_CB, S5_BSTATE + (t + 1) * S5_CB)
            pt_ref[rows, cols] = lag_blk[t - s] if t >= s else zero_blk
    for t in range(S5_L):
        q_full = jnp.dot(q_ref[t], rep_c, preferred_element_type=F32) * same_q
        qm_ref[:, t * S5_CB:(t + 1) * S5_CB] = q_full.astype(BF16)


def _s5_expand(p_blk, t_blk, q_blk):
    def blk(shape):
        return pl.BlockSpec((None,) + shape, lambda cb: (cb,) + (0,) * len(shape))
    return pl.pallas_call(
        _s5_expand_kernel,
        out_shape=(jax.ShapeDtypeStruct((S5_NCB, S5_BROW, S5_BSTATE + S5_BROW), BF16),
                   jax.ShapeDtypeStruct((S5_NCB, S5_BSTATE, S5_BROW), BF16)),
        grid=(S5_NCB,),
        in_specs=[blk((S5_BROW, 2 * S5_STATE)), blk((S5_L, S5_CB, S5_GROUP)),
                  blk((S5_L, S5_BSTATE, S5_GROUP))],
        out_specs=(blk((S5_BROW, S5_BSTATE + S5_BROW)), blk((S5_BSTATE, S5_BROW))),
        compiler_params=pltpu.CompilerParams(
            dimension_semantics=("arbitrary",), vmem_limit_bytes=VMEM_LIMIT),
        name="s5_tables",
    )(p_blk, t_blk, q_blk)


def _s5_kernel(ua_ref, ub_ref, pt_ref, q_ref, ar_ref, ai_ref, d_ref, wg_ref, o_ref,
               sr_ref, si_ref, u8_ref, es_ref, ot_ref):
    nb = ua_ref.shape[0]
    ncl = S5_TOK // S5_L
    nlb = S5_BSTATE // LANES
    hlb = nlb // 2
    u_refs = (ua_ref, ub_ref)
    low = lax.broadcasted_iota(jnp.int32, (1, LANES), 1) < S5_CB

    def interleave(a, b):
        return (jnp.where(low, a, pltpu.roll(b, S5_CB, axis=1)),
                jnp.where(low, pltpu.roll(a, S5_CB, axis=1), b))

    @pl.when(pl.program_id(0) == 0)
    def _():
        sr_ref[...] = jnp.zeros_like(sr_ref)
        si_ref[...] = jnp.zeros_like(si_ref)

    sub = 8
    ncg = ncl // sub
    grp = nb * sub
    for hh, u_ref in enumerate(u_refs):
        for t2 in range(S5_L // 2):
            for cg in range(ncg):
                tok = [u_ref[:, pl.ds(cg * sub * S5_L + 2 * t2 + k, sub, stride=S5_L), :]
                       .reshape(grp, LANES) for k in range(2)]
                for k, blk in enumerate(interleave(*tok)):
                    u8_ref[2 * hh + k, cg * grp:(cg + 1) * grp, t2 * LANES:(t2 + 1) * LANES] = blk

    for cb in range(S5_NCB):
        e = jnp.dot(u8_ref[cb].astype(BF16), pt_ref[cb, :, :S5_BSTATE],
                    preferred_element_type=F32)
        for j in range(nlb):
            es_ref[cb, j] = e[:, j * LANES:(j + 1) * LANES]

    a_r = ar_ref[...]
    a_i = ai_ref[...]
    sr = sr_ref[...]
    si = si_ref[...]
    for cl in range(ncl):
        step = pl.ds((cl // sub) * grp + cl % sub, nb, stride=sub)
        e_r = es_ref[:, :hlb, step, :]
        e_i = es_ref[:, hlb:, step, :]
        es_ref[:, :hlb, step, :] = sr
        es_ref[:, hlb:, step, :] = si
        sr, si = a_r * sr - a_i * si + e_r, a_r * si + a_i * sr + e_i
    sr_ref[...] = sr
    si_ref[...] = si

    ys = []
    for cb in range(S5_NCB):
        u8 = u8_ref[cb]
        s_all = jnp.concatenate([es_ref[cb, j] for j in range(nlb)], axis=1).astype(BF16)
        y = (jnp.dot(u8.astype(BF16), pt_ref[cb, :, S5_BSTATE:], preferred_element_type=F32)
             + jnp.dot(s_all, q_ref[cb], preferred_element_type=F32) + d_ref[cb] * u8)
        ys.append(jax.nn.gelu(y))
    for t2 in range(S5_L // 2):
        lanes = slice(t2 * LANES, (t2 + 1) * LANES)
        halves = [interleave(ys[2 * hh][:, lanes], ys[2 * hh + 1][:, lanes]) for hh in range(2)]
        for k in range(2):
            t = 2 * t2 + k
            yt = jnp.concatenate([halves[0][k], halves[1][k]], axis=1).astype(BF16)
            ag = jnp.dot(yt, wg_ref[...], preferred_element_type=F32)
            out = ag[:, :D_S5] * _sigmoid(ag[:, D_S5:])
            for hh in range(D_S5 // LANES):
                for cg in range(ncg):
                    ot_ref[hh, :, pl.ds(cg * sub * S5_L + t, sub, stride=S5_L), :] = (
                        out[cg * grp:(cg + 1) * grp, hh * LANES:(hh + 1) * LANES].reshape(nb, sub, LANES))
    o_ref[...] = jnp.concatenate([ot_ref[0], ot_ref[1]], axis=-1).astype(BF16)


def _s5(u, pt, q_mat, a_r, a_i, d_row, w_glu):
    nb, seq, _ = u.shape
    rows = nb * (S5_TOK // S5_L)
    nlb = S5_BSTATE // LANES
    assert 2 * S5_CB == LANES and D_S5 == 2 * LANES
    return pl.pallas_call(
        _s5_kernel,
        out_shape=jax.ShapeDtypeStruct((nb, seq, D_S5), BF16),
        grid=(seq // S5_TOK,),
        in_specs=[pl.BlockSpec((nb, S5_TOK, LANES), lambda i: (0, i, 0)),
                  pl.BlockSpec((nb, S5_TOK, LANES), lambda i: (0, i, 1)),
                  _resident(pt.shape), _resident(q_mat.shape), _resident(a_r.shape),
                  _resident(a_i.shape), _resident(d_row.shape), _resident(w_glu.shape)],
        out_specs=pl.BlockSpec((nb, S5_TOK, D_S5), lambda i: (0, i, 0)),
        scratch_shapes=[pltpu.VMEM((S5_NCB, nlb // 2, nb, LANES), F32),
                        pltpu.VMEM((S5_NCB, nlb // 2, nb, LANES), F32),
                        pltpu.VMEM((S5_NCB, rows, S5_BROW), F32),
                        pltpu.VMEM((S5_NCB, nlb, rows, LANES), F32),
                        pltpu.VMEM((D_S5 // LANES, nb, S5_TOK, LANES), F32)],
        compiler_params=pltpu.CompilerParams(
            dimension_semantics=("arbitrary",), vmem_limit_bytes=VMEM_LIMIT),
        name="s5",
    )(u, u, pt, q_mat, a_r, a_i, d_row, w_glu)


def _attn_bias(rel_bias):
    rb = rel_bias.astype(F32) * LOG2E
    far_past, far_future = rb[:, 2 * MAX_REL:], rb[:, :1]
    period = ATT_QB + ATT_W
    n_const = N_LEFT * CHUNK - MAX_REL
    vec = jnp.concatenate([
        jnp.broadcast_to(far_past, (HEADS, n_const)),
        rb[:, ::-1],
        jnp.broadcast_to(far_future, (HEADS, ATT_W - n_const - 2 * MAX_REL - 1)),
        jnp.broadcast_to(far_past, (HEADS, ATT_QB)),
    ], axis=1)
    toep = jnp.tile(vec, (1, ATT_QB))[:, :ATT_QB * (period - 1)]
    toep = toep.reshape(HEADS, ATT_QB, period - 1)[:, :, :ATT_W]
    r = jnp.arange(ATT_QB)[:, None]
    c = jnp.arange(ATT_W)[None, :] - N_LEFT * CHUNK
    dchunk = r // CHUNK - jnp.floor_divide(c, CHUNK)
    ok = (dchunk >= 0) & (dchunk <= N_LEFT)
    bias = jnp.where(ok[None], toep, -1e30)
    return bias.reshape(HEADS // 2, 2, ATT_QB, ATT_W)


def _attn_kernel(q_ref, k_ref, v_ref, bm_ref, o_ref):
    seq = q_ref.shape[1]
    lane = lax.broadcasted_iota(jnp.int32, (1, LANES), 1)
    first = lane < HEAD_DIM

    def block(q0, k0, width):
        for pr in range(D_ATT // LANES):
            cols = slice(pr * LANES, (pr + 1) * LANES)
            q2 = q_ref[0, pl.ds(q0, ATT_QB), cols]
            outs = []
            for hh in range(2):
                sel = first if hh == 0 else jnp.logical_not(first)
                qh = jnp.where(sel, q2, jnp.zeros_like(q2))
                for c0 in range(0, width, ATT_KT):
                    kw = k_ref[0, pl.ds(k0 + c0, ATT_KT), cols]
                    vw = v_ref[0, pl.ds(k0 + c0, ATT_KT), cols]
                    s = lax.dot_general(qh, kw, (((1,), (1,)), ((), ())),
                                        preferred_element_type=F32)
                    b0 = ATT_W - width + c0
                    s = s + bm_ref[pr, hh, :, b0:b0 + ATT_KT]
                    vh = jnp.where(sel, vw, jnp.ones_like(vw))
                    if c0 == 0:
                        mx = jnp.max(s, axis=-1, keepdims=True)
                        acc = jnp.dot(jnp.exp2(s - mx).astype(BF16), vh, preferred_element_type=F32)
                    else:
                        mx_new = jnp.maximum(mx, jnp.max(s, axis=-1, keepdims=True))
                        p = jnp.exp2(s - mx_new).astype(BF16)
                        acc = acc * jnp.exp2(mx - mx_new) + jnp.dot(p, vh, preferred_element_type=F32)
                        mx = mx_new
                outs.append(acc)
            num = jnp.where(first, outs[0], outs[1])
            den = pltpu.roll(jnp.where(first, outs[1], outs[0]), HEAD_DIM, axis=1)
            o_ref[0, pl.ds(q0, ATT_QB), cols] = (num / den).astype(BF16)

    n_short = N_LEFT * CHUNK // ATT_QB
    for qb in range(n_short):
        block(qb * ATT_QB, 0, (qb + 1) * ATT_QB)

    def body(qb, carry):
        q0 = pl.multiple_of(qb * ATT_QB, ATT_QB)
        k0 = pl.multiple_of(q0 - N_LEFT * CHUNK, ATT_QB)
        block(q0, k0, ATT_W)
        return carry
    lax.fori_loop(n_short, seq // ATT_QB, body, 0)


def _attn(q, k, v, bm):
    nb, seq, _ = q.shape
    seqblk = pl.BlockSpec((1, seq, D_ATT), lambda b: (b, 0, 0))
    return pl.pallas_call(
        _attn_kernel,
        out_shape=jax.ShapeDtypeStruct((nb, seq, D_ATT), BF16),
        grid=(nb,),
        in_specs=[seqblk, seqblk, seqblk, _resident(bm.shape)],
        out_specs=seqblk,
        compiler_params=pltpu.CompilerParams(
            dimension_semantics=("arbitrary",), vmem_limit_bytes=VMEM_LIMIT),
        name="attn",
    )(q, k, v, bm)


def _conv_taps(hc_ref, cw_ref, cb_ref, cbuf_ref, first_tile):
    nh = D_CONV // LANES
    tm = hc_ref.shape[0]

    @pl.when(first_tile)
    def _():
        cbuf_ref[:, :CONV_PAD, :] = jnp.zeros((nh, CONV_PAD, LANES), F32)
    for hh in range(nh):
        cbuf_ref[hh, CONV_PAD:, :] = hc_ref[:, hh * LANES:(hh + 1) * LANES]
    shift = CONV_PAD - (CONV_WIDTH - 1)
    halves = []
    for hh in range(nh):
        tiles = []
        for r0 in range(0, tm, CONV_TR):
            acc = jnp.zeros((CONV_TR, LANES), F32) + cb_ref[hh]
            for j in range(CONV_WIDTH):
                lo = r0 + shift + j
                acc = acc + cw_ref[hh, j:j + 1, :] * cbuf_ref[hh, lo:lo + CONV_TR, :]
            tiles.append(acc)
        halves.append(jnp.concatenate(tiles, axis=0))
    for hh in range(nh):
        cbuf_ref[hh, :CONV_PAD, :] = cbuf_ref[hh, tm:tm + CONV_PAD, :]
    return halves


def _conv_norm(halves, lg_ref, lb_ref):
    acc = jnp.concatenate(halves, axis=-1)
    mu = jnp.mean(acc, axis=-1, keepdims=True)
    cen = acc - mu
    var = jnp.mean(cen * cen, axis=-1, keepdims=True)
    y = (cen * lax.rsqrt(var + EPS)) * lg_ref[...] + lb_ref[...]
    return (y * _sigmoid(y)).astype(BF16)


def _after(v, zero_ref, width):
    bits = pltpu.bitcast(v, jnp.uint32)
    acc = bits[0:8, :]
    for r0 in range(8, v.shape[0], 8):
        acc = acc | bits[r0:r0 + 8, :]
    zero = pltpu.bitcast(acc[0:1, :] & zero_ref[...], F32)
    return jnp.tile(zero, (1, width // LANES)).astype(BF16)


def _merge_kernel(x_ref, s5_ref, at_ref, hc_ref, g_ref, wg_ref, bg_ref, ws_ref, wa_ref, wc_ref, wo_ref,
                  cw_ref, cb_ref, lg_ref, lb_ref, zero_ref, *refs, tiles_per_seq, cast_nblks):
    n_cast = len(cast_nblks)
    o_ref, cbuf_ref = refs[n_cast], refs[-1]
    _Casts.run_predicated(pl.program_id(0), cast_nblks, refs[:n_cast], refs[n_cast + 1:-1])
    halves = _conv_taps(hc_ref, cw_ref, cb_ref, cbuf_ref, pl.program_id(0) % tiles_per_seq == 0)
    _Casts.run_every_step(cast_nblks, refs[:n_cast], refs[n_cast + 1:-1])
    x = x_ref[...]
    h = _rmsnorm_bf16(x, g_ref[...])
    lhs = (h, h + _after(halves[0], zero_ref, D_MODEL), h + _after(halves[1], zero_ref, D_MODEL))
    branches = (lambda: s5_ref[...], lambda: at_ref[...], lambda: _conv_norm(halves, lg_ref, lb_ref))
    merged = jnp.zeros_like(x)
    for i, (branch, w_ref) in enumerate(zip(branches, (ws_ref, wa_ref, wc_ref))):
        cols = slice(i * D_MODEL, (i + 1) * D_MODEL)
        gcols = slice(N_SMALL + i * D_MODEL, N_SMALL + (i + 1) * D_MODEL)
        logits = jnp.dot(lhs[i], wg_ref[:, gcols], preferred_element_type=F32) + bg_ref[:, cols]
        y = jnp.dot(branch(), w_ref[...], preferred_element_type=F32)
        merged = merged + _sigmoid(logits) * y
    o_ref[...] = x + jnp.dot(merged.astype(BF16), wo_ref[...], preferred_element_type=F32)


def _merge(x, s5o, ato, hc, g, w_in, b_gate, w_s5, w_at, w_cv, w_out, w_dw, b_dw, ln_g, ln_b, seq,
           cast_items):
    n = x.shape[0]
    nh = D_CONV // LANES
    zero = jnp.zeros((1, LANES), jnp.uint32)
    assert seq % TM == 0
    steps = n // TM
    casts = _Casts(cast_items, steps)
    def tok(width):
        return pl.BlockSpec((TM, width), lambda i: (i, 0))
    return pl.pallas_call(
        functools.partial(_merge_kernel, tiles_per_seq=seq // TM, cast_nblks=tuple(casts.nblks)),
        out_shape=(jax.ShapeDtypeStruct((n, D_MODEL), F32), *casts.out_shape),
        grid=(steps,),
        in_specs=[tok(D_MODEL), tok(D_S5), tok(D_ATT), tok(D_CONV),
                  _resident((1, D_MODEL)), _resident(w_in.shape),
                  _resident((1, 3 * D_MODEL)), _resident((D_S5, D_MODEL)),
                  _resident((D_ATT, D_MODEL)), _resident((D_CONV, D_MODEL)),
                  _resident((D_MODEL, D_MODEL)), _resident((nh, CONV_WIDTH, LANES)),
                  _resident((nh, 1, LANES)), _resident((1, D_CONV)), _resident((1, D_CONV)),
                  _resident((1, LANES)), *casts.in_specs],
        out_specs=(tok(D_MODEL), *casts.out_specs),
        scratch_shapes=[pltpu.VMEM((nh, CONV_PAD + TM, LANES), F32)],
        compiler_params=pltpu.CompilerParams(
            dimension_semantics=("arbitrary",), vmem_limit_bytes=VMEM_LIMIT),
        name="merge",
    )(x, s5o, ato, hc, g, w_in, b_gate, w_s5, w_at, w_cv, w_out, w_dw, b_dw, ln_g, ln_b, zero,
      *casts.args)


def kernel(x, ffn1_norm, ffn1_w_up, ffn1_w_down, mix_norm, w_in, b_gate, s5_lambda_re, s5_lambda_im, s5_log_dt, s5_b_re, s5_b_im, s5_c_re, s5_c_im, s5_d, s5_w_glu, w_br_s5, attn_q_gain, attn_k_gain, attn_rel_bias, w_br_attn, conv_w_dw, conv_b_dw, conv_ln_g, conv_ln_b, w_br_conv, w_out, ffn2_norm, ffn2_w_up, ffn2_w_down):
    nb, seq, d = x.shape
    n = nb * seq
    depth = ffn1_norm.shape[0]
    xt = x.reshape(n, d)
    row = lambda v: v.reshape(1, -1).astype(F32)
    f1_up, f1_down, w_in_l = (ffn1_w_up[0].astype(BF16), ffn1_w_down[0].astype(BF16),
                              w_in[0].astype(BF16))
    for l in range(depth):
        xt = _ffn(xt, row(ffn1_norm[l]), f1_up, f1_down)

        qg2 = jnp.tile(row(attn_q_gain[l]), (1, LANES // HEAD_DIM))
        kg2 = jnp.tile(row(attn_k_gain[l]), (1, LANES // HEAD_DIM))
        own = (ffn2_w_up, ffn2_w_down, w_br_s5, w_br_attn, w_br_conv, w_out, s5_w_glu)
        u, q, k, v, hc, f2_up, f2_down, w_s5, w_at, w_cv, w_o, w_glu = _inproj(
            xt, row(mix_norm[l]), w_in_l, qg2, kg2, [(w, l) for w in own])

        pt, q_mat, a_r, a_i, d_row = _s5_tables(
            s5_lambda_re[l], s5_lambda_im[l], s5_log_dt[l], s5_b_re[l], s5_b_im[l],
            s5_c_re[l], s5_c_im[l], s5_d[l])
        s5o = _s5(u.reshape(nb, seq, D_S5), pt, q_mat, a_r, a_i, d_row, w_glu)

        ato = _attn(q.reshape(nb, seq, D_ATT), k.reshape(nb, seq, D_ATT),
                    v.reshape(nb, seq, D_ATT), _attn_bias(attn_rel_bias[l]))

        nh = D_CONV // LANES
        w_dw = conv_w_dw[l].astype(F32).reshape(CONV_WIDTH, nh, LANES).transpose(1, 0, 2)
        nxt = (ffn1_w_up, ffn1_w_down, w_in) if l + 1 < depth else ()
        xt, *nxt_bf16 = _merge(
            xt, s5o.reshape(n, D_S5), ato.reshape(n, D_ATT), hc, row(mix_norm[l]), w_in_l,
            row(b_gate[l]), w_s5, w_at, w_cv, w_o, w_dw,
            conv_b_dw[l].astype(F32).reshape(nh, 1, LANES), row(conv_ln_g[l]), row(conv_ln_b[l]),
            seq, [(w, l + 1) for w in nxt])
        if nxt_bf16:
            f1_up, f1_down, w_in_l = nxt_bf16

        xt = _ffn(xt, row(ffn2_norm[l]), f2_up, f2_down)
    return xt.reshape(nb, seq, d)
```

```python
import functools
import math

import jax
import jax.numpy as jnp
from jax import lax
from jax.experimental import pallas as pl
from jax.experimental.pallas import tpu as pltpu

F32 = jnp.float32
BF16 = jnp.bfloat16

D_MODEL = 1024
D_S5 = 256
S5_GROUP = 16
S5_GROUPS = 16
S5_STATE = 64
D_ATT = 512
HEAD_DIM = 64
HEADS = 8
CHUNK = 64
N_LEFT = 8
MAX_REL = 128
D_CONV = 256
CONV_WIDTH = 31
D_FF = 2816
EPS = 1e-6
LOG2E = math.log2(math.e)
N_SMALL = D_S5 + 3 * D_ATT + 2 * D_CONV

LANES = 128
MXU_DIM = 256
VMEM_LIMIT = 56 * 1024 * 1024

TM = 512
TM_WIDE = 1024
CAST_MIN_ROWS = 16
CAST_ROWS = 64
S5_L = 8
S5_CB = LANES // 2
S5_GB = S5_CB // S5_GROUP
S5_NCB = D_S5 // S5_CB
S5_BROW = S5_L * S5_CB
S5_BSTATE = 2 * S5_GB * S5_STATE
S5_TOK = 256
ATT_QB = 4 * CHUNK
ATT_W = ATT_QB + N_LEFT * CHUNK
ATT_KT = MXU_DIM
CONV_TR = 128
CONV_PAD = 32


def _resident(shape):
    nd = len(shape)
    return pl.BlockSpec(shape, lambda *_: (0,) * nd, pipeline_mode=pl.Buffered(1))


def _rmsnorm_bf16(x, g):
    ms = jnp.mean(x * x, axis=-1, keepdims=True)
    return ((x * lax.rsqrt(ms + EPS)) * g).astype(BF16)


def _sigmoid(x):
    return jax.nn.sigmoid(x)


class _Casts:
    def __init__(self, items, steps):
        self.in_specs, self.out_specs, self.out_shape, self.args, self.nblks = [], [], [], [], []
        for stacked, layer in items:
            _, rows, cols = stacked.shape
            every_step = rows % CAST_MIN_ROWS == 0 and rows // CAST_MIN_ROWS <= steps
            rb = CAST_MIN_ROWS if every_step else CAST_ROWS
            nblk = rows // rb
            assert nblk * rb == rows and nblk <= steps
            self.nblks.append(None if every_step else nblk)
            self.in_specs.append(pl.BlockSpec(
                (None, rb, cols),
                lambda i, layer=layer, nblk=nblk: (layer, jnp.minimum(i, nblk - 1), 0)))
            self.out_specs.append(pl.BlockSpec(
                (rb, cols), lambda i, nblk=nblk: (jnp.minimum(i, nblk - 1), 0)))
            self.out_shape.append(jax.ShapeDtypeStruct((rows, cols), BF16))
            self.args.append(stacked)

    @staticmethod
    def run_predicated(step, nblks, src_refs, dst_refs):
        for nblk in sorted({n for n in nblks if n is not None}):
            @pl.when(step < nblk)
            def _():
                for n, src, dst in zip(nblks, src_refs, dst_refs):
                    if n == nblk:
                        dst[...] = src[...].astype(BF16)

    @staticmethod
    def run_every_step(nblks, src_refs, dst_refs):
        for n, src, dst in zip(nblks, src_refs, dst_refs):
            if n is None:
                dst[...] = src[...].astype(BF16)


def _ffn_kernel(x_ref, g_ref, wa_ref, wb_ref, wd_ref, o_ref, *, bounds):
    x = x_ref[...]
    h = _rmsnorm_bf16(x, g_ref[...])
    y = jnp.zeros_like(x)
    for f0, f1 in zip(bounds[:-1], bounds[1:]):
        a = jnp.dot(h, wa_ref[:, f0:f1], preferred_element_type=F32)
        b = jnp.dot(h, wb_ref[:, f0:f1], preferred_element_type=F32)
        act = ((a * _sigmoid(a)) * b).astype(BF16)
        y = y + jnp.dot(act, wd_ref[f0:f1, :], preferred_element_type=F32)
    o_ref[...] = x + 0.5 * y


def _ffn_bounds():
    tiles = D_FF // MXU_DIM
    assert tiles * MXU_DIM == D_FF
    return (0, (tiles + 1) // 2 * MXU_DIM, D_FF)


def _ffn(x, g, w_up, w_down):
    n = x.shape[0]
    tok = pl.BlockSpec((TM_WIDE, D_MODEL), lambda i: (i, 0))
    def up_half(j):
        return pl.BlockSpec((D_MODEL, D_FF), lambda i: (0, j), pipeline_mode=pl.Buffered(1))
    return pl.pallas_call(
        functools.partial(_ffn_kernel, bounds=_ffn_bounds()),
        out_shape=jax.ShapeDtypeStruct((n, D_MODEL), F32),
        grid=(n // TM_WIDE,),
        in_specs=[tok, _resident((1, D_MODEL)), up_half(0), up_half(1), _resident((D_FF, D_MODEL))],
        out_specs=tok,
        compiler_params=pltpu.CompilerParams(
            dimension_semantics=("arbitrary",), vmem_limit_bytes=VMEM_LIMIT),
        name="ffn",
    )(x, g, w_up, w_up, w_down)


def _head_pair_norm(x2, gain2):
    lane = lax.broadcasted_iota(jnp.int32, (1, LANES), 1)
    first = lane < HEAD_DIM
    sq = x2 * x2
    s_a = jnp.sum(jnp.where(first, sq, 0.0), axis=-1, keepdims=True)
    s_b = jnp.sum(jnp.where(first, 0.0, sq), axis=-1, keepdims=True)
    ms = jnp.where(first, s_a, s_b) * (1.0 / HEAD_DIM)
    return (x2 * lax.rsqrt(ms + EPS)) * gain2


def _inproj_kernel(x_ref, g_ref, w_ref, qg_ref, kg_ref, *refs, cast_nblks):
    n_cast = len(cast_nblks)
    u_ref, q_ref, k_ref, v_ref, hc_ref = refs[n_cast:n_cast + 5]
    _Casts.run_predicated(pl.program_id(0), cast_nblks, refs[:n_cast], refs[n_cast + 5:])
    _Casts.run_every_step(cast_nblks, refs[:n_cast], refs[n_cast + 5:])
    h = _rmsnorm_bf16(x_ref[...], g_ref[...])
    proj = jnp.dot(h, w_ref[...], preferred_element_type=F32)
    u_ref[...] = proj[:, :D_S5]
    q0, k0, v0, z0 = D_S5, D_S5 + D_ATT, D_S5 + 2 * D_ATT, D_S5 + 3 * D_ATT
    scale = HEAD_DIM ** -0.5 * LOG2E
    for p in range(D_ATT // LANES):
        lo = p * LANES
        qn = _head_pair_norm(proj[:, q0 + lo:q0 + lo + LANES], qg_ref[...])
        kn = _head_pair_norm(proj[:, k0 + lo:k0 + lo + LANES], kg_ref[...])
        q_ref[:, lo:lo + LANES] = (qn * scale).astype(BF16)
        k_ref[:, lo:lo + LANES] = kn.astype(BF16)
    v_ref[...] = proj[:, v0:v0 + D_ATT].astype(BF16)
    a = proj[:, z0:z0 + D_CONV]
    gt = proj[:, z0 + D_CONV:z0 + 2 * D_CONV]
    hc_ref[...] = a * _sigmoid(gt)


def _inproj(x, g, w_in, qg2, kg2, cast_items):
    n = x.shape[0]
    steps = n // TM_WIDE
    casts = _Casts(cast_items, steps)
    def tok(width):
        return pl.BlockSpec((TM_WIDE, width), lambda i: (i, 0))
    w_small = pl.BlockSpec((D_MODEL, N_SMALL), lambda i: (0, 0), pipeline_mode=pl.Buffered(1))
    return pl.pallas_call(
        functools.partial(_inproj_kernel, cast_nblks=tuple(casts.nblks)),
        out_shape=(jax.ShapeDtypeStruct((n, D_S5), F32),
                   jax.ShapeDtypeStruct((n, D_ATT), BF16),
                   jax.ShapeDtypeStruct((n, D_ATT), BF16),
                   jax.ShapeDtypeStruct((n, D_ATT), BF16),
                   jax.ShapeDtypeStruct((n, D_CONV), F32), *casts.out_shape),
        grid=(steps,),
        in_specs=[tok(D_MODEL), _resident((1, D_MODEL)), w_small,
                  _resident((1, LANES)), _resident((1, LANES)), *casts.in_specs],
        out_specs=(tok(D_S5), tok(D_ATT), tok(D_ATT), tok(D_ATT), tok(D_CONV), *casts.out_specs),
        compiler_params=pltpu.CompilerParams(
            dimension_semantics=("arbitrary",), vmem_limit_bytes=VMEM_LIMIT),
        name="inproj",
    )(x, g, w_in, qg2, kg2, *casts.args)


def _s5_tables(lambda_re, lambda_im, log_dt, b_re, b_im, c_re, c_im, d_skip):
    lr = jnp.minimum(lambda_re.astype(F32), -1e-4)
    li = lambda_im.astype(F32)
    dt = jnp.exp(log_dt.astype(F32))[:, None]
    mag = jnp.exp(lr * dt)
    ar = mag * jnp.cos(li * dt)
    ai = mag * jnp.sin(li * dt)
    den = lr * lr + li * li
    coef_r = ((ar - 1.0) * lr + ai * li) / den
    coef_i = (ai * lr - (ar - 1.0) * li) / den
    br = b_re.astype(F32)
    bi = b_im.astype(F32)
    bbar_r = coef_r[..., None] * br - coef_i[..., None] * bi
    bbar_i = coef_r[..., None] * bi + coef_i[..., None] * br
    cr = c_re.astype(F32)
    ci = c_im.astype(F32)

    pr, pi = [jnp.ones_like(ar)], [jnp.zeros_like(ai)]
    for _ in range(S5_L):
        r, i = pr[-1], pi[-1]
        pr.append(r * ar - i * ai)
        pi.append(r * ai + i * ar)
    pr = jnp.stack(pr)
    pi = jnp.stack(pi)

    ncb, gb = S5_NCB, S5_GB
    pw_r = pr[S5_L - 1::-1]
    pw_i = pi[S5_L - 1::-1]
    e_r = pw_r[..., None] * bbar_r[None] - pw_i[..., None] * bbar_i[None]
    e_i = pw_r[..., None] * bbar_i[None] + pw_i[..., None] * bbar_r[None]
    p_blk = jnp.stack([jnp.transpose(e_r, (0, 1, 3, 2)), jnp.transpose(e_i, (0, 1, 3, 2))], axis=3)
    p_blk = p_blk.reshape(S5_L, ncb, gb * S5_GROUP, 2 * S5_STATE)
    p_blk = jnp.transpose(p_blk, (1, 0, 2, 3)).reshape(ncb, S5_BROW, 2 * S5_STATE)

    m_r = pr[:S5_L, :, None, :] * cr[None] - pi[:S5_L, :, None, :] * ci[None]
    m_i = pr[:S5_L, :, None, :] * ci[None] + pi[:S5_L, :, None, :] * cr[None]
    kern = (jnp.sum(m_r[:, :, :, :, None] * bbar_r[None, :, None, :, :], axis=3)
            - jnp.sum(m_i[:, :, :, :, None] * bbar_i[None, :, None, :, :], axis=3))
    t_blk = jnp.transpose(kern, (0, 1, 3, 2)).reshape(S5_L, ncb, S5_CB, S5_GROUP)
    t_blk = jnp.transpose(t_blk, (1, 0, 2, 3))

    q_r = pr[1:, :, None, :] * cr[None] - pi[1:, :, None, :] * ci[None]
    q_i = -(pr[1:, :, None, :] * ci[None] + pi[1:, :, None, :] * cr[None])
    q_blk = jnp.stack([q_r, q_i], axis=0).reshape(2, S5_L, ncb, gb, S5_GROUP, S5_STATE)
    q_blk = jnp.transpose(q_blk, (2, 1, 4, 0, 3, 5))
    q_blk = q_blk.reshape(ncb, S5_L, S5_GROUP, S5_BSTATE)

    half_blocks = S5_BSTATE // 2 // LANES
    a_r = pr[S5_L].reshape(ncb, half_blocks, 1, LANES)
    a_i = pi[S5_L].reshape(ncb, half_blocks, 1, LANES)
    d_row = jnp.tile(d_skip.astype(F32).reshape(ncb, 1, S5_CB), (1, 1, S5_L))
    pt, q_mat = _s5_expand(p_blk.astype(BF16), t_blk.astype(BF16), q_blk.astype(BF16))
    return pt, q_mat, a_r, a_i, d_row


def _s5_expand_kernel(p_ref, t_ref, q_ref, pt_ref, qm_ref):
    def iota(shape, dim):
        return lax.broadcasted_iota(jnp.int32, shape, dim)

    def onehot(rows, cols, row_key, col_key):
        r, c = iota((rows, cols), 0), iota((rows, cols), 1)
        return jnp.where(row_key(r) == col_key(c), 1.0, 0.0)

    half = S5_BSTATE // 2
    rep = onehot(2 * S5_STATE, S5_BSTATE, lambda r: r,
                 lambda c: (c // half) * S5_STATE + c % S5_STATE).astype(BF16)
    same = onehot(S5_CB, S5_BSTATE, lambda r: r // S5_GROUP, lambda c: (c % half) // S5_STATE)
    rep_c = onehot(S5_GROUP, S5_CB, lambda r: r, lambda c: c % S5_GROUP).astype(BF16)
    same_g = onehot(S5_CB, S5_CB, lambda r: r // S5_GROUP, lambda c: c // S5_GROUP)
    same_q = onehot(S5_BSTATE, S5_CB, lambda r: (r % half) // S5_STATE, lambda c: c // S5_GROUP)
    lag_blk = [(jnp.dot(t_ref[d], rep_c, preferred_element_type=F32) * same_g).astype(BF16)
               for d in range(S5_L)]
    zero_blk = jnp.zeros((S5_CB, S5_CB), BF16)
    for s in range(S5_L):
        rows = slice(s * S5_CB, (s + 1) * S5_CB)
        p_full = jnp.dot(p_ref[rows, :], rep, preferred_element_type=F32) * same
        pt_ref[rows, :S5_BSTATE] = p_full.astype(BF16)
        for t in range(S5_L):
            cols = slice(S5_BSTATE + t * S5_CB, S5_BSTATE + (t + 1) * S5_CB)
            pt_ref[rows, cols] = lag_blk[t - s] if t >= s else zero_blk
    for t in range(S5_L):
        q_full = lax.dot_general(q_ref[t], rep_c, (((0,), (0,)), ((), ())),
                                 preferred_element_type=F32) * same_q
        qm_ref[:, t * S5_CB:(t + 1) * S5_CB] = q_full.astype(BF16)


def _s5_expand(p_blk, t_blk, q_blk):
    def blk(shape):
        return pl.BlockSpec((None,) + shape, lambda cb: (cb,) + (0,) * len(shape))
    return pl.pallas_call(
        _s5_expand_kernel,
        out_shape=(jax.ShapeDtypeStruct((S5_NCB, S5_BROW, S5_BSTATE + S5_BROW), BF16),
                   jax.ShapeDtypeStruct((S5_NCB, S5_BSTATE, S5_BROW), BF16)),
        grid=(S5_NCB,),
        in_specs=[blk((S5_BROW, 2 * S5_STATE)), blk((S5_L, S5_CB, S5_GROUP)),
                  blk((S5_L, S5_GROUP, S5_BSTATE))],
        out_specs=(blk((S5_BROW, S5_BSTATE + S5_BROW)), blk((S5_BSTATE, S5_BROW))),
        compiler_params=pltpu.CompilerParams(
            dimension_semantics=("arbitrary",), vmem_limit_bytes=VMEM_LIMIT),
        name="s5_tables",
    )(p_blk, t_blk, q_blk)


def _s5_kernel(ua_ref, ub_ref, pt_ref, q_ref, ar_ref, ai_ref, d_ref, wg_ref, o_ref,
               sr_ref, si_ref, u8_ref, es_ref, ot_ref):
    nb = ua_ref.shape[0]
    ncl = S5_TOK // S5_L
    nlb = S5_BSTATE // LANES
    hlb = nlb // 2
    u_refs = (ua_ref, ub_ref)
    low = lax.broadcasted_iota(jnp.int32, (1, LANES), 1) < S5_CB

    def interleave(a, b):
        return (jnp.where(low, a, pltpu.roll(b, S5_CB, axis=1)),
                jnp.where(low, pltpu.roll(a, S5_CB, axis=1), b))

    @pl.when(pl.program_id(0) == 0)
    def _():
        sr_ref[...] = jnp.zeros_like(sr_ref)
        si_ref[...] = jnp.zeros_like(si_ref)

    sub = 8
    ncg = ncl // sub
    grp = nb * sub
    for hh, u_ref in enumerate(u_refs):
        for t2 in range(S5_L // 2):
            for cg in range(ncg):
                tok = [u_ref[:, pl.ds(cg * sub * S5_L + 2 * t2 + k, sub, stride=S5_L), :]
                       .reshape(grp, LANES) for k in range(2)]
                for k, blk in enumerate(interleave(*tok)):
                    u8_ref[2 * hh + k, cg * grp:(cg + 1) * grp, t2 * LANES:(t2 + 1) * LANES] = blk

    for cb in range(S5_NCB):
        e = jnp.dot(u8_ref[cb].astype(BF16), pt_ref[cb, :, :S5_BSTATE],
                    preferred_element_type=F32)
        for j in range(nlb):
            es_ref[cb, j] = e[:, j * LANES:(j + 1) * LANES]

    a_r = ar_ref[...]
    a_i = ai_ref[...]
    sr = sr_ref[...]
    si = si_ref[...]
    for cl in range(ncl):
        step = pl.ds((cl // sub) * grp + cl % sub, nb, stride=sub)
        e_r = es_ref[:, :hlb, step, :]
        e_i = es_ref[:, hlb:, step, :]
        es_ref[:, :hlb, step, :] = sr
        es_ref[:, hlb:, step, :] = si
        sr, si = a_r * sr - a_i * si + e_r, a_r * si + a_i * sr + e_i
    sr_ref[...] = sr
    si_ref[...] = si

    ys = []
    for cb in range(S5_NCB):
        u8 = u8_ref[cb]
        s_all = jnp.concatenate([es_ref[cb, j] for j in range(nlb)], axis=1).astype(BF16)
        y = (jnp.dot(u8.astype(BF16), pt_ref[cb, :, S5_BSTATE:], preferred_element_type=F32)
             + jnp.dot(s_all, q_ref[cb], preferred_element_type=F32) + d_ref[cb] * u8)
        ys.append(jax.nn.gelu(y))
    for t2 in range(S5_L // 2):
        lanes = slice(t2 * LANES, (t2 + 1) * LANES)
        halves = [interleave(ys[2 * hh][:, lanes], ys[2 * hh + 1][:, lanes]) for hh in range(2)]
        for k in range(2):
            t = 2 * t2 + k
            yt = jnp.concatenate([halves[0][k], halves[1][k]], axis=1).astype(BF16)
            ag = jnp.dot(yt, wg_ref[...], preferred_element_type=F32)
            out = ag[:, :D_S5] * _sigmoid(ag[:, D_S5:])
            for hh in range(D_S5 // LANES):
                for cg in range(ncg):
                    ot_ref[hh, :, pl.ds(cg * sub * S5_L + t, sub, stride=S5_L), :] = (
                        out[cg * grp:(cg + 1) * grp, hh * LANES:(hh + 1) * LANES].reshape(nb, sub, LANES))
    o_ref[...] = jnp.concatenate([ot_ref[0], ot_ref[1]], axis=-1).astype(BF16)


def _s5(u, pt, q_mat, a_r, a_i, d_row, w_glu):
    nb, seq, _ = u.shape
    rows = nb * (S5_TOK // S5_L)
    nlb = S5_BSTATE // LANES
    assert 2 * S5_CB == LANES and D_S5 == 2 * LANES
    return pl.pallas_call(
        _s5_kernel,
        out_shape=jax.ShapeDtypeStruct((nb, seq, D_S5), BF16),
        grid=(seq // S5_TOK,),
        in_specs=[pl.BlockSpec((nb, S5_TOK, LANES), lambda i: (0, i, 0)),
                  pl.BlockSpec((nb, S5_TOK, LANES), lambda i: (0, i, 1)),
                  _resident(pt.shape), _resident(q_mat.shape), _resident(a_r.shape),
                  _resident(a_i.shape), _resident(d_row.shape), _resident(w_glu.shape)],
        out_specs=pl.BlockSpec((nb, S5_TOK, D_S5), lambda i: (0, i, 0)),
        scratch_shapes=[pltpu.VMEM((S5_NCB, nlb // 2, nb, LANES), F32),
                        pltpu.VMEM((S5_NCB, nlb // 2, nb, LANES), F32),
                        pltpu.VMEM((S5_NCB, rows, S5_BROW), F32),
                        pltpu.VMEM((S5_NCB, nlb, rows, LANES), F32),
                        pltpu.VMEM((D_S5 // LANES, nb, S5_TOK, LANES), F32)],
        compiler_params=pltpu.CompilerParams(
            dimension_semantics=("arbitrary",), vmem_limit_bytes=VMEM_LIMIT),
        name="s5",
    )(u, u, pt, q_mat, a_r, a_i, d_row, w_glu)


def _attn_bias(rel_bias):
    rb = rel_bias.astype(F32) * LOG2E
    far_past, far_future = rb[:, 2 * MAX_REL:], rb[:, :1]
    period = ATT_QB + ATT_W
    n_const = N_LEFT * CHUNK - MAX_REL
    vec = jnp.concatenate([
        jnp.broadcast_to(far_past, (HEADS, n_const)),
        rb[:, ::-1],
        jnp.broadcast_to(far_future, (HEADS, ATT_W - n_const - 2 * MAX_REL - 1)),
        jnp.broadcast_to(far_past, (HEADS, ATT_QB)),
    ], axis=1)
    toep = jnp.tile(vec, (1, ATT_QB))[:, :ATT_QB * (period - 1)]
    toep = toep.reshape(HEADS, ATT_QB, period - 1)[:, :, :ATT_W]
    r = jnp.arange(ATT_QB)[:, None]
    c = jnp.arange(ATT_W)[None, :] - N_LEFT * CHUNK
    dchunk = r // CHUNK - jnp.floor_divide(c, CHUNK)
    ok = (dchunk >= 0) & (dchunk <= N_LEFT)
    bias = jnp.where(ok[None], toep, -1e30)
    return bias.reshape(HEADS // 2, 2, ATT_QB, ATT_W)


def _attn_kernel(q_ref, k_ref, v_ref, bm_ref, o_ref):
    seq = q_ref.shape[1]
    lane = lax.broadcasted_iota(jnp.int32, (1, LANES), 1)
    first = lane < HEAD_DIM

    def block(q0, k0, width):
        for pr in range(D_ATT // LANES):
            cols = slice(pr * LANES, (pr + 1) * LANES)
            q2 = q_ref[0, pl.ds(q0, ATT_QB), cols]
            outs = []
            for hh in range(2):
                sel = first if hh == 0 else jnp.logical_not(first)
                qh = jnp.where(sel, q2, jnp.zeros_like(q2))
                for c0 in range(0, width, ATT_KT):
                    kw = k_ref[0, pl.ds(k0 + c0, ATT_KT), cols]
                    vw = v_ref[0, pl.ds(k0 + c0, ATT_KT), cols]
                    s = lax.dot_general(qh, kw, (((1,), (1,)), ((), ())),
                                        preferred_element_type=F32)
                    b0 = ATT_W - width + c0
                    s = s + bm_ref[pr, hh, :, b0:b0 + ATT_KT]
                    vh = jnp.where(sel, vw, jnp.ones_like(vw))
                    if c0 == 0:
                        mx = jnp.max(s, axis=-1, keepdims=True)
                        acc = jnp.dot(jnp.exp2(s - mx).astype(BF16), vh, preferred_element_type=F32)
                    else:
                        mx_new = jnp.maximum(mx, jnp.max(s, axis=-1, keepdims=True))
                        p = jnp.exp2(s - mx_new).astype(BF16)
                        acc = acc * jnp.exp2(mx - mx_new) + jnp.dot(p, vh, preferred_element_type=F32)
                        mx = mx_new
                outs.append(acc)
            num = jnp.where(first, outs[0], outs[1])
            den = pltpu.roll(jnp.where(first, outs[1], outs[0]), HEAD_DIM, axis=1)
            o_ref[0, pl.ds(q0, ATT_QB), cols] = (num / den).astype(BF16)

    n_short = N_LEFT * CHUNK // ATT_QB
    for qb in range(n_short):
        block(qb * ATT_QB, 0, (qb + 1) * ATT_QB)

    def body(qb, carry):
        q0 = pl.multiple_of(qb * ATT_QB, ATT_QB)
        k0 = pl.multiple_of(q0 - N_LEFT * CHUNK, ATT_QB)
        block(q0, k0, ATT_W)
        return carry
    lax.fori_loop(n_short, seq // ATT_QB, body, 0)


def _attn(q, k, v, bm):
    nb, seq, _ = q.shape
    seqblk = pl.BlockSpec((1, seq, D_ATT), lambda b: (b, 0, 0))
    return pl.pallas_call(
        _attn_kernel,
        out_shape=jax.ShapeDtypeStruct((nb, seq, D_ATT), BF16),
        grid=(nb,),
        in_specs=[seqblk, seqblk, seqblk, _resident(bm.shape)],
        out_specs=seqblk,
        compiler_params=pltpu.CompilerParams(
            dimension_semantics=("arbitrary",), vmem_limit_bytes=VMEM_LIMIT),
        name="attn",
    )(q, k, v, bm)


def _conv_taps(hc_ref, cw_ref, cb_ref, cbuf_ref, first_tile):
    nh = D_CONV // LANES
    tm = hc_ref.shape[0]

    @pl.when(first_tile)
    def _():
        cbuf_ref[:, :CONV_PAD, :] = jnp.zeros((nh, CONV_PAD, LANES), F32)
    for hh in range(nh):
        cbuf_ref[hh, CONV_PAD:, :] = hc_ref[:, hh * LANES:(hh + 1) * LANES]
    shift = CONV_PAD - (CONV_WIDTH - 1)
    halves = []
    for hh in range(nh):
        tiles = []
        for r0 in range(0, tm, CONV_TR):
            acc = jnp.zeros((CONV_TR, LANES), F32) + cb_ref[hh]
            for j in range(CONV_WIDTH):
                lo = r0 + shift + j
                acc = acc + cw_ref[hh, j:j + 1, :] * cbuf_ref[hh, lo:lo + CONV_TR, :]
            tiles.append(acc)
        halves.append(jnp.concatenate(tiles, axis=0))
    for hh in range(nh):
        cbuf_ref[hh, :CONV_PAD, :] = cbuf_ref[hh, tm:tm + CONV_PAD, :]
    return halves


def _conv_norm(halves, lg_ref, lb_ref):
    acc = jnp.concatenate(halves, axis=-1)
    mu = jnp.mean(acc, axis=-1, keepdims=True)
    cen = acc - mu
    var = jnp.mean(cen * cen, axis=-1, keepdims=True)
    y = (cen * lax.rsqrt(var + EPS)) * lg_ref[...] + lb_ref[...]
    return (y * _sigmoid(y)).astype(BF16)


def _after(v, zero_ref, width):
    bits = pltpu.bitcast(v, jnp.uint32)
    acc = bits[0:8, :]
    for r0 in range(8, v.shape[0], 8):
        acc = acc | bits[r0:r0 + 8, :]
    zero = pltpu.bitcast(acc[0:1, :] & zero_ref[...], F32)
    return jnp.tile(zero, (1, width // LANES)).astype(BF16)


def _merge_kernel(x_ref, s5_ref, at_ref, hc_ref, g_ref, wg_ref, bg_ref, ws_ref, wa_ref, wc_ref, wo_ref,
                  cw_ref, cb_ref, lg_ref, lb_ref, zero_ref, *refs, tiles_per_seq, cast_nblks):
    n_cast = len(cast_nblks)
    o_ref, cbuf_ref = refs[n_cast], refs[-1]
    _Casts.run_predicated(pl.program_id(0), cast_nblks, refs[:n_cast], refs[n_cast + 1:-1])
    halves = _conv_taps(hc_ref, cw_ref, cb_ref, cbuf_ref, pl.program_id(0) % tiles_per_seq == 0)
    _Casts.run_every_step(cast_nblks, refs[:n_cast], refs[n_cast + 1:-1])
    x = x_ref[...]
    h = _rmsnorm_bf16(x, g_ref[...])
    lhs = (h, h + _after(halves[0], zero_ref, D_MODEL), h + _after(halves[1], zero_ref, D_MODEL))
    branches = (lambda: s5_ref[...], lambda: at_ref[...], lambda: _conv_norm(halves, lg_ref, lb_ref))
    merged = jnp.zeros_like(x)
    for i, (branch, w_ref) in enumerate(zip(branches, (ws_ref, wa_ref, wc_ref))):
        cols = slice(i * D_MODEL, (i + 1) * D_MODEL)
        gcols = slice(N_SMALL + i * D_MODEL, N_SMALL + (i + 1) * D_MODEL)
        logits = jnp.dot(lhs[i], wg_ref[:, gcols], preferred_element_type=F32) + bg_ref[:, cols]
        y = jnp.dot(branch(), w_ref[...], preferred_element_type=F32)
        merged = merged + _sigmoid(logits) * y
    o_ref[...] = x + jnp.dot(merged.astype(BF16), wo_ref[...], preferred_element_type=F32)


def _merge(x, s5o, ato, hc, g, w_in, b_gate, w_s5, w_at, w_cv, w_out, w_dw, b_dw, ln_g, ln_b, seq,
           cast_items):
    n = x.shape[0]
    nh = D_CONV // LANES
    zero = jnp.zeros((1, LANES), jnp.uint32)
    assert seq % TM == 0
    steps = n // TM
    casts = _Casts(cast_items, steps)
    def tok(width):
        return pl.BlockSpec((TM, width), lambda i: (i, 0))
    return pl.pallas_call(
        functools.partial(_merge_kernel, tiles_per_seq=seq // TM, cast_nblks=tuple(casts.nblks)),
        out_shape=(jax.ShapeDtypeStruct((n, D_MODEL), F32), *casts.out_shape),
        grid=(steps,),
        in_specs=[tok(D_MODEL), tok(D_S5), tok(D_ATT), tok(D_CONV),
                  _resident((1, D_MODEL)), _resident(w_in.shape),
                  _resident((1, 3 * D_MODEL)), _resident((D_S5, D_MODEL)),
                  _resident((D_ATT, D_MODEL)), _resident((D_CONV, D_MODEL)),
                  _resident((D_MODEL, D_MODEL)), _resident((nh, CONV_WIDTH, LANES)),
                  _resident((nh, 1, LANES)), _resident((1, D_CONV)), _resident((1, D_CONV)),
                  _resident((1, LANES)), *casts.in_specs],
        out_specs=(tok(D_MODEL), *casts.out_specs),
        scratch_shapes=[pltpu.VMEM((nh, CONV_PAD + TM, LANES), F32)],
        compiler_params=pltpu.CompilerParams(
            dimension_semantics=("arbitrary",), vmem_limit_bytes=VMEM_LIMIT),
        name="merge",
    )(x, s5o, ato, hc, g, w_in, b_gate, w_s5, w_at, w_cv, w_out, w_dw, b_dw, ln_g, ln_b, zero,
      *casts.args)


def kernel(x, ffn1_norm, ffn1_w_up, ffn1_w_down, mix_norm, w_in, b_gate, s5_lambda_re, s5_lambda_im, s5_log_dt, s5_b_re, s5_b_im, s5_c_re, s5_c_im, s5_d, s5_w_glu, w_br_s5, attn_q_gain, attn_k_gain, attn_rel_bias, w_br_attn, conv_w_dw, conv_b_dw, conv_ln_g, conv_ln_b, w_br_conv, w_out, ffn2_norm, ffn2_w_up, ffn2_w_down):
    nb, seq, d = x.shape
    n = nb * seq
    depth = ffn1_norm.shape[0]
    xt = x.reshape(n, d)
    row = lambda v: v.reshape(1, -1).astype(F32)
    f1_up, f1_down, w_in_l = (ffn1_w_up[0].astype(BF16), ffn1_w_down[0].astype(BF16),
                              w_in[0].astype(BF16))
    for l in range(depth):
        xt = _ffn(xt, row(ffn1_norm[l]), f1_up, f1_down)

        qg2 = jnp.tile(row(attn_q_gain[l]), (1, LANES // HEAD_DIM))
        kg2 = jnp.tile(row(attn_k_gain[l]), (1, LANES // HEAD_DIM))
        own = (ffn2_w_up, ffn2_w_down, w_br_s5, w_br_attn, w_br_conv, w_out, s5_w_glu)
        u, q, k, v, hc, f2_up, f2_down, w_s5, w_at, w_cv, w_o, w_glu = _inproj(
            xt, row(mix_norm[l]), w_in_l, qg2, kg2, [(w, l) for w in own])

        pt, q_mat, a_r, a_i, d_row = _s5_tables(
            s5_lambda_re[l], s5_lambda_im[l], s5_log_dt[l], s5_b_re[l], s5_b_im[l],
            s5_c_re[l], s5_c_im[l], s5_d[l])
        s5o = _s5(u.reshape(nb, seq, D_S5), pt, q_mat, a_r, a_i, d_row, w_glu)

        ato = _attn(q.reshape(nb, seq, D_ATT), k.reshape(nb, seq, D_ATT),
                    v.reshape(nb, seq, D_ATT), _attn_bias(attn_rel_bias[l]))

        nh = D_CONV // LANES
        w_dw = conv_w_dw[l].astype(F32).reshape(CONV_WIDTH, nh, LANES).transpose(1, 0, 2)
        nxt = (ffn1_w_up, ffn1_w_down, w_in) if l + 1 < depth else ()
        xt, *nxt_bf16 = _merge(
            xt, s5o.reshape(n, D_S5), ato.reshape(n, D_ATT), hc, row(mix_norm[l]), w_in_l,
            row(b_gate[l]), w_s5, w_at, w_cv, w_o, w_dw,
            conv_b_dw[l].astype(F32).reshape(nh, 1, LANES), row(conv_ln_g[l]), row(conv_ln_b[l]),
            seq, [(w, l + 1) for w in nxt])
        if nxt_bf16:
            f1_up, f1_down, w_in_l = nxt_bf16

        xt = _ffn(xt, row(ffn2_norm[l]), f2_up, f2_down)
    return xt.reshape(nb, seq, d)
```

```python
import functools
import math

import jax
import jax.numpy as jnp
from jax import lax
from jax.experimental import pallas as pl
from jax.experimental.pallas import tpu as pltpu

F32 = jnp.float32
BF16 = jnp.bfloat16

D_MODEL = 1024
D_S5 = 256
S5_GROUP = 16
S5_GROUPS = 16
S5_STATE = 64
D_ATT = 512
HEAD_DIM = 64
HEADS = 8
CHUNK = 64
N_LEFT = 8
MAX_REL = 128
D_CONV = 256
CONV_WIDTH = 31
D_FF = 2816
EPS = 1e-6
LOG2E = math.log2(math.e)
N_SMALL = D_S5 + 3 * D_ATT + 2 * D_CONV

LANES = 128
MXU_DIM = 256
VMEM_LIMIT = 56 * 1024 * 1024

TM = 512
TM_WIDE = 1024
CAST_MIN_ROWS = 16
CAST_ROWS = 64
S5_L = 8
S5_CB = LANES // 2
S5_GB = S5_CB // S5_GROUP
S5_NCB = D_S5 // S5_CB
S5_BROW = S5_L * S5_CB
S5_BSTATE = 2 * S5_GB * S5_STATE
S5_TOK = 256
ATT_QB = 4 * CHUNK
ATT_W = ATT_QB + N_LEFT * CHUNK
ATT_KT = MXU_DIM
ATT_PERIOD = ATT_QB + ATT_W
CONV_TR = 128
CONV_PAD = 32


def _resident(shape):
    nd = len(shape)
    return pl.BlockSpec(shape, lambda *_: (0,) * nd, pipeline_mode=pl.Buffered(1))


def _rmsnorm_bf16(x, g):
    ms = jnp.mean(x * x, axis=-1, keepdims=True)
    return ((x * lax.rsqrt(ms + EPS)) * g).astype(BF16)


def _sigmoid(x):
    return jax.nn.sigmoid(x)


class _Casts:
    def __init__(self, items, steps):
        self.in_specs, self.out_specs, self.out_shape, self.args, self.nblks = [], [], [], [], []
        for stacked, layer in items:
            _, rows, cols = stacked.shape
            every_step = rows % CAST_MIN_ROWS == 0 and rows // CAST_MIN_ROWS <= steps
            rb = CAST_MIN_ROWS if every_step else CAST_ROWS
            nblk = rows // rb
            assert nblk * rb == rows and nblk <= steps
            self.nblks.append(None if every_step else nblk)
            self.in_specs.append(pl.BlockSpec(
                (None, rb, cols),
                lambda i, layer=layer, nblk=nblk: (layer, jnp.minimum(i, nblk - 1), 0)))
            self.out_specs.append(pl.BlockSpec(
                (rb, cols), lambda i, nblk=nblk: (jnp.minimum(i, nblk - 1), 0)))
            self.out_shape.append(jax.ShapeDtypeStruct((rows, cols), BF16))
            self.args.append(stacked)

    @staticmethod
    def run_predicated(step, nblks, src_refs, dst_refs):
        for nblk in sorted({n for n in nblks if n is not None}):
            @pl.when(step < nblk)
            def _():
                for n, src, dst in zip(nblks, src_refs, dst_refs):
                    if n == nblk:
                        dst[...] = src[...].astype(BF16)

    @staticmethod
    def run_every_step(nblks, src_refs, dst_refs):
        for n, src, dst in zip(nblks, src_refs, dst_refs):
            if n is None:
                dst[...] = src[...].astype(BF16)


def _ffn_kernel(x_ref, g_ref, wa_ref, wb_ref, wd_ref, o_ref, *, bounds):
    x = x_ref[...]
    h = _rmsnorm_bf16(x, g_ref[...])
    y = jnp.zeros_like(x)
    for f0, f1 in zip(bounds[:-1], bounds[1:]):
        a = jnp.dot(h, wa_ref[:, f0:f1], preferred_element_type=F32)
        b = jnp.dot(h, wb_ref[:, f0:f1], preferred_element_type=F32)
        act = ((a * _sigmoid(a)) * b).astype(BF16)
        y = y + jnp.dot(act, wd_ref[f0:f1, :], preferred_element_type=F32)
    o_ref[...] = x + 0.5 * y


def _ffn_bounds():
    tiles = D_FF // MXU_DIM
    assert tiles * MXU_DIM == D_FF
    return (0, (tiles + 1) // 2 * MXU_DIM, D_FF)


def _ffn(x, g, w_up, w_down):
    n = x.shape[0]
    tok = pl.BlockSpec((TM_WIDE, D_MODEL), lambda i: (i, 0))
    def up_half(j):
        return pl.BlockSpec((D_MODEL, D_FF), lambda i: (0, j), pipeline_mode=pl.Buffered(1))
    return pl.pallas_call(
        functools.partial(_ffn_kernel, bounds=_ffn_bounds()),
        out_shape=jax.ShapeDtypeStruct((n, D_MODEL), F32),
        grid=(n // TM_WIDE,),
        in_specs=[tok, _resident((1, D_MODEL)), up_half(0), up_half(1), _resident((D_FF, D_MODEL))],
        out_specs=tok,
        compiler_params=pltpu.CompilerParams(
            dimension_semantics=("arbitrary",), vmem_limit_bytes=VMEM_LIMIT),
        name="ffn",
    )(x, g, w_up, w_up, w_down)


def _head_pair_norm(x2, gain2):
    lane = lax.broadcasted_iota(jnp.int32, (1, LANES), 1)
    first = lane < HEAD_DIM
    sq = x2 * x2
    s_a = jnp.sum(jnp.where(first, sq, 0.0), axis=-1, keepdims=True)
    s_b = jnp.sum(jnp.where(first, 0.0, sq), axis=-1, keepdims=True)
    ms = jnp.where(first, s_a, s_b) * (1.0 / HEAD_DIM)
    return (x2 * lax.rsqrt(ms + EPS)) * gain2


def _inproj_kernel(x_ref, g_ref, w_ref, qg_ref, kg_ref, *refs, cast_nblks):
    n_cast = len(cast_nblks)
    u_ref, q_ref, k_ref, v_ref, hc_ref = refs[n_cast:n_cast + 5]
    _Casts.run_predicated(pl.program_id(0), cast_nblks, refs[:n_cast], refs[n_cast + 5:])
    _Casts.run_every_step(cast_nblks, refs[:n_cast], refs[n_cast + 5:])
    h = _rmsnorm_bf16(x_ref[...], g_ref[...])
    proj = jnp.dot(h, w_ref[...], preferred_element_type=F32)
    u_ref[...] = proj[:, :D_S5]
    q0, k0, v0, z0 = D_S5, D_S5 + D_ATT, D_S5 + 2 * D_ATT, D_S5 + 3 * D_ATT
    scale = HEAD_DIM ** -0.5 * LOG2E
    for p in range(D_ATT // LANES):
        lo = p * LANES
        qn = _head_pair_norm(proj[:, q0 + lo:q0 + lo + LANES], qg_ref[...])
        kn = _head_pair_norm(proj[:, k0 + lo:k0 + lo + LANES], kg_ref[...])
        q_ref[:, lo:lo + LANES] = (qn * scale).astype(BF16)
        k_ref[:, lo:lo + LANES] = kn.astype(BF16)
    v_ref[...] = proj[:, v0:v0 + D_ATT].astype(BF16)
    a = proj[:, z0:z0 + D_CONV]
    gt = proj[:, z0 + D_CONV:z0 + 2 * D_CONV]
    hc_ref[...] = a * _sigmoid(gt)


def _inproj(x, g, w_in, qg2, kg2, cast_items):
    n = x.shape[0]
    steps = n // TM_WIDE
    casts = _Casts(cast_items, steps)
    def tok(width):
        return pl.BlockSpec((TM_WIDE, width), lambda i: (i, 0))
    w_small = pl.BlockSpec((D_MODEL, N_SMALL), lambda i: (0, 0), pipeline_mode=pl.Buffered(1))
    return pl.pallas_call(
        functools.partial(_inproj_kernel, cast_nblks=tuple(casts.nblks)),
        out_shape=(jax.ShapeDtypeStruct((n, D_S5), F32),
                   jax.ShapeDtypeStruct((n, D_ATT), BF16),
                   jax.ShapeDtypeStruct((n, D_ATT), BF16),
                   jax.ShapeDtypeStruct((n, D_ATT), BF16),
                   jax.ShapeDtypeStruct((n, D_CONV), F32), *casts.out_shape),
        grid=(steps,),
        in_specs=[tok(D_MODEL), _resident((1, D_MODEL)), w_small,
                  _resident((1, LANES)), _resident((1, LANES)), *casts.in_specs],
        out_specs=(tok(D_S5), tok(D_ATT), tok(D_ATT), tok(D_ATT), tok(D_CONV), *casts.out_specs),
        compiler_params=pltpu.CompilerParams(
            dimension_semantics=("arbitrary",), vmem_limit_bytes=VMEM_LIMIT),
        name="inproj",
    )(x, g, w_in, qg2, kg2, *casts.args)


def _s5_tables(lambda_re, lambda_im, log_dt, b_re, b_im, c_re, c_im, d_skip):
    lr = jnp.minimum(lambda_re.astype(F32), -1e-4)
    li = lambda_im.astype(F32)
    dt = jnp.exp(log_dt.astype(F32))[:, None]
    mag = jnp.exp(lr * dt)
    ar = mag * jnp.cos(li * dt)
    ai = mag * jnp.sin(li * dt)
    den = lr * lr + li * li
    coef_r = ((ar - 1.0) * lr + ai * li) / den
    coef_i = (ai * lr - (ar - 1.0) * li) / den
    br = b_re.astype(F32)
    bi = b_im.astype(F32)
    bbar_r = coef_r[..., None] * br - coef_i[..., None] * bi
    bbar_i = coef_r[..., None] * bi + coef_i[..., None] * br
    cr = c_re.astype(F32)
    ci = c_im.astype(F32)

    pr, pi = [jnp.ones_like(ar)], [jnp.zeros_like(ai)]
    for _ in range(S5_L):
        r, i = pr[-1], pi[-1]
        pr.append(r * ar - i * ai)
        pi.append(r * ai + i * ar)
    pr = jnp.stack(pr)
    pi = jnp.stack(pi)

    ncb, gb = S5_NCB, S5_GB
    pw_r = pr[S5_L - 1::-1]
    pw_i = pi[S5_L - 1::-1]
    e_r = pw_r[..., None] * bbar_r[None] - pw_i[..., None] * bbar_i[None]
    e_i = pw_r[..., None] * bbar_i[None] + pw_i[..., None] * bbar_r[None]
    p_blk = jnp.stack([jnp.transpose(e_r, (0, 1, 3, 2)), jnp.transpose(e_i, (0, 1, 3, 2))], axis=3)
    p_blk = p_blk.reshape(S5_L, ncb, gb * S5_GROUP, 2 * S5_STATE)
    p_blk = jnp.transpose(p_blk, (1, 0, 2, 3)).reshape(ncb, S5_BROW, 2 * S5_STATE)

    m_r = pr[:S5_L, :, None, :] * cr[None] - pi[:S5_L, :, None, :] * ci[None]
    m_i = pr[:S5_L, :, None, :] * ci[None] + pi[:S5_L, :, None, :] * cr[None]
    kern = (jnp.sum(m_r[:, :, :, :, None] * bbar_r[None, :, None, :, :], axis=3)
            - jnp.sum(m_i[:, :, :, :, None] * bbar_i[None, :, None, :, :], axis=3))
    t_blk = jnp.transpose(kern, (0, 1, 3, 2)).reshape(S5_L, ncb, S5_CB, S5_GROUP)
    t_blk = jnp.transpose(t_blk, (1, 0, 2, 3))

    q_r = pr[1:, :, None, :] * cr[None] - pi[1:, :, None, :] * ci[None]
    q_i = -(pr[1:, :, None, :] * ci[None] + pi[1:, :, None, :] * cr[None])
    q_blk = jnp.stack([q_r, q_i], axis=0).reshape(2, S5_L, ncb, gb, S5_GROUP, S5_STATE)
    q_blk = jnp.transpose(q_blk, (2, 1, 4, 0, 3, 5))
    q_blk = q_blk.reshape(ncb, S5_L, S5_GROUP, S5_BSTATE)

    half_blocks = S5_BSTATE // 2 // LANES
    a_r = pr[S5_L].reshape(ncb, half_blocks, 1, LANES)
    a_i = pi[S5_L].reshape(ncb, half_blocks, 1, LANES)
    d_row = jnp.tile(d_skip.astype(F32).reshape(ncb, 1, S5_CB), (1, 1, S5_L))
    pt, q_mat = _s5_expand(p_blk.astype(BF16), t_blk.astype(BF16), q_blk.astype(BF16))
    return pt, q_mat, a_r, a_i, d_row


def _s5_expand_kernel(p_ref, t_ref, q_ref, pt_ref, qm_ref):
    def iota(shape, dim):
        return lax.broadcasted_iota(jnp.int32, shape, dim)

    def onehot(rows, cols, row_key, col_key):
        r, c = iota((rows, cols), 0), iota((rows, cols), 1)
        return jnp.where(row_key(r) == col_key(c), 1.0, 0.0)

    half = S5_BSTATE // 2
    rep = onehot(2 * S5_STATE, S5_BSTATE, lambda r: r,
                 lambda c: (c // half) * S5_STATE + c % S5_STATE).astype(BF16)
    same = onehot(S5_CB, S5_BSTATE, lambda r: r // S5_GROUP, lambda c: (c % half) // S5_STATE)
    rep_c = onehot(S5_GROUP, S5_CB, lambda r: r, lambda c: c % S5_GROUP).astype(BF16)
    same_g = onehot(S5_CB, S5_CB, lambda r: r // S5_GROUP, lambda c: c // S5_GROUP)
    same_q = onehot(S5_BSTATE, S5_CB, lambda r: (r % half) // S5_STATE, lambda c: c // S5_GROUP)
    lag_blk = [(jnp.dot(t_ref[d], rep_c, preferred_element_type=F32) * same_g).astype(BF16)
               for d in range(S5_L)]
    zero_blk = jnp.zeros((S5_CB, S5_CB), BF16)
    for s in range(S5_L):
        rows = slice(s * S5_CB, (s + 1) * S5_CB)
        p_full = jnp.dot(p_ref[rows, :], rep, preferred_element_type=F32) * same
        pt_ref[rows, :S5_BSTATE] = p_full.astype(BF16)
        for t in range(S5_L):
            cols = slice(S5_BSTATE + t * S5_CB, S5_BSTATE + (t + 1) * S5_CB)
            pt_ref[rows, cols] = lag_blk[t - s] if t >= s else zero_blk
    for t in range(S5_L):
        q_full = lax.dot_general(q_ref[t], rep_c, (((0,), (0,)), ((), ())),
                                 preferred_element_type=F32) * same_q
        qm_ref[:, t * S5_CB:(t + 1) * S5_CB] = q_full.astype(BF16)


def _s5_expand(p_blk, t_blk, q_blk):
    def blk(shape):
        return pl.BlockSpec((None,) + shape, lambda cb: (cb,) + (0,) * len(shape))
    return pl.pallas_call(
        _s5_expand_kernel,
        out_shape=(jax.ShapeDtypeStruct((S5_NCB, S5_BROW, S5_BSTATE + S5_BROW), BF16),
                   jax.ShapeDtypeStruct((S5_NCB, S5_BSTATE, S5_BROW), BF16)),
        grid=(S5_NCB,),
        in_specs=[blk((S5_BROW, 2 * S5_STATE)), blk((S5_L, S5_CB, S5_GROUP)),
                  blk((S5_L, S5_GROUP, S5_BSTATE))],
        out_specs=(blk((S5_BROW, S5_BSTATE + S5_BROW)), blk((S5_BSTATE, S5_BROW))),
        compiler_params=pltpu.CompilerParams(
            dimension_semantics=("arbitrary",), vmem_limit_bytes=VMEM_LIMIT),
        name="s5_tables",
    )(p_blk, t_blk, q_blk)


def _s5_kernel(ua_ref, ub_ref, pt_ref, q_ref, ar_ref, ai_ref, d_ref, wg_ref, o_ref,
               sr_ref, si_ref, u8_ref, es_ref, ot_ref):
    nb = ua_ref.shape[0]
    ncl = S5_TOK // S5_L
    nlb = S5_BSTATE // LANES
    hlb = nlb // 2
    u_refs = (ua_ref, ub_ref)
    low = lax.broadcasted_iota(jnp.int32, (1, LANES), 1) < S5_CB

    def interleave(a, b):
        return (jnp.where(low, a, pltpu.roll(b, S5_CB, axis=1)),
                jnp.where(low, pltpu.roll(a, S5_CB, axis=1), b))

    @pl.when(pl.program_id(0) == 0)
    def _():
        sr_ref[...] = jnp.zeros_like(sr_ref)
        si_ref[...] = jnp.zeros_like(si_ref)

    sub = 8
    ncg = ncl // sub
    grp = nb * sub
    for hh, u_ref in enumerate(u_refs):
        for t2 in range(S5_L // 2):
            for cg in range(ncg):
                tok = [u_ref[:, pl.ds(cg * sub * S5_L + 2 * t2 + k, sub, stride=S5_L), :]
                       .reshape(grp, LANES) for k in range(2)]
                for k, blk in enumerate(interleave(*tok)):
                    u8_ref[2 * hh + k, cg * grp:(cg + 1) * grp, t2 * LANES:(t2 + 1) * LANES] = blk

    for cb in range(S5_NCB):
        e = jnp.dot(u8_ref[cb].astype(BF16), pt_ref[cb, :, :S5_BSTATE],
                    preferred_element_type=F32)
        for j in range(nlb):
            es_ref[cb, j] = e[:, j * LANES:(j + 1) * LANES]

    a_r = ar_ref[...]
    a_i = ai_ref[...]
    sr = sr_ref[...]
    si = si_ref[...]
    for cl in range(ncl):
        step = pl.ds((cl // sub) * grp + cl % sub, nb, stride=sub)
        e_r = es_ref[:, :hlb, step, :]
        e_i = es_ref[:, hlb:, step, :]
        es_ref[:, :hlb, step, :] = sr
        es_ref[:, hlb:, step, :] = si
        sr, si = a_r * sr - a_i * si + e_r, a_r * si + a_i * sr + e_i
    sr_ref[...] = sr
    si_ref[...] = si

    ys = []
    for cb in range(S5_NCB):
        u8 = u8_ref[cb]
        s_all = jnp.concatenate([es_ref[cb, j] for j in range(nlb)], axis=1).astype(BF16)
        y = (jnp.dot(u8.astype(BF16), pt_ref[cb, :, S5_BSTATE:], preferred_element_type=F32)
             + jnp.dot(s_all, q_ref[cb], preferred_element_type=F32) + d_ref[cb] * u8)
        ys.append(jax.nn.gelu(y))
    for t2 in range(S5_L // 2):
        lanes = slice(t2 * LANES, (t2 + 1) * LANES)
        halves = [interleave(ys[2 * hh][:, lanes], ys[2 * hh + 1][:, lanes]) for hh in range(2)]
        for k in range(2):
            t = 2 * t2 + k
            yt = jnp.concatenate([halves[0][k], halves[1][k]], axis=1).astype(BF16)
            ag = jnp.dot(yt, wg_ref[...], preferred_element_type=F32)
            out = ag[:, :D_S5] * _sigmoid(ag[:, D_S5:])
            for hh in range(D_S5 // LANES):
                for cg in range(ncg):
                    ot_ref[hh, :, pl.ds(cg * sub * S5_L + t, sub, stride=S5_L), :] = (
                        out[cg * grp:(cg + 1) * grp, hh * LANES:(hh + 1) * LANES].reshape(nb, sub, LANES))
    o_ref[...] = jnp.concatenate([ot_ref[0], ot_ref[1]], axis=-1).astype(BF16)


def _s5(u, pt, q_mat, a_r, a_i, d_row, w_glu):
    nb, seq, _ = u.shape
    rows = nb * (S5_TOK // S5_L)
    nlb = S5_BSTATE // LANES
    assert 2 * S5_CB == LANES and D_S5 == 2 * LANES
    return pl.pallas_call(
        _s5_kernel,
        out_shape=jax.ShapeDtypeStruct((nb, seq, D_S5), BF16),
        grid=(seq // S5_TOK,),
        in_specs=[pl.BlockSpec((nb, S5_TOK, LANES), lambda i: (0, i, 0)),
                  pl.BlockSpec((nb, S5_TOK, LANES), lambda i: (0, i, 1)),
                  _resident(pt.shape), _resident(q_mat.shape), _resident(a_r.shape),
                  _resident(a_i.shape), _resident(d_row.shape), _resident(w_glu.shape)],
        out_specs=pl.BlockSpec((nb, S5_TOK, D_S5), lambda i: (0, i, 0)),
        scratch_shapes=[pltpu.VMEM((S5_NCB, nlb // 2, nb, LANES), F32),
                        pltpu.VMEM((S5_NCB, nlb // 2, nb, LANES), F32),
                        pltpu.VMEM((S5_NCB, rows, S5_BROW), F32),
                        pltpu.VMEM((S5_NCB, nlb, rows, LANES), F32),
                        pltpu.VMEM((D_S5 // LANES, nb, S5_TOK, LANES), F32)],
        compiler_params=pltpu.CompilerParams(
            dimension_semantics=("arbitrary",), vmem_limit_bytes=VMEM_LIMIT),
        name="s5",
    )(u, u, pt, q_mat, a_r, a_i, d_row, w_glu)


def _attn_bias_vec(rel_bias):
    rb = rel_bias.astype(F32) * LOG2E
    far_past, far_future = rb[:, 2 * MAX_REL:], rb[:, :1]
    n_const = N_LEFT * CHUNK - MAX_REL
    return jnp.concatenate([
        jnp.broadcast_to(far_past, (HEADS, n_const)),
        rb[:, ::-1],
        jnp.broadcast_to(far_future, (HEADS, ATT_W - n_const - 2 * MAX_REL - 1)),
        jnp.broadcast_to(far_past, (HEADS, ATT_PERIOD - ATT_W)),
    ], axis=1)


def _attn_build_bias(vec_ref, bm_ref):
    r = lax.broadcasted_iota(jnp.int32, (ATT_QB, ATT_W), 0)
    c = lax.broadcasted_iota(jnp.int32, (ATT_QB, ATT_W), 1)
    dchunk = r // CHUNK + N_LEFT - c // CHUNK
    ok = (dchunk >= 0) & (dchunk <= N_LEFT)
    for h in range(HEADS):
        base = jnp.broadcast_to(vec_ref[h:h + 1, :], (ATT_QB, ATT_PERIOD))
        toep = pltpu.roll(base, 0, 1, stride=1, stride_axis=0)[:, :ATT_W]
        bm_ref[h // 2, h % 2] = jnp.where(ok, toep, -1e30)


def _attn_kernel(q_ref, k_ref, v_ref, vec_ref, o_ref, bm_ref):
    seq = q_ref.shape[1]
    lane = lax.broadcasted_iota(jnp.int32, (1, LANES), 1)
    first = lane < HEAD_DIM

    @pl.when(pl.program_id(0) == 0)
    def _():
        _attn_build_bias(vec_ref, bm_ref)

    def block(q0, k0, width):
        for pr in range(D_ATT // LANES):
            cols = slice(pr * LANES, (pr + 1) * LANES)
            q2 = q_ref[0, pl.ds(q0, ATT_QB), cols]
            outs = []
            for hh in range(2):
                sel = first if hh == 0 else jnp.logical_not(first)
                qh = jnp.where(sel, q2, jnp.zeros_like(q2))
                for c0 in range(0, width, ATT_KT):
                    kw = k_ref[0, pl.ds(k0 + c0, ATT_KT), cols]
                    vw = v_ref[0, pl.ds(k0 + c0, ATT_KT), cols]
                    s = lax.dot_general(qh, kw, (((1,), (1,)), ((), ())),
                                        preferred_element_type=F32)
                    b0 = ATT_W - width + c0
                    s = s + bm_ref[pr, hh, :, b0:b0 + ATT_KT]
                    vh = jnp.where(sel, vw, jnp.ones_like(vw))
                    if c0 == 0:
                        mx = jnp.max(s, axis=-1, keepdims=True)
                        acc = jnp.dot(jnp.exp2(s - mx).astype(BF16), vh, preferred_element_type=F32)
                    else:
                        mx_new = jnp.maximum(mx, jnp.max(s, axis=-1, keepdims=True))
                        p = jnp.exp2(s - mx_new).astype(BF16)
                        acc = acc * jnp.exp2(mx - mx_new) + jnp.dot(p, vh, preferred_element_type=F32)
                        mx = mx_new
                outs.append(acc)
            num = jnp.where(first, outs[0], outs[1])
            den = pltpu.roll(jnp.where(first, outs[1], outs[0]), HEAD_DIM, axis=1)
            o_ref[0, pl.ds(q0, ATT_QB), cols] = (num / den).astype(BF16)

    n_short = N_LEFT * CHUNK // ATT_QB
    for qb in range(n_short):
        block(qb * ATT_QB, 0, (qb + 1) * ATT_QB)

    def body(qb, carry):
        q0 = pl.multiple_of(qb * ATT_QB, ATT_QB)
        k0 = pl.multiple_of(q0 - N_LEFT * CHUNK, ATT_QB)
        block(q0, k0, ATT_W)
        return carry
    lax.fori_loop(n_short, seq // ATT_QB, body, 0)


def _attn(q, k, v, bias_vec):
    nb, seq, _ = q.shape
    seqblk = pl.BlockSpec((1, seq, D_ATT), lambda b: (b, 0, 0))
    return pl.pallas_call(
        _attn_kernel,
        out_shape=jax.ShapeDtypeStruct((nb, seq, D_ATT), BF16),
        grid=(nb,),
        in_specs=[seqblk, seqblk, seqblk, _resident(bias_vec.shape)],
        out_specs=seqblk,
        scratch_shapes=[pltpu.VMEM((HEADS // 2, 2, ATT_QB, ATT_W), F32)],
        compiler_params=pltpu.CompilerParams(
            dimension_semantics=("arbitrary",), vmem_limit_bytes=VMEM_LIMIT),
        name="attn",
    )(q, k, v, bias_vec)


def _conv_taps(hc_ref, cw_ref, cb_ref, cbuf_ref, first_tile):
    nh = D_CONV // LANES
    tm = hc_ref.shape[0]

    @pl.when(first_tile)
    def _():
        cbuf_ref[:, :CONV_PAD, :] = jnp.zeros((nh, CONV_PAD, LANES), F32)
    for hh in range(nh):
        cbuf_ref[hh, CONV_PAD:, :] = hc_ref[:, hh * LANES:(hh + 1) * LANES]
    shift = CONV_PAD - (CONV_WIDTH - 1)
    halves = []
    for hh in range(nh):
        tiles = []
        for r0 in range(0, tm, CONV_TR):
            acc = jnp.zeros((CONV_TR, LANES), F32) + cb_ref[hh]
            for j in range(CONV_WIDTH):
                lo = r0 + shift + j
                acc = acc + cw_ref[hh, j:j + 1, :] * cbuf_ref[hh, lo:lo + CONV_TR, :]
            tiles.append(acc)
        halves.append(jnp.concatenate(tiles, axis=0))
    for hh in range(nh):
        cbuf_ref[hh, :CONV_PAD, :] = cbuf_ref[hh, tm:tm + CONV_PAD, :]
    return halves


def _conv_norm(halves, lg_ref, lb_ref):
    acc = jnp.concatenate(halves, axis=-1)
    mu = jnp.mean(acc, axis=-1, keepdims=True)
    cen = acc - mu
    var = jnp.mean(cen * cen, axis=-1, keepdims=True)
    y = (cen * lax.rsqrt(var + EPS)) * lg_ref[...] + lb_ref[...]
    return (y * _sigmoid(y)).astype(BF16)


def _after(v, zero_ref, width):
    bits = pltpu.bitcast(v, jnp.uint32)
    acc = bits[0:8, :]
    for r0 in range(8, v.shape[0], 8):
        acc = acc | bits[r0:r0 + 8, :]
    zero = pltpu.bitcast(acc[0:1, :] & zero_ref[...], F32)
    return jnp.tile(zero, (1, width // LANES)).astype(BF16)


def _merge_kernel(x_ref, s5_ref, at_ref, hc_ref, g_ref, wg_ref, bg_ref, ws_ref, wa_ref, wc_ref, wo_ref,
                  cw_ref, cb_ref, lg_ref, lb_ref, zero_ref, *refs, tiles_per_seq, cast_nblks):
    n_cast = len(cast_nblks)
    o_ref, cbuf_ref = refs[n_cast], refs[-1]
    _Casts.run_predicated(pl.program_id(0), cast_nblks, refs[:n_cast], refs[n_cast + 1:-1])
    halves = _conv_taps(hc_ref, cw_ref, cb_ref, cbuf_ref, pl.program_id(0) % tiles_per_seq == 0)
    _Casts.run_every_step(cast_nblks, refs[:n_cast], refs[n_cast + 1:-1])
    x = x_ref[...]
    h = _rmsnorm_bf16(x, g_ref[...])
    lhs = (h, h + _after(halves[0], zero_ref, D_MODEL), h + _after(halves[1], zero_ref, D_MODEL))
    branches = (lambda: s5_ref[...], lambda: at_ref[...], lambda: _conv_norm(halves, lg_ref, lb_ref))
    merged = jnp.zeros_like(x)
    for i, (branch, w_ref) in enumerate(zip(branches, (ws_ref, wa_ref, wc_ref))):
        cols = slice(i * D_MODEL, (i + 1) * D_MODEL)
        gcols = slice(N_SMALL + i * D_MODEL, N_SMALL + (i + 1) * D_MODEL)
        logits = jnp.dot(lhs[i], wg_ref[:, gcols], preferred_element_type=F32) + bg_ref[:, cols]
        y = jnp.dot(branch(), w_ref[...], preferred_element_type=F32)
        merged = merged + _sigmoid(logits) * y
    o_ref[...] = x + jnp.dot(merged.astype(BF16), wo_ref[...], preferred_element_type=F32)


def _merge(x, s5o, ato, hc, g, w_in, b_gate, w_s5, w_at, w_cv, w_out, w_dw, b_dw, ln_g, ln_b, seq,
           cast_items):
    n = x.shape[0]
    nh = D_CONV // LANES
    zero = jnp.zeros((1, LANES), jnp.uint32)
    assert seq % TM == 0
    steps = n // TM
    casts = _Casts(cast_items, steps)
    def tok(width):
        return pl.BlockSpec((TM, width), lambda i: (i, 0))
    return pl.pallas_call(
        functools.partial(_merge_kernel, tiles_per_seq=seq // TM, cast_nblks=tuple(casts.nblks)),
        out_shape=(jax.ShapeDtypeStruct((n, D_MODEL), F32), *casts.out_shape),
        grid=(steps,),
        in_specs=[tok(D_MODEL), tok(D_S5), tok(D_ATT), tok(D_CONV),
                  _resident((1, D_MODEL)), _resident(w_in.shape),
                  _resident((1, 3 * D_MODEL)), _resident((D_S5, D_MODEL)),
                  _resident((D_ATT, D_MODEL)), _resident((D_CONV, D_MODEL)),
                  _resident((D_MODEL, D_MODEL)), _resident((nh, CONV_WIDTH, LANES)),
                  _resident((nh, 1, LANES)), _resident((1, D_CONV)), _resident((1, D_CONV)),
                  _resident((1, LANES)), *casts.in_specs],
        out_specs=(tok(D_MODEL), *casts.out_specs),
        scratch_shapes=[pltpu.VMEM((nh, CONV_PAD + TM, LANES), F32)],
        compiler_params=pltpu.CompilerParams(
            dimension_semantics=("arbitrary",), vmem_limit_bytes=VMEM_LIMIT),
        name="merge",
    )(x, s5o, ato, hc, g, w_in, b_gate, w_s5, w_at, w_cv, w_out, w_dw, b_dw, ln_g, ln_b, zero,
      *casts.args)


def kernel(x, ffn1_norm, ffn1_w_up, ffn1_w_down, mix_norm, w_in, b_gate, s5_lambda_re, s5_lambda_im, s5_log_dt, s5_b_re, s5_b_im, s5_c_re, s5_c_im, s5_d, s5_w_glu, w_br_s5, attn_q_gain, attn_k_gain, attn_rel_bias, w_br_attn, conv_w_dw, conv_b_dw, conv_ln_g, conv_ln_b, w_br_conv, w_out, ffn2_norm, ffn2_w_up, ffn2_w_down):
    nb, seq, d = x.shape
    n = nb * seq
    depth = ffn1_norm.shape[0]
    xt = x.reshape(n, d)
    row = lambda v: v.reshape(1, -1).astype(F32)
    f1_up, f1_down, w_in_l = (ffn1_w_up[0].astype(BF16), ffn1_w_down[0].astype(BF16),
                              w_in[0].astype(BF16))
    for l in range(depth):
        xt = _ffn(xt, row(ffn1_norm[l]), f1_up, f1_down)

        qg2 = jnp.tile(row(attn_q_gain[l]), (1, LANES // HEAD_DIM))
        kg2 = jnp.tile(row(attn_k_gain[l]), (1, LANES // HEAD_DIM))
        own = (ffn2_w_up, ffn2_w_down, w_br_s5, w_br_attn, w_br_conv, w_out, s5_w_glu)
        u, q, k, v, hc, f2_up, f2_down, w_s5, w_at, w_cv, w_o, w_glu = _inproj(
            xt, row(mix_norm[l]), w_in_l, qg2, kg2, [(w, l) for w in own])

        pt, q_mat, a_r, a_i, d_row = _s5_tables(
            s5_lambda_re[l], s5_lambda_im[l], s5_log_dt[l], s5_b_re[l], s5_b_im[l],
            s5_c_re[l], s5_c_im[l], s5_d[l])
        s5o = _s5(u.reshape(nb, seq, D_S5), pt, q_mat, a_r, a_i, d_row, w_glu)

        ato = _attn(q.reshape(nb, seq, D_ATT), k.reshape(nb, seq, D_ATT),
                    v.reshape(nb, seq, D_ATT), _attn_bias_vec(attn_rel_bias[l]))

        nh = D_CONV // LANES
        w_dw = conv_w_dw[l].astype(F32).reshape(CONV_WIDTH, nh, LANES).transpose(1, 0, 2)
        nxt = (ffn1_w_up, ffn1_w_down, w_in) if l + 1 < depth else ()
        xt, *nxt_bf16 = _merge(
            xt, s5o.reshape(n, D_S5), ato.reshape(n, D_ATT), hc, row(mix_norm[l]), w_in_l,
            row(b_gate[l]), w_s5, w_at, w_cv, w_o, w_dw,
            conv_b_dw[l].astype(F32).reshape(nh, 1, LANES), row(conv_ln_g[l]), row(conv_ln_b[l]),
            seq, [(w, l + 1) for w in nxt])
        if nxt_bf16:
            f1_up, f1_down, w_in_l = nxt_bf16

        xt = _ffn(xt, row(ffn2_norm[l]), f2_up, f2_down)
    return xt.reshape(nb, seq, d)
```

```python
import functools
import math

import jax
import jax.numpy as jnp
from jax import lax
from jax.experimental import pallas as pl
from jax.experimental.pallas import tpu as pltpu

F32 = jnp.float32
BF16 = jnp.bfloat16

D_MODEL = 1024
D_S5 = 256
S5_GROUP = 16
S5_GROUPS = 16
S5_STATE = 64
D_ATT = 512
HEAD_DIM = 64
HEADS = 8
CHUNK = 64
N_LEFT = 8
MAX_REL = 128
D_CONV = 256
CONV_WIDTH = 31
D_FF = 2816
EPS = 1e-6
LOG2E = math.log2(math.e)
N_SMALL = D_S5 + 3 * D_ATT + 2 * D_CONV

LANES = 128
MXU_DIM = 256
VMEM_LIMIT = 56 * 1024 * 1024

TM = 512
TM_WIDE = 1024
CAST_MIN_ROWS = 16
CAST_ROWS = 64
S5_L = 8
S5_CB = LANES // 2
S5_GB = S5_CB // S5_GROUP
S5_NCB = D_S5 // S5_CB
S5_BROW = S5_L * S5_CB
S5_BSTATE = 2 * S5_GB * S5_STATE
S5_TOK = 256
ATT_QB = 4 * CHUNK
ATT_W = ATT_QB + N_LEFT * CHUNK
ATT_KT = MXU_DIM
ATT_PERIOD = ATT_QB + ATT_W
CONV_TR = 128
CONV_PAD = 32


def _resident(shape):
    nd = len(shape)
    return pl.BlockSpec(shape, lambda *_: (0,) * nd, pipeline_mode=pl.Buffered(1))


def _rmsnorm_bf16(x, g):
    ms = jnp.mean(x * x, axis=-1, keepdims=True)
    return ((x * lax.rsqrt(ms + EPS)) * g).astype(BF16)


def _sigmoid(x):
    return jax.nn.sigmoid(x)


class _Casts:
    def __init__(self, items, steps):
        self.in_specs, self.out_specs, self.out_shape, self.args, self.nblks = [], [], [], [], []
        for stacked, layer in items:
            _, rows, cols = stacked.shape
            every_step = rows % CAST_MIN_ROWS == 0 and rows // CAST_MIN_ROWS <= steps
            rb = CAST_MIN_ROWS if every_step else CAST_ROWS
            nblk = rows // rb
            assert nblk * rb == rows and nblk <= steps
            self.nblks.append(None if every_step else nblk)
            self.in_specs.append(pl.BlockSpec(
                (None, rb, cols),
                lambda i, layer=layer, nblk=nblk: (layer, jnp.minimum(i, nblk - 1), 0)))
            self.out_specs.append(pl.BlockSpec(
                (rb, cols), lambda i, nblk=nblk: (jnp.minimum(i, nblk - 1), 0)))
            self.out_shape.append(jax.ShapeDtypeStruct((rows, cols), BF16))
            self.args.append(stacked)

    @staticmethod
    def run_predicated(step, nblks, src_refs, dst_refs):
        for nblk in sorted({n for n in nblks if n is not None}):
            @pl.when(step < nblk)
            def _():
                for n, src, dst in zip(nblks, src_refs, dst_refs):
                    if n == nblk:
                        dst[...] = src[...].astype(BF16)

    @staticmethod
    def run_every_step(nblks, src_refs, dst_refs):
        for n, src, dst in zip(nblks, src_refs, dst_refs):
            if n is None:
                dst[...] = src[...].astype(BF16)


def _ffn_kernel(x_ref, g_ref, wa_ref, wb_ref, wd_ref, o_ref, *, bounds):
    x = x_ref[...]
    h = _rmsnorm_bf16(x, g_ref[...])
    y = jnp.zeros_like(x)
    for f0, f1 in zip(bounds[:-1], bounds[1:]):
        a = jnp.dot(h, wa_ref[:, f0:f1], preferred_element_type=F32)
        b = jnp.dot(h, wb_ref[:, f0:f1], preferred_element_type=F32)
        act = ((a * _sigmoid(a)) * b).astype(BF16)
        y = y + jnp.dot(act, wd_ref[f0:f1, :], preferred_element_type=F32)
    o_ref[...] = x + 0.5 * y


def _ffn_bounds():
    tiles = D_FF // MXU_DIM
    assert tiles * MXU_DIM == D_FF
    return (0, (tiles + 1) // 2 * MXU_DIM, D_FF)


def _ffn(x, g, w_up, w_down):
    n = x.shape[0]
    tok = pl.BlockSpec((TM_WIDE, D_MODEL), lambda i: (i, 0))
    def up_half(j):
        return pl.BlockSpec((D_MODEL, D_FF), lambda i: (0, j), pipeline_mode=pl.Buffered(1))
    return pl.pallas_call(
        functools.partial(_ffn_kernel, bounds=_ffn_bounds()),
        out_shape=jax.ShapeDtypeStruct((n, D_MODEL), F32),
        grid=(n // TM_WIDE,),
        in_specs=[tok, _resident((1, D_MODEL)), up_half(0), up_half(1), _resident((D_FF, D_MODEL))],
        out_specs=tok,
        compiler_params=pltpu.CompilerParams(
            dimension_semantics=("arbitrary",), vmem_limit_bytes=VMEM_LIMIT),
        name="ffn",
    )(x, g, w_up, w_up, w_down)


def _head_pair_norm(x2, gain2):
    lane = lax.broadcasted_iota(jnp.int32, (1, LANES), 1)
    first = lane < HEAD_DIM
    sq = x2 * x2
    s_a = jnp.sum(jnp.where(first, sq, 0.0), axis=-1, keepdims=True)
    s_b = jnp.sum(jnp.where(first, 0.0, sq), axis=-1, keepdims=True)
    ms = jnp.where(first, s_a, s_b) * (1.0 / HEAD_DIM)
    return (x2 * lax.rsqrt(ms + EPS)) * gain2


def _inproj_kernel(x_ref, g_ref, w_ref, qg_ref, kg_ref, *refs, cast_nblks):
    n_cast = len(cast_nblks)
    u_ref, q_ref, kt_ref, v_ref, hc_ref = refs[n_cast:n_cast + 5]
    _Casts.run_predicated(pl.program_id(0), cast_nblks, refs[:n_cast], refs[n_cast + 5:])
    _Casts.run_every_step(cast_nblks, refs[:n_cast], refs[n_cast + 5:])
    h = _rmsnorm_bf16(x_ref[...], g_ref[...])
    proj = jnp.dot(h, w_ref[...], preferred_element_type=F32)
    u_ref[...] = proj[:, :D_S5]
    q0, k0, v0, z0 = D_S5, D_S5 + D_ATT, D_S5 + 2 * D_ATT, D_S5 + 3 * D_ATT
    scale = HEAD_DIM ** -0.5 * LOG2E
    for p in range(D_ATT // LANES):
        lo = p * LANES
        qn = _head_pair_norm(proj[:, q0 + lo:q0 + lo + LANES], qg_ref[...])
        kn = _head_pair_norm(proj[:, k0 + lo:k0 + lo + LANES], kg_ref[...])
        q_ref[:, lo:lo + LANES] = (qn * scale).astype(BF16)
        kt_ref[lo:lo + LANES, :] = kn.T.astype(BF16)
    v_ref[...] = proj[:, v0:v0 + D_ATT].astype(BF16)
    a = proj[:, z0:z0 + D_CONV]
    gt = proj[:, z0 + D_CONV:z0 + 2 * D_CONV]
    hc_ref[...] = a * _sigmoid(gt)


def _inproj(x, g, w_in, qg2, kg2, seq, cast_items):
    n = x.shape[0]
    steps = n // TM_WIDE
    tiles_per_seq = seq // TM_WIDE
    assert tiles_per_seq * TM_WIDE == seq
    casts = _Casts(cast_items, steps)
    def tok(width):
        return pl.BlockSpec((TM_WIDE, width), lambda i: (i, 0))
    w_small = pl.BlockSpec((D_MODEL, N_SMALL), lambda i: (0, 0), pipeline_mode=pl.Buffered(1))
    kt_spec = pl.BlockSpec((None, D_ATT, TM_WIDE),
                           lambda i: (i // tiles_per_seq, 0, i % tiles_per_seq))
    return pl.pallas_call(
        functools.partial(_inproj_kernel, cast_nblks=tuple(casts.nblks)),
        out_shape=(jax.ShapeDtypeStruct((n, D_S5), F32),
                   jax.ShapeDtypeStruct((n, D_ATT), BF16),
                   jax.ShapeDtypeStruct((n // seq, D_ATT, seq), BF16),
                   jax.ShapeDtypeStruct((n, D_ATT), BF16),
                   jax.ShapeDtypeStruct((n, D_CONV), F32), *casts.out_shape),
        grid=(steps,),
        in_specs=[tok(D_MODEL), _resident((1, D_MODEL)), w_small,
                  _resident((1, LANES)), _resident((1, LANES)), *casts.in_specs],
        out_specs=(tok(D_S5), tok(D_ATT), kt_spec, tok(D_ATT), tok(D_CONV), *casts.out_specs),
        compiler_params=pltpu.CompilerParams(
            dimension_semantics=("arbitrary",), vmem_limit_bytes=VMEM_LIMIT),
        name="inproj",
    )(x, g, w_in, qg2, kg2, *casts.args)


def _s5_tables(lambda_re, lambda_im, log_dt, b_re, b_im, c_re, c_im, d_skip):
    lr = jnp.minimum(lambda_re.astype(F32), -1e-4)
    li = lambda_im.astype(F32)
    dt = jnp.exp(log_dt.astype(F32))[:, None]
    mag = jnp.exp(lr * dt)
    ar = mag * jnp.cos(li * dt)
    ai = mag * jnp.sin(li * dt)
    den = lr * lr + li * li
    coef_r = ((ar - 1.0) * lr + ai * li) / den
    coef_i = (ai * lr - (ar - 1.0) * li) / den
    br = b_re.astype(F32)
    bi = b_im.astype(F32)
    bbar_r = coef_r[..., None] * br - coef_i[..., None] * bi
    bbar_i = coef_r[..., None] * bi + coef_i[..., None] * br
    cr = c_re.astype(F32)
    ci = c_im.astype(F32)

    pr, pi = [jnp.ones_like(ar)], [jnp.zeros_like(ai)]
    for _ in range(S5_L):
        r, i = pr[-1], pi[-1]
        pr.append(r * ar - i * ai)
        pi.append(r * ai + i * ar)
    pr = jnp.stack(pr)
    pi = jnp.stack(pi)

    ncb, gb = S5_NCB, S5_GB
    pw_r = pr[S5_L - 1::-1]
    pw_i = pi[S5_L - 1::-1]
    e_r = pw_r[..., None] * bbar_r[None] - pw_i[..., None] * bbar_i[None]
    e_i = pw_r[..., None] * bbar_i[None] + pw_i[..., None] * bbar_r[None]
    p_blk = jnp.stack([jnp.transpose(e_r, (0, 1, 3, 2)), jnp.transpose(e_i, (0, 1, 3, 2))], axis=3)
    p_blk = p_blk.reshape(S5_L, ncb, gb * S5_GROUP, 2 * S5_STATE)
    p_blk = jnp.transpose(p_blk, (1, 0, 2, 3)).reshape(ncb, S5_BROW, 2 * S5_STATE)

    m_r = pr[:S5_L, :, None, :] * cr[None] - pi[:S5_L, :, None, :] * ci[None]
    m_i = pr[:S5_L, :, None, :] * ci[None] + pi[:S5_L, :, None, :] * cr[None]
    kern = (jnp.sum(m_r[:, :, :, :, None] * bbar_r[None, :, None, :, :], axis=3)
            - jnp.sum(m_i[:, :, :, :, None] * bbar_i[None, :, None, :, :], axis=3))
    t_blk = jnp.transpose(kern, (0, 1, 3, 2)).reshape(S5_L, ncb, S5_CB, S5_GROUP)
    t_blk = jnp.transpose(t_blk, (1, 0, 2, 3))

    q_r = pr[1:, :, None, :] * cr[None] - pi[1:, :, None, :] * ci[None]
    q_i = -(pr[1:, :, None, :] * ci[None] + pi[1:, :, None, :] * cr[None])
    q_blk = jnp.stack([q_r, q_i], axis=0).reshape(2, S5_L, ncb, gb, S5_GROUP, S5_STATE)
    q_blk = jnp.transpose(q_blk, (2, 1, 4, 0, 3, 5))
    q_blk = q_blk.reshape(ncb, S5_L, S5_GROUP, S5_BSTATE)

    half_blocks = S5_BSTATE // 2 // LANES
    a_r = pr[S5_L].reshape(ncb, half_blocks, 1, LANES)
    a_i = pi[S5_L].reshape(ncb, half_blocks, 1, LANES)
    d_row = jnp.tile(d_skip.astype(F32).reshape(ncb, 1, S5_CB), (1, 1, S5_L))
    pt, q_mat = _s5_expand(p_blk.astype(BF16), t_blk.astype(BF16), q_blk.astype(BF16))
    return pt, q_mat, a_r, a_i, d_row


def _s5_expand_kernel(p_ref, t_ref, q_ref, pt_ref, qm_ref):
    def iota(shape, dim):
        return lax.broadcasted_iota(jnp.int32, shape, dim)

    def onehot(rows, cols, row_key, col_key):
        r, c = iota((rows, cols), 0), iota((rows, cols), 1)
        return jnp.where(row_key(r) == col_key(c), 1.0, 0.0)

    half = S5_BSTATE // 2
    rep = onehot(2 * S5_STATE, S5_BSTATE, lambda r: r,
                 lambda c: (c // half) * S5_STATE + c % S5_STATE).astype(BF16)
    same = onehot(S5_CB, S5_BSTATE, lambda r: r // S5_GROUP, lambda c: (c % half) // S5_STATE)
    rep_c = onehot(S5_GROUP, S5_CB, lambda r: r, lambda c: c % S5_GROUP).astype(BF16)
    same_g = onehot(S5_CB, S5_CB, lambda r: r // S5_GROUP, lambda c: c // S5_GROUP)
    same_q = onehot(S5_BSTATE, S5_CB, lambda r: (r % half) // S5_STATE, lambda c: c // S5_GROUP)
    lag_blk = [(jnp.dot(t_ref[d], rep_c, preferred_element_type=F32) * same_g).astype(BF16)
               for d in range(S5_L)]
    zero_blk = jnp.zeros((S5_CB, S5_CB), BF16)
    for s in range(S5_L):
        rows = slice(s * S5_CB, (s + 1) * S5_CB)
        p_full = jnp.dot(p_ref[rows, :], rep, preferred_element_type=F32) * same
        pt_ref[rows, :S5_BSTATE] = p_full.astype(BF16)
        for t in range(S5_L):
            cols = slice(S5_BSTATE + t * S5_CB, S5_BSTATE + (t + 1) * S5_CB)
            pt_ref[rows, cols] = lag_blk[t - s] if t >= s else zero_blk
    for t in range(S5_L):
        q_full = lax.dot_general(q_ref[t], rep_c, (((0,), (0,)), ((), ())),
                                 preferred_element_type=F32) * same_q
        qm_ref[:, t * S5_CB:(t + 1) * S5_CB] = q_full.astype(BF16)


def _s5_expand(p_blk, t_blk, q_blk):
    def blk(shape):
        return pl.BlockSpec((None,) + shape, lambda cb: (cb,) + (0,) * len(shape))
    return pl.pallas_call(
        _s5_expand_kernel,
        out_shape=(jax.ShapeDtypeStruct((S5_NCB, S5_BROW, S5_BSTATE + S5_BROW), BF16),
                   jax.ShapeDtypeStruct((S5_NCB, S5_BSTATE, S5_BROW), BF16)),
        grid=(S5_NCB,),
        in_specs=[blk((S5_BROW, 2 * S5_STATE)), blk((S5_L, S5_CB, S5_GROUP)),
                  blk((S5_L, S5_GROUP, S5_BSTATE))],
        out_specs=(blk((S5_BROW, S5_BSTATE + S5_BROW)), blk((S5_BSTATE, S5_BROW))),
        compiler_params=pltpu.CompilerParams(
            dimension_semantics=("arbitrary",), vmem_limit_bytes=VMEM_LIMIT),
        name="s5_tables",
    )(p_blk, t_blk, q_blk)


def _s5_kernel(ua_ref, ub_ref, pt_ref, q_ref, ar_ref, ai_ref, d_ref, wg_ref, o_ref,
               sr_ref, si_ref, u8_ref, es_ref, ot_ref):
    nb = ua_ref.shape[0]
    ncl = S5_TOK // S5_L
    nlb = S5_BSTATE // LANES
    hlb = nlb // 2
    u_refs = (ua_ref, ub_ref)
    low = lax.broadcasted_iota(jnp.int32, (1, LANES), 1) < S5_CB

    def interleave(a, b):
        return (jnp.where(low, a, pltpu.roll(b, S5_CB, axis=1)),
                jnp.where(low, pltpu.roll(a, S5_CB, axis=1), b))

    @pl.when(pl.program_id(0) == 0)
    def _():
        sr_ref[...] = jnp.zeros_like(sr_ref)
        si_ref[...] = jnp.zeros_like(si_ref)

    sub = 8
    ncg = ncl // sub
    grp = nb * sub
    for hh, u_ref in enumerate(u_refs):
        for t2 in range(S5_L // 2):
            for cg in range(ncg):
                tok = [u_ref[:, pl.ds(cg * sub * S5_L + 2 * t2 + k, sub, stride=S5_L), :]
                       .reshape(grp, LANES) for k in range(2)]
                for k, blk in enumerate(interleave(*tok)):
                    u8_ref[2 * hh + k, cg * grp:(cg + 1) * grp, t2 * LANES:(t2 + 1) * LANES] = blk

    for cb in range(S5_NCB):
        e = jnp.dot(u8_ref[cb].astype(BF16), pt_ref[cb, :, :S5_BSTATE],
                    preferred_element_type=F32)
        for j in range(nlb):
            es_ref[cb, j] = e[:, j * LANES:(j + 1) * LANES]

    a_r = ar_ref[...]
    a_i = ai_ref[...]
    sr = sr_ref[...]
    si = si_ref[...]
    for cl in range(ncl):
        step = pl.ds((cl // sub) * grp + cl % sub, nb, stride=sub)
        e_r = es_ref[:, :hlb, step, :]
        e_i = es_ref[:, hlb:, step, :]
        es_ref[:, :hlb, step, :] = sr
        es_ref[:, hlb:, step, :] = si
        sr, si = a_r * sr - a_i * si + e_r, a_r * si + a_i * sr + e_i
    sr_ref[...] = sr
    si_ref[...] = si

    ys = []
    for cb in range(S5_NCB):
        u8 = u8_ref[cb]
        s_all = jnp.concatenate([es_ref[cb, j] for j in range(nlb)], axis=1).astype(BF16)
        y = (jnp.dot(u8.astype(BF16), pt_ref[cb, :, S5_BSTATE:], preferred_element_type=F32)
             + jnp.dot(s_all, q_ref[cb], preferred_element_type=F32) + d_ref[cb] * u8)
        ys.append(jax.nn.gelu(y))
    for t2 in range(S5_L // 2):
        lanes = slice(t2 * LANES, (t2 + 1) * LANES)
        halves = [interleave(ys[2 * hh][:, lanes], ys[2 * hh + 1][:, lanes]) for hh in range(2)]
        for k in range(2):
            t = 2 * t2 + k
            yt = jnp.concatenate([halves[0][k], halves[1][k]], axis=1).astype(BF16)
            ag = jnp.dot(yt, wg_ref[...], preferred_element_type=F32)
            out = ag[:, :D_S5] * _sigmoid(ag[:, D_S5:])
            for hh in range(D_S5 // LANES):
                for cg in range(ncg):
                    ot_ref[hh, :, pl.ds(cg * sub * S5_L + t, sub, stride=S5_L), :] = (
                        out[cg * grp:(cg + 1) * grp, hh * LANES:(hh + 1) * LANES].reshape(nb, sub, LANES))
    o_ref[...] = jnp.concatenate([ot_ref[0], ot_ref[1]], axis=-1).astype(BF16)


def _s5(u, pt, q_mat, a_r, a_i, d_row, w_glu):
    nb, seq, _ = u.shape
    rows = nb * (S5_TOK // S5_L)
    nlb = S5_BSTATE // LANES
    assert 2 * S5_CB == LANES and D_S5 == 2 * LANES
    return pl.pallas_call(
        _s5_kernel,
        out_shape=jax.ShapeDtypeStruct((nb, seq, D_S5), BF16),
        grid=(seq // S5_TOK,),
        in_specs=[pl.BlockSpec((nb, S5_TOK, LANES), lambda i: (0, i, 0)),
                  pl.BlockSpec((nb, S5_TOK, LANES), lambda i: (0, i, 1)),
                  _resident(pt.shape), _resident(q_mat.shape), _resident(a_r.shape),
                  _resident(a_i.shape), _resident(d_row.shape), _resident(w_glu.shape)],
        out_specs=pl.BlockSpec((nb, S5_TOK, D_S5), lambda i: (0, i, 0)),
        scratch_shapes=[pltpu.VMEM((S5_NCB, nlb // 2, nb, LANES), F32),
                        pltpu.VMEM((S5_NCB, nlb // 2, nb, LANES), F32),
                        pltpu.VMEM((S5_NCB, rows, S5_BROW), F32),
                        pltpu.VMEM((S5_NCB, nlb, rows, LANES), F32),
                        pltpu.VMEM((D_S5 // LANES, nb, S5_TOK, LANES), F32)],
        compiler_params=pltpu.CompilerParams(
            dimension_semantics=("arbitrary",), vmem_limit_bytes=VMEM_LIMIT),
        name="s5",
    )(u, u, pt, q_mat, a_r, a_i, d_row, w_glu)


def _attn_bias_vec(rel_bias):
    rb = rel_bias.astype(F32) * LOG2E
    far_past, far_future = rb[:, 2 * MAX_REL:], rb[:, :1]
    n_const = N_LEFT * CHUNK - MAX_REL
    return jnp.concatenate([
        jnp.broadcast_to(far_past, (HEADS, n_const)),
        rb[:, ::-1],
        jnp.broadcast_to(far_future, (HEADS, ATT_W - n_const - 2 * MAX_REL - 1)),
        jnp.broadcast_to(far_past, (HEADS, ATT_PERIOD - ATT_W)),
    ], axis=1)


def _attn_build_bias(vec_ref, bm_ref):
    r = lax.broadcasted_iota(jnp.int32, (ATT_QB, ATT_W), 0)
    c = lax.broadcasted_iota(jnp.int32, (ATT_QB, ATT_W), 1)
    dchunk = r // CHUNK + N_LEFT - c // CHUNK
    ok = (dchunk >= 0) & (dchunk <= N_LEFT)
    for h in range(HEADS):
        base = jnp.broadcast_to(vec_ref[h:h + 1, :], (ATT_QB, ATT_PERIOD))
        toep = pltpu.roll(base, 0, 1, stride=1, stride_axis=0)[:, :ATT_W]
        bm_ref[h // 2, h % 2] = jnp.where(ok, toep, -1e30)


def _attn_kernel(q_ref, kt_ref, v_ref, vec_ref, o_ref, bm_ref):
    seq = q_ref.shape[1]
    lane = lax.broadcasted_iota(jnp.int32, (1, LANES), 1)
    first = lane < HEAD_DIM

    @pl.when(pl.program_id(0) == 0)
    def _():
        _attn_build_bias(vec_ref, bm_ref)

    def block(q0, k0, width):
        for pr in range(D_ATT // LANES):
            cols = slice(pr * LANES, (pr + 1) * LANES)
            q2 = q_ref[0, pl.ds(q0, ATT_QB), cols]
            outs = []
            for hh in range(2):
                sel = first if hh == 0 else jnp.logical_not(first)
                qh = jnp.where(sel, q2, jnp.zeros_like(q2))
                for c0 in range(0, width, ATT_KT):
                    kt = kt_ref[0, cols, pl.ds(k0 + c0, ATT_KT)]
                    vw = v_ref[0, pl.ds(k0 + c0, ATT_KT), cols]
                    s = jnp.dot(qh, kt, preferred_element_type=F32)
                    b0 = ATT_W - width + c0
                    s = s + bm_ref[pr, hh, :, b0:b0 + ATT_KT]
                    vh = jnp.where(sel, vw, jnp.ones_like(vw))
                    if c0 == 0:
                        mx = jnp.max(s, axis=-1, keepdims=True)
                        acc = jnp.dot(jnp.exp2(s - mx).astype(BF16), vh, preferred_element_type=F32)
                    else:
                        mx_new = jnp.maximum(mx, jnp.max(s, axis=-1, keepdims=True))
                        p = jnp.exp2(s - mx_new).astype(BF16)
                        acc = acc * jnp.exp2(mx - mx_new) + jnp.dot(p, vh, preferred_element_type=F32)
                        mx = mx_new
                outs.append(acc)
            num = jnp.where(first, outs[0], outs[1])
            den = pltpu.roll(jnp.where(first, outs[1], outs[0]), HEAD_DIM, axis=1)
            o_ref[0, pl.ds(q0, ATT_QB), cols] = (num / den).astype(BF16)

    n_short = N_LEFT * CHUNK // ATT_QB
    for qb in range(n_short):
        block(qb * ATT_QB, 0, (qb + 1) * ATT_QB)

    def body(qb, carry):
        q0 = pl.multiple_of(qb * ATT_QB, ATT_QB)
        k0 = pl.multiple_of(q0 - N_LEFT * CHUNK, ATT_QB)
        block(q0, k0, ATT_W)
        return carry
    lax.fori_loop(n_short, seq // ATT_QB, body, 0)


def _attn(q, kt, v, bias_vec):
    nb, seq, _ = q.shape
    seqblk = pl.BlockSpec((1, seq, D_ATT), lambda b: (b, 0, 0))
    return pl.pallas_call(
        _attn_kernel,
        out_shape=jax.ShapeDtypeStruct((nb, seq, D_ATT), BF16),
        grid=(nb,),
        in_specs=[seqblk, pl.BlockSpec((1, D_ATT, seq), lambda b: (b, 0, 0)), seqblk,
                  _resident(bias_vec.shape)],
        out_specs=seqblk,
        scratch_shapes=[pltpu.VMEM((HEADS // 2, 2, ATT_QB, ATT_W), F32)],
        compiler_params=pltpu.CompilerParams(
            dimension_semantics=("arbitrary",), vmem_limit_bytes=VMEM_LIMIT),
        name="attn",
    )(q, kt, v, bias_vec)


def _conv_taps(hc_ref, cw_ref, cb_ref, cbuf_ref, first_tile):
    nh = D_CONV // LANES
    tm = hc_ref.shape[0]

    @pl.when(first_tile)
    def _():
        cbuf_ref[:, :CONV_PAD, :] = jnp.zeros((nh, CONV_PAD, LANES), F32)
    for hh in range(nh):
        cbuf_ref[hh, CONV_PAD:, :] = hc_ref[:, hh * LANES:(hh + 1) * LANES]
    shift = CONV_PAD - (CONV_WIDTH - 1)
    halves = []
    for hh in range(nh):
        tiles = []
        for r0 in range(0, tm, CONV_TR):
            acc = jnp.zeros((CONV_TR, LANES), F32) + cb_ref[hh]
            for j in range(CONV_WIDTH):
                lo = r0 + shift + j
                acc = acc + cw_ref[hh, j:j + 1, :] * cbuf_ref[hh, lo:lo + CONV_TR, :]
            tiles.append(acc)
        halves.append(jnp.concatenate(tiles, axis=0))
    for hh in range(nh):
        cbuf_ref[hh, :CONV_PAD, :] = cbuf_ref[hh, tm:tm + CONV_PAD, :]
    return halves


def _conv_norm(halves, lg_ref, lb_ref):
    acc = jnp.concatenate(halves, axis=-1)
    mu = jnp.mean(acc, axis=-1, keepdims=True)
    cen = acc - mu
    var = jnp.mean(cen * cen, axis=-1, keepdims=True)
    y = (cen * lax.rsqrt(var + EPS)) * lg_ref[...] + lb_ref[...]
    return (y * _sigmoid(y)).astype(BF16)


def _after(v, zero_ref, width):
    bits = pltpu.bitcast(v, jnp.uint32)
    acc = bits[0:8, :]
    for r0 in range(8, v.shape[0], 8):
        acc = acc | bits[r0:r0 + 8, :]
    zero = pltpu.bitcast(acc[0:1, :] & zero_ref[...], F32)
    return jnp.tile(zero, (1, width // LANES)).astype(BF16)


def _merge_kernel(x_ref, s5_ref, at_ref, hc_ref, g_ref, wg_ref, bg_ref, ws_ref, wa_ref, wc_ref, wo_ref,
                  cw_ref, cb_ref, lg_ref, lb_ref, zero_ref, *refs, tiles_per_seq, cast_nblks):
    n_cast = len(cast_nblks)
    o_ref, cbuf_ref = refs[n_cast], refs[-1]
    _Casts.run_predicated(pl.program_id(0), cast_nblks, refs[:n_cast], refs[n_cast + 1:-1])
    halves = _conv_taps(hc_ref, cw_ref, cb_ref, cbuf_ref, pl.program_id(0) % tiles_per_seq == 0)
    _Casts.run_every_step(cast_nblks, refs[:n_cast], refs[n_cast + 1:-1])
    x = x_ref[...]
    h = _rmsnorm_bf16(x, g_ref[...])
    lhs = (h, h + _after(halves[0], zero_ref, D_MODEL), h + _after(halves[1], zero_ref, D_MODEL))
    branches = (lambda: s5_ref[...], lambda: at_ref[...], lambda: _conv_norm(halves, lg_ref, lb_ref))
    merged = jnp.zeros_like(x)
    for i, (branch, w_ref) in enumerate(zip(branches, (ws_ref, wa_ref, wc_ref))):
        cols = slice(i * D_MODEL, (i + 1) * D_MODEL)
        gcols = slice(N_SMALL + i * D_MODEL, N_SMALL + (i + 1) * D_MODEL)
        logits = jnp.dot(lhs[i], wg_ref[:, gcols], preferred_element_type=F32) + bg_ref[:, cols]
        y = jnp.dot(branch(), w_ref[...], preferred_element_type=F32)
        merged = merged + _sigmoid(logits) * y
    o_ref[...] = x + jnp.dot(merged.astype(BF16), wo_ref[...], preferred_element_type=F32)


def _merge(x, s5o, ato, hc, g, w_in, b_gate, w_s5, w_at, w_cv, w_out, w_dw, b_dw, ln_g, ln_b, seq,
           cast_items):
    n = x.shape[0]
    nh = D_CONV // LANES
    zero = jnp.zeros((1, LANES), jnp.uint32)
    assert seq % TM == 0
    steps = n // TM
    casts = _Casts(cast_items, steps)
    def tok(width):
        return pl.BlockSpec((TM, width), lambda i: (i, 0))
    return pl.pallas_call(
        functools.partial(_merge_kernel, tiles_per_seq=seq // TM, cast_nblks=tuple(casts.nblks)),
        out_shape=(jax.ShapeDtypeStruct((n, D_MODEL), F32), *casts.out_shape),
        grid=(steps,),
        in_specs=[tok(D_MODEL), tok(D_S5), tok(D_ATT), tok(D_CONV),
                  _resident((1, D_MODEL)), _resident(w_in.shape),
                  _resident((1, 3 * D_MODEL)), _resident((D_S5, D_MODEL)),
                  _resident((D_ATT, D_MODEL)), _resident((D_CONV, D_MODEL)),
                  _resident((D_MODEL, D_MODEL)), _resident((nh, CONV_WIDTH, LANES)),
                  _resident((nh, 1, LANES)), _resident((1, D_CONV)), _resident((1, D_CONV)),
                  _resident((1, LANES)), *casts.in_specs],
        out_specs=(tok(D_MODEL), *casts.out_specs),
        scratch_shapes=[pltpu.VMEM((nh, CONV_PAD + TM, LANES), F32)],
        compiler_params=pltpu.CompilerParams(
            dimension_semantics=("arbitrary",), vmem_limit_bytes=VMEM_LIMIT),
        name="merge",
    )(x, s5o, ato, hc, g, w_in, b_gate, w_s5, w_at, w_cv, w_out, w_dw, b_dw, ln_g, ln_b, zero,
      *casts.args)


def kernel(x, ffn1_norm, ffn1_w_up, ffn1_w_down, mix_norm, w_in, b_gate, s5_lambda_re, s5_lambda_im, s5_log_dt, s5_b_re, s5_b_im, s5_c_re, s5_c_im, s5_d, s5_w_glu, w_br_s5, attn_q_gain, attn_k_gain, attn_rel_bias, w_br_attn, conv_w_dw, conv_b_dw, conv_ln_g, conv_ln_b, w_br_conv, w_out, ffn2_norm, ffn2_w_up, ffn2_w_down):
    nb, seq, d = x.shape
    n = nb * seq
    depth = ffn1_norm.shape[0]
    xt = x.reshape(n, d)
    row = lambda v: v.reshape(1, -1).astype(F32)
    f1_up, f1_down, w_in_l = (ffn1_w_up[0].astype(BF16), ffn1_w_down[0].astype(BF16),
                              w_in[0].astype(BF16))
    for l in range(depth):
        xt = _ffn(xt, row(ffn1_norm[l]), f1_up, f1_down)

        qg2 = jnp.tile(row(attn_q_gain[l]), (1, LANES // HEAD_DIM))
        kg2 = jnp.tile(row(attn_k_gain[l]), (1, LANES // HEAD_DIM))
        own = (ffn2_w_up, ffn2_w_down, w_br_s5, w_br_attn, w_br_conv, w_out, s5_w_glu)
        u, q, kt, v, hc, f2_up, f2_down, w_s5, w_at, w_cv, w_o, w_glu = _inproj(
            xt, row(mix_norm[l]), w_in_l, qg2, kg2, seq, [(w, l) for w in own])

        pt, q_mat, a_r, a_i, d_row = _s5_tables(
            s5_lambda_re[l], s5_lambda_im[l], s5_log_dt[l], s5_b_re[l], s5_b_im[l],
            s5_c_re[l], s5_c_im[l], s5_d[l])
        s5o = _s5(u.reshape(nb, seq, D_S5), pt, q_mat, a_r, a_i, d_row, w_glu)

        ato = _attn(q.reshape(nb, seq, D_ATT), kt, v.reshape(nb, seq, D_ATT),
                    _attn_bias_vec(attn_rel_bias[l]))

        nh = D_CONV // LANES
        w_dw = conv_w_dw[l].astype(F32).reshape(CONV_WIDTH, nh, LANES).transpose(1, 0, 2)
        nxt = (ffn1_w_up, ffn1_w_down, w_in) if l + 1 < depth else ()
        xt, *nxt_bf16 = _merge(
            xt, s5o.reshape(n, D_S5), ato.reshape(n, D_ATT), hc, row(mix_norm[l]), w_in_l,
            row(b_gate[l]), w_s5, w_at, w_cv, w_o, w_dw,
            conv_b_dw[l].astype(F32).reshape(nh, 1, LANES), row(conv_ln_g[l]), row(conv_ln_b[l]),
            seq, [(w, l + 1) for w in nxt])
        if nxt_bf16:
            f1_up, f1_down, w_in_l = nxt_bf16

        xt = _ffn(xt, row(ffn2_norm[l]), f2_up, f2_down)
    return xt.reshape(nb, seq, d)
```

```python
import functools
import math

import jax
import jax.numpy as jnp
from jax import lax
from jax.experimental import pallas as pl
from jax.experimental.pallas import tpu as pltpu

F32 = jnp.float32
BF16 = jnp.bfloat16

D_MODEL = 1024
D_S5 = 256
S5_GROUP = 16
S5_GROUPS = 16
S5_STATE = 64
D_ATT = 512
HEAD_DIM = 64
HEADS = 8
CHUNK = 64
N_LEFT = 8
MAX_REL = 128
D_CONV = 256
CONV_WIDTH = 31
D_FF = 2816
EPS = 1e-6
LOG2E = math.log2(math.e)
N_SMALL = D_S5 + 3 * D_ATT + 2 * D_CONV

LANES = 128
MXU_DIM = 256
VMEM_LIMIT = 56 * 1024 * 1024

TM = 1024
TM_WIDE = 1024
CAST_MIN_ROWS = 16
CAST_ROWS = 64
S5_L = 8
S5_CB = LANES // 2
S5_GB = S5_CB // S5_GROUP
S5_NCB = D_S5 // S5_CB
S5_BROW = S5_L * S5_CB
S5_BSTATE = 2 * S5_GB * S5_STATE
S5_TOK = 256
ATT_QB = 4 * CHUNK
ATT_W = ATT_QB + N_LEFT * CHUNK
ATT_KT = MXU_DIM
ATT_PERIOD = ATT_QB + ATT_W
CONV_TR = 128
CONV_PAD = 32


def _resident(shape):
    nd = len(shape)
    return pl.BlockSpec(shape, lambda *_: (0,) * nd, pipeline_mode=pl.Buffered(1))


def _rmsnorm_bf16(x, g):
    ms = jnp.mean(x * x, axis=-1, keepdims=True)
    return ((x * lax.rsqrt(ms + EPS)) * g).astype(BF16)


def _sigmoid(x):
    return jax.nn.sigmoid(x)


class _Casts:
    def __init__(self, items, steps):
        self.in_specs, self.out_specs, self.out_shape, self.args, self.nblks = [], [], [], [], []
        for stacked, layer in items:
            _, rows, cols = stacked.shape
            every_step = rows % CAST_MIN_ROWS == 0 and rows // CAST_MIN_ROWS <= steps
            rb = CAST_MIN_ROWS if every_step else CAST_ROWS
            nblk = rows // rb
            assert nblk * rb == rows and nblk <= steps
            self.nblks.append(None if every_step else nblk)
            self.in_specs.append(pl.BlockSpec(
                (None, rb, cols),
                lambda i, layer=layer, nblk=nblk: (layer, jnp.minimum(i, nblk - 1), 0)))
            self.out_specs.append(pl.BlockSpec(
                (rb, cols), lambda i, nblk=nblk: (jnp.minimum(i, nblk - 1), 0)))
            self.out_shape.append(jax.ShapeDtypeStruct((rows, cols), BF16))
            self.args.append(stacked)

    @staticmethod
    def run_predicated(step, nblks, src_refs, dst_refs):
        for nblk in sorted({n for n in nblks if n is not None}):
            @pl.when(step < nblk)
            def _():
                for n, src, dst in zip(nblks, src_refs, dst_refs):
                    if n == nblk:
                        dst[...] = src[...].astype(BF16)

    @staticmethod
    def run_every_step(nblks, src_refs, dst_refs):
        for n, src, dst in zip(nblks, src_refs, dst_refs):
            if n is None:
                dst[...] = src[...].astype(BF16)


def _ffn_kernel(x_ref, g_ref, wa_ref, wb_ref, wd_ref, o_ref, *, bounds):
    x = x_ref[...]
    h = _rmsnorm_bf16(x, g_ref[...])
    y = jnp.zeros_like(x)
    for f0, f1 in zip(bounds[:-1], bounds[1:]):
        a = jnp.dot(h, wa_ref[:, f0:f1], preferred_element_type=F32)
        b = jnp.dot(h, wb_ref[:, f0:f1], preferred_element_type=F32)
        act = ((a * _sigmoid(a)) * b).astype(BF16)
        y = y + jnp.dot(act, wd_ref[f0:f1, :], preferred_element_type=F32)
    o_ref[...] = x + 0.5 * y


def _ffn_bounds():
    tiles = D_FF // MXU_DIM
    assert tiles * MXU_DIM == D_FF
    return (0, (tiles + 1) // 2 * MXU_DIM, D_FF)


def _ffn(x, g, w_up, w_down):
    n = x.shape[0]
    tok = pl.BlockSpec((TM_WIDE, D_MODEL), lambda i: (i, 0))
    def up_half(j):
        return pl.BlockSpec((D_MODEL, D_FF), lambda i: (0, j), pipeline_mode=pl.Buffered(1))
    return pl.pallas_call(
        functools.partial(_ffn_kernel, bounds=_ffn_bounds()),
        out_shape=jax.ShapeDtypeStruct((n, D_MODEL), F32),
        grid=(n // TM_WIDE,),
        in_specs=[tok, _resident((1, D_MODEL)), up_half(0), up_half(1), _resident((D_FF, D_MODEL))],
        out_specs=tok,
        compiler_params=pltpu.CompilerParams(
            dimension_semantics=("arbitrary",), vmem_limit_bytes=VMEM_LIMIT),
        name="ffn",
    )(x, g, w_up, w_up, w_down)


def _head_pair_norm(x2, gain2):
    lane = lax.broadcasted_iota(jnp.int32, (1, LANES), 1)
    first = lane < HEAD_DIM
    sq = x2 * x2
    s_a = jnp.sum(jnp.where(first, sq, 0.0), axis=-1, keepdims=True)
    s_b = jnp.sum(jnp.where(first, 0.0, sq), axis=-1, keepdims=True)
    ms = jnp.where(first, s_a, s_b) * (1.0 / HEAD_DIM)
    return (x2 * lax.rsqrt(ms + EPS)) * gain2


def _inproj_kernel(x_ref, g_ref, w_ref, qg_ref, kg_ref, *refs, cast_nblks):
    n_cast = len(cast_nblks)
    u_ref, q_ref, kt_ref, v_ref, hc_ref = refs[n_cast:n_cast + 5]
    _Casts.run_predicated(pl.program_id(0), cast_nblks, refs[:n_cast], refs[n_cast + 5:])
    _Casts.run_every_step(cast_nblks, refs[:n_cast], refs[n_cast + 5:])
    h = _rmsnorm_bf16(x_ref[...], g_ref[...])
    proj = jnp.dot(h, w_ref[...], preferred_element_type=F32)
    u_ref[...] = proj[:, :D_S5]
    q0, k0, v0, z0 = D_S5, D_S5 + D_ATT, D_S5 + 2 * D_ATT, D_S5 + 3 * D_ATT
    scale = HEAD_DIM ** -0.5 * LOG2E
    for p in range(D_ATT // LANES):
        lo = p * LANES
        qn = _head_pair_norm(proj[:, q0 + lo:q0 + lo + LANES], qg_ref[...])
        kn = _head_pair_norm(proj[:, k0 + lo:k0 + lo + LANES], kg_ref[...])
        q_ref[:, lo:lo + LANES] = (qn * scale).astype(BF16)
        kt_ref[lo:lo + LANES, :] = kn.T.astype(BF16)
    v_ref[...] = proj[:, v0:v0 + D_ATT].astype(BF16)
    a = proj[:, z0:z0 + D_CONV]
    gt = proj[:, z0 + D_CONV:z0 + 2 * D_CONV]
    hc_ref[...] = a * _sigmoid(gt)


def _inproj(x, g, w_in, qg2, kg2, seq, cast_items):
    n = x.shape[0]
    steps = n // TM_WIDE
    tiles_per_seq = seq // TM_WIDE
    assert tiles_per_seq * TM_WIDE == seq
    casts = _Casts(cast_items, steps)
    def tok(width):
        return pl.BlockSpec((TM_WIDE, width), lambda i: (i, 0))
    w_small = pl.BlockSpec((D_MODEL, N_SMALL), lambda i: (0, 0), pipeline_mode=pl.Buffered(1))
    kt_spec = pl.BlockSpec((None, D_ATT, TM_WIDE),
                           lambda i: (i // tiles_per_seq, 0, i % tiles_per_seq))
    return pl.pallas_call(
        functools.partial(_inproj_kernel, cast_nblks=tuple(casts.nblks)),
        out_shape=(jax.ShapeDtypeStruct((n, D_S5), F32),
                   jax.ShapeDtypeStruct((n, D_ATT), BF16),
                   jax.ShapeDtypeStruct((n // seq, D_ATT, seq), BF16),
                   jax.ShapeDtypeStruct((n, D_ATT), BF16),
                   jax.ShapeDtypeStruct((n, D_CONV), F32), *casts.out_shape),
        grid=(steps,),
        in_specs=[tok(D_MODEL), _resident((1, D_MODEL)), w_small,
                  _resident((1, LANES)), _resident((1, LANES)), *casts.in_specs],
        out_specs=(tok(D_S5), tok(D_ATT), kt_spec, tok(D_ATT), tok(D_CONV), *casts.out_specs),
        compiler_params=pltpu.CompilerParams(
            dimension_semantics=("arbitrary",), vmem_limit_bytes=VMEM_LIMIT),
        name="inproj",
    )(x, g, w_in, qg2, kg2, *casts.args)


def _s5_tables(lambda_re, lambda_im, log_dt, b_re, b_im, c_re, c_im, d_skip):
    lr = jnp.minimum(lambda_re.astype(F32), -1e-4)
    li = lambda_im.astype(F32)
    dt = jnp.exp(log_dt.astype(F32))[:, None]
    mag = jnp.exp(lr * dt)
    ar = mag * jnp.cos(li * dt)
    ai = mag * jnp.sin(li * dt)
    den = lr * lr + li * li
    coef_r = ((ar - 1.0) * lr + ai * li) / den
    coef_i = (ai * lr - (ar - 1.0) * li) / den
    br = b_re.astype(F32)
    bi = b_im.astype(F32)
    bbar_r = coef_r[..., None] * br - coef_i[..., None] * bi
    bbar_i = coef_r[..., None] * bi + coef_i[..., None] * br
    cr = c_re.astype(F32)
    ci = c_im.astype(F32)

    pr, pi = [jnp.ones_like(ar)], [jnp.zeros_like(ai)]
    for _ in range(S5_L):
        r, i = pr[-1], pi[-1]
        pr.append(r * ar - i * ai)
        pi.append(r * ai + i * ar)
    pr = jnp.stack(pr)
    pi = jnp.stack(pi)

    ncb, gb = S5_NCB, S5_GB
    pw_r = pr[S5_L - 1::-1]
    pw_i = pi[S5_L - 1::-1]
    e_r = pw_r[..., None] * bbar_r[None] - pw_i[..., None] * bbar_i[None]
    e_i = pw_r[..., None] * bbar_i[None] + pw_i[..., None] * bbar_r[None]
    p_blk = jnp.stack([jnp.transpose(e_r, (0, 1, 3, 2)), jnp.transpose(e_i, (0, 1, 3, 2))], axis=3)
    p_blk = p_blk.reshape(S5_L, ncb, gb * S5_GROUP, 2 * S5_STATE)
    p_blk = jnp.transpose(p_blk, (1, 0, 2, 3)).reshape(ncb, S5_BROW, 2 * S5_STATE)

    m_r = pr[:S5_L, :, None, :] * cr[None] - pi[:S5_L, :, None, :] * ci[None]
    m_i = pr[:S5_L, :, None, :] * ci[None] + pi[:S5_L, :, None, :] * cr[None]
    kern = (jnp.sum(m_r[:, :, :, :, None] * bbar_r[None, :, None, :, :], axis=3)
            - jnp.sum(m_i[:, :, :, :, None] * bbar_i[None, :, None, :, :], axis=3))
    t_blk = jnp.transpose(kern, (0, 1, 3, 2)).reshape(S5_L, ncb, S5_CB, S5_GROUP)
    t_blk = jnp.transpose(t_blk, (1, 0, 2, 3))

    q_r = pr[1:, :, None, :] * cr[None] - pi[1:, :, None, :] * ci[None]
    q_i = -(pr[1:, :, None, :] * ci[None] + pi[1:, :, None, :] * cr[None])
    q_blk = jnp.stack([q_r, q_i], axis=0).reshape(2, S5_L, ncb, gb, S5_GROUP, S5_STATE)
    q_blk = jnp.transpose(q_blk, (2, 1, 4, 0, 3, 5))
    q_blk = q_blk.reshape(ncb, S5_L, S5_GROUP, S5_BSTATE)

    half_blocks = S5_BSTATE // 2 // LANES
    a_r = pr[S5_L].reshape(ncb, half_blocks, 1, LANES)
    a_i = pi[S5_L].reshape(ncb, half_blocks, 1, LANES)
    d_row = jnp.tile(d_skip.astype(F32).reshape(ncb, 1, S5_CB), (1, 1, S5_L))
    pt, q_mat = _s5_expand(p_blk.astype(BF16), t_blk.astype(BF16), q_blk.astype(BF16))
    return pt, q_mat, a_r, a_i, d_row


def _s5_expand_kernel(p_ref, t_ref, q_ref, pt_ref, qm_ref):
    def iota(shape, dim):
        return lax.broadcasted_iota(jnp.int32, shape, dim)

    def onehot(rows, cols, row_key, col_key):
        r, c = iota((rows, cols), 0), iota((rows, cols), 1)
        return jnp.where(row_key(r) == col_key(c), 1.0, 0.0)

    half = S5_BSTATE // 2
    rep = onehot(2 * S5_STATE, S5_BSTATE, lambda r: r,
                 lambda c: (c // half) * S5_STATE + c % S5_STATE).astype(BF16)
    same = onehot(S5_CB, S5_BSTATE, lambda r: r // S5_GROUP, lambda c: (c % half) // S5_STATE)
    rep_c = onehot(S5_GROUP, S5_CB, lambda r: r, lambda c: c % S5_GROUP).astype(BF16)
    same_g = onehot(S5_CB, S5_CB, lambda r: r // S5_GROUP, lambda c: c // S5_GROUP)
    same_q = onehot(S5_BSTATE, S5_CB, lambda r: (r % half) // S5_STATE, lambda c: c // S5_GROUP)
    lag_blk = [(jnp.dot(t_ref[d], rep_c, preferred_element_type=F32) * same_g).astype(BF16)
               for d in range(S5_L)]
    zero_blk = jnp.zeros((S5_CB, S5_CB), BF16)
    for s in range(S5_L):
        rows = slice(s * S5_CB, (s + 1) * S5_CB)
        p_full = jnp.dot(p_ref[rows, :], rep, preferred_element_type=F32) * same
        pt_ref[rows, :S5_BSTATE] = p_full.astype(BF16)
        for t in range(S5_L):
            cols = slice(S5_BSTATE + t * S5_CB, S5_BSTATE + (t + 1) * S5_CB)
            pt_ref[rows, cols] = lag_blk[t - s] if t >= s else zero_blk
    for t in range(S5_L):
        q_full = lax.dot_general(q_ref[t], rep_c, (((0,), (0,)), ((), ())),
                                 preferred_element_type=F32) * same_q
        qm_ref[:, t * S5_CB:(t + 1) * S5_CB] = q_full.astype(BF16)


def _s5_expand(p_blk, t_blk, q_blk):
    def blk(shape):
        return pl.BlockSpec((None,) + shape, lambda cb: (cb,) + (0,) * len(shape))
    return pl.pallas_call(
        _s5_expand_kernel,
        out_shape=(jax.ShapeDtypeStruct((S5_NCB, S5_BROW, S5_BSTATE + S5_BROW), BF16),
                   jax.ShapeDtypeStruct((S5_NCB, S5_BSTATE, S5_BROW), BF16)),
        grid=(S5_NCB,),
        in_specs=[blk((S5_BROW, 2 * S5_STATE)), blk((S5_L, S5_CB, S5_GROUP)),
                  blk((S5_L, S5_GROUP, S5_BSTATE))],
        out_specs=(blk((S5_BROW, S5_BSTATE + S5_BROW)), blk((S5_BSTATE, S5_BROW))),
        compiler_params=pltpu.CompilerParams(
            dimension_semantics=("arbitrary",), vmem_limit_bytes=VMEM_LIMIT),
        name="s5_tables",
    )(p_blk, t_blk, q_blk)


def _s5_kernel(ua_ref, ub_ref, pt_ref, q_ref, ar_ref, ai_ref, d_ref, wg_ref, o_ref,
               sr_ref, si_ref, u8_ref, es_ref, ot_ref):
    nb = ua_ref.shape[0]
    ncl = S5_TOK // S5_L
    nlb = S5_BSTATE // LANES
    hlb = nlb // 2
    u_refs = (ua_ref, ub_ref)
    low = lax.broadcasted_iota(jnp.int32, (1, LANES), 1) < S5_CB

    def interleave(a, b):
        return (jnp.where(low, a, pltpu.roll(b, S5_CB, axis=1)),
                jnp.where(low, pltpu.roll(a, S5_CB, axis=1), b))

    @pl.when(pl.program_id(0) == 0)
    def _():
        sr_ref[...] = jnp.zeros_like(sr_ref)
        si_ref[...] = jnp.zeros_like(si_ref)

    sub = 8
    ncg = ncl // sub
    grp = nb * sub
    for hh, u_ref in enumerate(u_refs):
        for t2 in range(S5_L // 2):
            for cg in range(ncg):
                tok = [u_ref[:, pl.ds(cg * sub * S5_L + 2 * t2 + k, sub, stride=S5_L), :]
                       .reshape(grp, LANES) for k in range(2)]
                for k, blk in enumerate(interleave(*tok)):
                    u8_ref[2 * hh + k, cg * grp:(cg + 1) * grp, t2 * LANES:(t2 + 1) * LANES] = blk

    for cb in range(S5_NCB):
        e = jnp.dot(u8_ref[cb].astype(BF16), pt_ref[cb, :, :S5_BSTATE],
                    preferred_element_type=F32)
        for j in range(nlb):
            es_ref[cb, j] = e[:, j * LANES:(j + 1) * LANES]

    a_r = ar_ref[...]
    a_i = ai_ref[...]
    sr = sr_ref[...]
    si = si_ref[...]
    for cl in range(ncl):
        step = pl.ds((cl // sub) * grp + cl % sub, nb, stride=sub)
        e_r = es_ref[:, :hlb, step, :]
        e_i = es_ref[:, hlb:, step, :]
        es_ref[:, :hlb, step, :] = sr
        es_ref[:, hlb:, step, :] = si
        sr, si = a_r * sr - a_i * si + e_r, a_r * si + a_i * sr + e_i
    sr_ref[...] = sr
    si_ref[...] = si

    ys = []
    for cb in range(S5_NCB):
        u8 = u8_ref[cb]
        s_all = jnp.concatenate([es_ref[cb, j] for j in range(nlb)], axis=1).astype(BF16)
        y = (jnp.dot(u8.astype(BF16), pt_ref[cb, :, S5_BSTATE:], preferred_element_type=F32)
             + jnp.dot(s_all, q_ref[cb], preferred_element_type=F32) + d_ref[cb] * u8)
        ys.append(jax.nn.gelu(y))
    for t2 in range(S5_L // 2):
        lanes = slice(t2 * LANES, (t2 + 1) * LANES)
        halves = [interleave(ys[2 * hh][:, lanes], ys[2 * hh + 1][:, lanes]) for hh in range(2)]
        for k in range(2):
            t = 2 * t2 + k
            yt = jnp.concatenate([halves[0][k], halves[1][k]], axis=1).astype(BF16)
            ag = jnp.dot(yt, wg_ref[...], preferred_element_type=F32)
            out = ag[:, :D_S5] * _sigmoid(ag[:, D_S5:])
            for hh in range(D_S5 // LANES):
                for cg in range(ncg):
                    ot_ref[hh, :, pl.ds(cg * sub * S5_L + t, sub, stride=S5_L), :] = (
                        out[cg * grp:(cg + 1) * grp, hh * LANES:(hh + 1) * LANES].reshape(nb, sub, LANES))
    o_ref[...] = jnp.concatenate([ot_ref[0], ot_ref[1]], axis=-1).astype(BF16)


def _s5(u, pt, q_mat, a_r, a_i, d_row, w_glu):
    nb, seq, _ = u.shape
    rows = nb * (S5_TOK // S5_L)
    nlb = S5_BSTATE // LANES
    assert 2 * S5_CB == LANES and D_S5 == 2 * LANES
    return pl.pallas_call(
        _s5_kernel,
        out_shape=jax.ShapeDtypeStruct((nb, seq, D_S5), BF16),
        grid=(seq // S5_TOK,),
        in_specs=[pl.BlockSpec((nb, S5_TOK, LANES), lambda i: (0, i, 0)),
                  pl.BlockSpec((nb, S5_TOK, LANES), lambda i: (0, i, 1)),
                  _resident(pt.shape), _resident(q_mat.shape), _resident(a_r.shape),
                  _resident(a_i.shape), _resident(d_row.shape), _resident(w_glu.shape)],
        out_specs=pl.BlockSpec((nb, S5_TOK, D_S5), lambda i: (0, i, 0)),
        scratch_shapes=[pltpu.VMEM((S5_NCB, nlb // 2, nb, LANES), F32),
                        pltpu.VMEM((S5_NCB, nlb // 2, nb, LANES), F32),
                        pltpu.VMEM((S5_NCB, rows, S5_BROW), F32),
                        pltpu.VMEM((S5_NCB, nlb, rows, LANES), F32),
                        pltpu.VMEM((D_S5 // LANES, nb, S5_TOK, LANES), F32)],
        compiler_params=pltpu.CompilerParams(
            dimension_semantics=("arbitrary",), vmem_limit_bytes=VMEM_LIMIT),
        name="s5",
    )(u, u, pt, q_mat, a_r, a_i, d_row, w_glu)


def _attn_bias_vec(rel_bias):
    rb = rel_bias.astype(F32) * LOG2E
    far_past, far_future = rb[:, 2 * MAX_REL:], rb[:, :1]
    n_const = N_LEFT * CHUNK - MAX_REL
    return jnp.concatenate([
        jnp.broadcast_to(far_past, (HEADS, n_const)),
        rb[:, ::-1],
        jnp.broadcast_to(far_future, (HEADS, ATT_W - n_const - 2 * MAX_REL - 1)),
        jnp.broadcast_to(far_past, (HEADS, ATT_PERIOD - ATT_W)),
    ], axis=1)


def _attn_build_bias(vec_ref, bm_ref):
    r = lax.broadcasted_iota(jnp.int32, (ATT_QB, ATT_W), 0)
    c = lax.broadcasted_iota(jnp.int32, (ATT_QB, ATT_W), 1)
    dchunk = r // CHUNK + N_LEFT - c // CHUNK
    ok = (dchunk >= 0) & (dchunk <= N_LEFT)
    for h in range(HEADS):
        base = jnp.broadcast_to(vec_ref[h:h + 1, :], (ATT_QB, ATT_PERIOD))
        toep = pltpu.roll(base, 0, 1, stride=1, stride_axis=0)[:, :ATT_W]
        bm_ref[h // 2, h % 2] = jnp.where(ok, toep, -1e30)


def _attn_kernel(q_ref, kt_ref, v_ref, vec_ref, o_ref, bm_ref):
    seq = q_ref.shape[1]
    lane = lax.broadcasted_iota(jnp.int32, (1, LANES), 1)
    first = lane < HEAD_DIM

    @pl.when(pl.program_id(0) == 0)
    def _():
        _attn_build_bias(vec_ref, bm_ref)

    def block(q0, k0, width):
        for pr in range(D_ATT // LANES):
            cols = slice(pr * LANES, (pr + 1) * LANES)
            q2 = q_ref[0, pl.ds(q0, ATT_QB), cols]
            outs = []
            for hh in range(2):
                sel = first if hh == 0 else jnp.logical_not(first)
                qh = jnp.where(sel, q2, jnp.zeros_like(q2))
                for c0 in range(0, width, ATT_KT):
                    kt = kt_ref[0, cols, pl.ds(k0 + c0, ATT_KT)]
                    vw = v_ref[0, pl.ds(k0 + c0, ATT_KT), cols]
                    s = jnp.dot(qh, kt, preferred_element_type=F32)
                    b0 = ATT_W - width + c0
                    s = s + bm_ref[pr, hh, :, b0:b0 + ATT_KT]
                    vh = jnp.where(sel, vw, jnp.ones_like(vw))
                    if c0 == 0:
                        mx = jnp.max(s, axis=-1, keepdims=True)
                        acc = jnp.dot(jnp.exp2(s - mx).astype(BF16), vh, preferred_element_type=F32)
                    else:
                        mx_new = jnp.maximum(mx, jnp.max(s, axis=-1, keepdims=True))
                        p = jnp.exp2(s - mx_new).astype(BF16)
                        acc = acc * jnp.exp2(mx - mx_new) + jnp.dot(p, vh, preferred_element_type=F32)
                        mx = mx_new
                outs.append(acc)
            num = jnp.where(first, outs[0], outs[1])
            den = pltpu.roll(jnp.where(first, outs[1], outs[0]), HEAD_DIM, axis=1)
            o_ref[0, pl.ds(q0, ATT_QB), cols] = (num / den).astype(BF16)

    n_short = N_LEFT * CHUNK // ATT_QB
    for qb in range(n_short):
        block(qb * ATT_QB, 0, (qb + 1) * ATT_QB)

    def body(qb, carry):
        q0 = pl.multiple_of(qb * ATT_QB, ATT_QB)
        k0 = pl.multiple_of(q0 - N_LEFT * CHUNK, ATT_QB)
        block(q0, k0, ATT_W)
        return carry
    lax.fori_loop(n_short, seq // ATT_QB, body, 0)


def _attn(q, kt, v, bias_vec):
    nb, seq, _ = q.shape
    seqblk = pl.BlockSpec((1, seq, D_ATT), lambda b: (b, 0, 0))
    return pl.pallas_call(
        _attn_kernel,
        out_shape=jax.ShapeDtypeStruct((nb, seq, D_ATT), BF16),
        grid=(nb,),
        in_specs=[seqblk, pl.BlockSpec((1, D_ATT, seq), lambda b: (b, 0, 0)), seqblk,
                  _resident(bias_vec.shape)],
        out_specs=seqblk,
        scratch_shapes=[pltpu.VMEM((HEADS // 2, 2, ATT_QB, ATT_W), F32)],
        compiler_params=pltpu.CompilerParams(
            dimension_semantics=("arbitrary",), vmem_limit_bytes=VMEM_LIMIT),
        name="attn",
    )(q, kt, v, bias_vec)


def _conv_taps(hc_ref, cw_ref, cb_ref, cbuf_ref, first_tile):
    nh = D_CONV // LANES
    tm = hc_ref.shape[0]

    @pl.when(first_tile)
    def _():
        cbuf_ref[:, :CONV_PAD, :] = jnp.zeros((nh, CONV_PAD, LANES), F32)
    for hh in range(nh):
        cbuf_ref[hh, CONV_PAD:, :] = hc_ref[:, hh * LANES:(hh + 1) * LANES]
    shift = CONV_PAD - (CONV_WIDTH - 1)
    halves = []
    for hh in range(nh):
        tiles = []
        for r0 in range(0, tm, CONV_TR):
            acc = jnp.zeros((CONV_TR, LANES), F32) + cb_ref[hh]
            for j in range(CONV_WIDTH):
                lo = r0 + shift + j
                acc = acc + cw_ref[hh, j:j + 1, :] * cbuf_ref[hh, lo:lo + CONV_TR, :]
            tiles.append(acc)
        halves.append(jnp.concatenate(tiles, axis=0))
    for hh in range(nh):
        cbuf_ref[hh, :CONV_PAD, :] = cbuf_ref[hh, tm:tm + CONV_PAD, :]
    return halves


def _conv_norm(halves, lg_ref, lb_ref):
    acc = jnp.concatenate(halves, axis=-1)
    mu = jnp.mean(acc, axis=-1, keepdims=True)
    cen = acc - mu
    var = jnp.mean(cen * cen, axis=-1, keepdims=True)
    y = (cen * lax.rsqrt(var + EPS)) * lg_ref[...] + lb_ref[...]
    return (y * _sigmoid(y)).astype(BF16)


def _after(v, zero_ref, width):
    bits = pltpu.bitcast(v, jnp.uint32)
    acc = bits[0:8, :]
    for r0 in range(8, v.shape[0], 8):
        acc = acc | bits[r0:r0 + 8, :]
    zero = pltpu.bitcast(acc[0:1, :] & zero_ref[...], F32)
    return jnp.tile(zero, (1, width // LANES)).astype(BF16)


def _merge_kernel(x_ref, s5_ref, at_ref, hc_ref, g_ref, wg_ref, bg_ref, ws_ref, wa_ref, wc_ref, wo_ref,
                  cw_ref, cb_ref, lg_ref, lb_ref, zero_ref, *refs, tiles_per_seq, cast_nblks):
    n_cast = len(cast_nblks)
    o_ref, cbuf_ref = refs[n_cast], refs[-1]
    _Casts.run_predicated(pl.program_id(0), cast_nblks, refs[:n_cast], refs[n_cast + 1:-1])
    halves = _conv_taps(hc_ref, cw_ref, cb_ref, cbuf_ref, pl.program_id(0) % tiles_per_seq == 0)
    _Casts.run_every_step(cast_nblks, refs[:n_cast], refs[n_cast + 1:-1])
    x = x_ref[...]
    h = _rmsnorm_bf16(x, g_ref[...])
    lhs = (h, h + _after(halves[0], zero_ref, D_MODEL), h + _after(halves[1], zero_ref, D_MODEL))
    branches = (lambda: s5_ref[...], lambda: at_ref[...], lambda: _conv_norm(halves, lg_ref, lb_ref))
    merged = jnp.zeros_like(x)
    for i, (branch, w_ref) in enumerate(zip(branches, (ws_ref, wa_ref, wc_ref))):
        cols = slice(i * D_MODEL, (i + 1) * D_MODEL)
        gcols = slice(N_SMALL + i * D_MODEL, N_SMALL + (i + 1) * D_MODEL)
        logits = jnp.dot(lhs[i], wg_ref[:, gcols], preferred_element_type=F32) + bg_ref[:, cols]
        y = jnp.dot(branch(), w_ref[...], preferred_element_type=F32)
        merged = merged + _sigmoid(logits) * y
    o_ref[...] = x + jnp.dot(merged.astype(BF16), wo_ref[...], preferred_element_type=F32)


def _merge(x, s5o, ato, hc, g, w_in, b_gate, w_s5, w_at, w_cv, w_out, w_dw, b_dw, ln_g, ln_b, seq,
           cast_items):
    n = x.shape[0]
    nh = D_CONV // LANES
    zero = jnp.zeros((1, LANES), jnp.uint32)
    assert seq % TM == 0
    steps = n // TM
    casts = _Casts(cast_items, steps)
    def tok(width):
        return pl.BlockSpec((TM, width), lambda i: (i, 0))
    return pl.pallas_call(
        functools.partial(_merge_kernel, tiles_per_seq=seq // TM, cast_nblks=tuple(casts.nblks)),
        out_shape=(jax.ShapeDtypeStruct((n, D_MODEL), F32), *casts.out_shape),
        grid=(steps,),
        in_specs=[tok(D_MODEL), tok(D_S5), tok(D_ATT), tok(D_CONV),
                  _resident((1, D_MODEL)), _resident(w_in.shape),
                  _resident((1, 3 * D_MODEL)), _resident((D_S5, D_MODEL)),
                  _resident((D_ATT, D_MODEL)), _resident((D_CONV, D_MODEL)),
                  _resident((D_MODEL, D_MODEL)), _resident((nh, CONV_WIDTH, LANES)),
                  _resident((nh, 1, LANES)), _resident((1, D_CONV)), _resident((1, D_CONV)),
                  _resident((1, LANES)), *casts.in_specs],
        out_specs=(tok(D_MODEL), *casts.out_specs),
        scratch_shapes=[pltpu.VMEM((nh, CONV_PAD + TM, LANES), F32)],
        compiler_params=pltpu.CompilerParams(
            dimension_semantics=("arbitrary",), vmem_limit_bytes=VMEM_LIMIT),
        name="merge",
    )(x, s5o, ato, hc, g, w_in, b_gate, w_s5, w_at, w_cv, w_out, w_dw, b_dw, ln_g, ln_b, zero,
      *casts.args)


def kernel(x, ffn1_norm, ffn1_w_up, ffn1_w_down, mix_norm, w_in, b_gate, s5_lambda_re, s5_lambda_im, s5_log_dt, s5_b_re, s5_b_im, s5_c_re, s5_c_im, s5_d, s5_w_glu, w_br_s5, attn_q_gain, attn_k_gain, attn_rel_bias, w_br_attn, conv_w_dw, conv_b_dw, conv_ln_g, conv_ln_b, w_br_conv, w_out, ffn2_norm, ffn2_w_up, ffn2_w_down):
    nb, seq, d = x.shape
    n = nb * seq
    depth = ffn1_norm.shape[0]
    xt = x.reshape(n, d)
    row = lambda v: v.reshape(1, -1).astype(F32)
    f1_up, f1_down, w_in_l = (ffn1_w_up[0].astype(BF16), ffn1_w_down[0].astype(BF16),
                              w_in[0].astype(BF16))
    for l in range(depth):
        xt = _ffn(xt, row(ffn1_norm[l]), f1_up, f1_down)

        qg2 = jnp.tile(row(attn_q_gain[l]), (1, LANES // HEAD_DIM))
        kg2 = jnp.tile(row(attn_k_gain[l]), (1, LANES // HEAD_DIM))
        own = (ffn2_w_up, ffn2_w_down, w_br_s5, w_br_attn, w_br_conv, w_out, s5_w_glu)
        u, q, kt, v, hc, f2_up, f2_down, w_s5, w_at, w_cv, w_o, w_glu = _inproj(
            xt, row(mix_norm[l]), w_in_l, qg2, kg2, seq, [(w, l) for w in own])

        pt, q_mat, a_r, a_i, d_row = _s5_tables(
            s5_lambda_re[l], s5_lambda_im[l], s5_log_dt[l], s5_b_re[l], s5_b_im[l],
            s5_c_re[l], s5_c_im[l], s5_d[l])
        s5o = _s5(u.reshape(nb, seq, D_S5), pt, q_mat, a_r, a_i, d_row, w_glu)

        ato = _attn(q.reshape(nb, seq, D_ATT), kt, v.reshape(nb, seq, D_ATT),
                    _attn_bias_vec(attn_rel_bias[l]))

        nh = D_CONV // LANES
        w_dw = conv_w_dw[l].astype(F32).reshape(CONV_WIDTH, nh, LANES).transpose(1, 0, 2)
        nxt = (ffn1_w_up, ffn1_w_down, w_in) if l + 1 < depth else ()
        xt, *nxt_bf16 = _merge(
            xt, s5o.reshape(n, D_S5), ato.reshape(n, D_ATT), hc, row(mix_norm[l]), w_in_l,
            row(b_gate[l]), w_s5, w_at, w_cv, w_o, w_dw,
            conv_b_dw[l].astype(F32).reshape(nh, 1, LANES), row(conv_ln_g[l]), row(conv_ln_b[l]),
            seq, [(w, l + 1) for w in nxt])
        if nxt_bf16:
            f1_up, f1_down, w_in_l = nxt_bf16

        xt = _ffn(xt, row(ffn2_norm[l]), f2_up, f2_down)
    return xt.reshape(nb, seq, d)
```

```python
import functools
import math

import jax
import jax.numpy as jnp
from jax import lax
from jax.experimental import pallas as pl
from jax.experimental.pallas import tpu as pltpu

F32 = jnp.float32
BF16 = jnp.bfloat16

D_MODEL = 1024
D_S5 = 256
S5_GROUP = 16
S5_GROUPS = 16
S5_STATE = 64
D_ATT = 512
HEAD_DIM = 64
HEADS = 8
CHUNK = 64
N_LEFT = 8
MAX_REL = 128
D_CONV = 256
CONV_WIDTH = 31
D_FF = 2816
EPS = 1e-6
LOG2E = math.log2(math.e)
N_SMALL = D_S5 + 3 * D_ATT + 2 * D_CONV

LANES = 128
MXU_DIM = 256
VMEM_LIMIT = 56 * 1024 * 1024

TM = 1024
TM_WIDE = 1024
CAST_MIN_ROWS = 16
CAST_ROWS = 64
S5_L = 8
S5_CB = LANES // 2
S5_GB = S5_CB // S5_GROUP
S5_NCB = D_S5 // S5_CB
S5_BROW = S5_L * S5_CB
S5_BSTATE = 2 * S5_GB * S5_STATE
S5_TOK = 256
ATT_QB = 4 * CHUNK
ATT_W = ATT_QB + N_LEFT * CHUNK
ATT_KT = MXU_DIM
ATT_PERIOD = ATT_QB + ATT_W
CONV_TR = 64
CONV_PAD = 32


def _resident(shape):
    nd = len(shape)
    return pl.BlockSpec(shape, lambda *_: (0,) * nd, pipeline_mode=pl.Buffered(1))


def _rmsnorm_bf16(x, g):
    ms = jnp.mean(x * x, axis=-1, keepdims=True)
    return ((x * lax.rsqrt(ms + EPS)) * g).astype(BF16)


def _sigmoid(x):
    return jax.nn.sigmoid(x)


class _Casts:
    def __init__(self, items, steps):
        self.in_specs, self.out_specs, self.out_shape, self.args, self.nblks = [], [], [], [], []
        for stacked, layer in items:
            _, rows, cols = stacked.shape
            every_step = rows % CAST_MIN_ROWS == 0 and rows // CAST_MIN_ROWS <= steps
            rb = CAST_MIN_ROWS if every_step else CAST_ROWS
            nblk = rows // rb
            assert nblk * rb == rows and nblk <= steps
            self.nblks.append(None if every_step else nblk)
            self.in_specs.append(pl.BlockSpec(
                (None, rb, cols),
                lambda i, layer=layer, nblk=nblk: (layer, jnp.minimum(i, nblk - 1), 0)))
            self.out_specs.append(pl.BlockSpec(
                (rb, cols), lambda i, nblk=nblk: (jnp.minimum(i, nblk - 1), 0)))
            self.out_shape.append(jax.ShapeDtypeStruct((rows, cols), BF16))
            self.args.append(stacked)

    @staticmethod
    def run_predicated(step, nblks, src_refs, dst_refs):
        for nblk in sorted({n for n in nblks if n is not None}):
            @pl.when(step < nblk)
            def _():
                for n, src, dst in zip(nblks, src_refs, dst_refs):
                    if n == nblk:
                        dst[...] = src[...].astype(BF16)

    @staticmethod
    def run_every_step(nblks, src_refs, dst_refs):
        for n, src, dst in zip(nblks, src_refs, dst_refs):
            if n is None:
                dst[...] = src[...].astype(BF16)


def _ffn_kernel(x_ref, g_ref, wa_ref, wb_ref, wd_ref, o_ref, *, bounds):
    x = x_ref[...]
    h = _rmsnorm_bf16(x, g_ref[...])
    y = jnp.zeros_like(x)
    for f0, f1 in zip(bounds[:-1], bounds[1:]):
        a = jnp.dot(h, wa_ref[:, f0:f1], preferred_element_type=F32)
        b = jnp.dot(h, wb_ref[:, f0:f1], preferred_element_type=F32)
        act = ((a * _sigmoid(a)) * b).astype(BF16)
        y = y + jnp.dot(act, wd_ref[f0:f1, :], preferred_element_type=F32)
    o_ref[...] = x + 0.5 * y


def _ffn_bounds():
    tiles = D_FF // MXU_DIM
    assert tiles * MXU_DIM == D_FF
    return (0, (tiles + 1) // 2 * MXU_DIM, D_FF)


def _ffn(x, g, w_up, w_down):
    n = x.shape[0]
    tok = pl.BlockSpec((TM_WIDE, D_MODEL), lambda i: (i, 0))
    def up_half(j):
        return pl.BlockSpec((D_MODEL, D_FF), lambda i: (0, j), pipeline_mode=pl.Buffered(1))
    return pl.pallas_call(
        functools.partial(_ffn_kernel, bounds=_ffn_bounds()),
        out_shape=jax.ShapeDtypeStruct((n, D_MODEL), F32),
        grid=(n // TM_WIDE,),
        in_specs=[tok, _resident((1, D_MODEL)), up_half(0), up_half(1), _resident((D_FF, D_MODEL))],
        out_specs=tok,
        compiler_params=pltpu.CompilerParams(
            dimension_semantics=("arbitrary",), vmem_limit_bytes=VMEM_LIMIT),
        name="ffn",
    )(x, g, w_up, w_up, w_down)


def _head_pair_norm(x2, gain2):
    lane = lax.broadcasted_iota(jnp.int32, (1, LANES), 1)
    first = lane < HEAD_DIM
    sq = x2 * x2
    s_a = jnp.sum(jnp.where(first, sq, 0.0), axis=-1, keepdims=True)
    s_b = jnp.sum(jnp.where(first, 0.0, sq), axis=-1, keepdims=True)
    ms = jnp.where(first, s_a, s_b) * (1.0 / HEAD_DIM)
    return (x2 * lax.rsqrt(ms + EPS)) * gain2


def _inproj_kernel(x_ref, g_ref, w_ref, qg_ref, kg_ref, *refs, cast_nblks):
    n_cast = len(cast_nblks)
    u_ref, q_ref, kt_ref, v_ref, hc_ref = refs[n_cast:n_cast + 5]
    _Casts.run_predicated(pl.program_id(0), cast_nblks, refs[:n_cast], refs[n_cast + 5:])
    _Casts.run_every_step(cast_nblks, refs[:n_cast], refs[n_cast + 5:])
    h = _rmsnorm_bf16(x_ref[...], g_ref[...])
    proj = jnp.dot(h, w_ref[...], preferred_element_type=F32)
    u_ref[...] = proj[:, :D_S5]
    q0, k0, v0, z0 = D_S5, D_S5 + D_ATT, D_S5 + 2 * D_ATT, D_S5 + 3 * D_ATT
    scale = HEAD_DIM ** -0.5 * LOG2E
    for p in range(D_ATT // LANES):
        lo = p * LANES
        qn = _head_pair_norm(proj[:, q0 + lo:q0 + lo + LANES], qg_ref[...])
        kn = _head_pair_norm(proj[:, k0 + lo:k0 + lo + LANES], kg_ref[...])
        q_ref[:, lo:lo + LANES] = (qn * scale).astype(BF16)
        kt_ref[lo:lo + LANES, :] = kn.T.astype(BF16)
    v_ref[...] = proj[:, v0:v0 + D_ATT].astype(BF16)
    a = proj[:, z0:z0 + D_CONV]
    gt = proj[:, z0 + D_CONV:z0 + 2 * D_CONV]
    hc_ref[...] = a * _sigmoid(gt)


def _inproj(x, g, w_in, qg2, kg2, seq, cast_items):
    n = x.shape[0]
    steps = n // TM_WIDE
    tiles_per_seq = seq // TM_WIDE
    assert tiles_per_seq * TM_WIDE == seq
    casts = _Casts(cast_items, steps)
    def tok(width):
        return pl.BlockSpec((TM_WIDE, width), lambda i: (i, 0))
    w_small = pl.BlockSpec((D_MODEL, N_SMALL), lambda i: (0, 0), pipeline_mode=pl.Buffered(1))
    kt_spec = pl.BlockSpec((None, D_ATT, TM_WIDE),
                           lambda i: (i // tiles_per_seq, 0, i % tiles_per_seq))
    return pl.pallas_call(
        functools.partial(_inproj_kernel, cast_nblks=tuple(casts.nblks)),
        out_shape=(jax.ShapeDtypeStruct((n, D_S5), F32),
                   jax.ShapeDtypeStruct((n, D_ATT), BF16),
                   jax.ShapeDtypeStruct((n // seq, D_ATT, seq), BF16),
                   jax.ShapeDtypeStruct((n, D_ATT), BF16),
                   jax.ShapeDtypeStruct((n, D_CONV), F32), *casts.out_shape),
        grid=(steps,),
        in_specs=[tok(D_MODEL), _resident((1, D_MODEL)), w_small,
                  _resident((1, LANES)), _resident((1, LANES)), *casts.in_specs],
        out_specs=(tok(D_S5), tok(D_ATT), kt_spec, tok(D_ATT), tok(D_CONV), *casts.out_specs),
        compiler_params=pltpu.CompilerParams(
            dimension_semantics=("arbitrary",), vmem_limit_bytes=VMEM_LIMIT),
        name="inproj",
    )(x, g, w_in, qg2, kg2, *casts.args)


def _s5_tables(lambda_re, lambda_im, log_dt, b_re, b_im, c_re, c_im, d_skip):
    lr = jnp.minimum(lambda_re.astype(F32), -1e-4)
    li = lambda_im.astype(F32)
    dt = jnp.exp(log_dt.astype(F32))[:, None]
    mag = jnp.exp(lr * dt)
    ar = mag * jnp.cos(li * dt)
    ai = mag * jnp.sin(li * dt)
    den = lr * lr + li * li
    coef_r = ((ar - 1.0) * lr + ai * li) / den
    coef_i = (ai * lr - (ar - 1.0) * li) / den
    br = b_re.astype(F32)
    bi = b_im.astype(F32)
    bbar_r = coef_r[..., None] * br - coef_i[..., None] * bi
    bbar_i = coef_r[..., None] * bi + coef_i[..., None] * br
    cr = c_re.astype(F32)
    ci = c_im.astype(F32)

    pr, pi = [jnp.ones_like(ar)], [jnp.zeros_like(ai)]
    for _ in range(S5_L):
        r, i = pr[-1], pi[-1]
        pr.append(r * ar - i * ai)
        pi.append(r * ai + i * ar)
    pr = jnp.stack(pr)
    pi = jnp.stack(pi)

    ncb, gb = S5_NCB, S5_GB
    pw_r = pr[S5_L - 1::-1]
    pw_i = pi[S5_L - 1::-1]
    e_r = pw_r[..., None] * bbar_r[None] - pw_i[..., None] * bbar_i[None]
    e_i = pw_r[..., None] * bbar_i[None] + pw_i[..., None] * bbar_r[None]
    p_blk = jnp.stack([jnp.transpose(e_r, (0, 1, 3, 2)), jnp.transpose(e_i, (0, 1, 3, 2))], axis=3)
    p_blk = p_blk.reshape(S5_L, ncb, gb * S5_GROUP, 2 * S5_STATE)
    p_blk = jnp.transpose(p_blk, (1, 0, 2, 3)).reshape(ncb, S5_BROW, 2 * S5_STATE)

    m_r = pr[:S5_L, :, None, :] * cr[None] - pi[:S5_L, :, None, :] * ci[None]
    m_i = pr[:S5_L, :, None, :] * ci[None] + pi[:S5_L, :, None, :] * cr[None]
    kern = (jnp.sum(m_r[:, :, :, :, None] * bbar_r[None, :, None, :, :], axis=3)
            - jnp.sum(m_i[:, :, :, :, None] * bbar_i[None, :, None, :, :], axis=3))
    t_blk = jnp.transpose(kern, (0, 1, 3, 2)).reshape(S5_L, ncb, S5_CB, S5_GROUP)
    t_blk = jnp.transpose(t_blk, (1, 0, 2, 3))

    q_r = pr[1:, :, None, :] * cr[None] - pi[1:, :, None, :] * ci[None]
    q_i = -(pr[1:, :, None, :] * ci[None] + pi[1:, :, None, :] * cr[None])
    q_blk = jnp.stack([q_r, q_i], axis=0).reshape(2, S5_L, ncb, gb, S5_GROUP, S5_STATE)
    q_blk = jnp.transpose(q_blk, (2, 1, 4, 0, 3, 5))
    q_blk = q_blk.reshape(ncb, S5_L, S5_GROUP, S5_BSTATE)

    half_blocks = S5_BSTATE // 2 // LANES
    a_r = pr[S5_L].reshape(ncb, half_blocks, 1, LANES)
    a_i = pi[S5_L].reshape(ncb, half_blocks, 1, LANES)
    d_row = jnp.tile(d_skip.astype(F32).reshape(ncb, 1, S5_CB), (1, 1, S5_L))
    pt, q_mat = _s5_expand(p_blk.astype(BF16), t_blk.astype(BF16), q_blk.astype(BF16))
    return pt, q_mat, a_r, a_i, d_row


def _s5_expand_kernel(p_ref, t_ref, q_ref, pt_ref, qm_ref):
    def iota(shape, dim):
        return lax.broadcasted_iota(jnp.int32, shape, dim)

    def onehot(rows, cols, row_key, col_key):
        r, c = iota((rows, cols), 0), iota((rows, cols), 1)
        return jnp.where(row_key(r) == col_key(c), 1.0, 0.0)

    half = S5_BSTATE // 2
    rep = onehot(2 * S5_STATE, S5_BSTATE, lambda r: r,
                 lambda c: (c // half) * S5_STATE + c % S5_STATE).astype(BF16)
    same = onehot(S5_CB, S5_BSTATE, lambda r: r // S5_GROUP, lambda c: (c % half) // S5_STATE)
    rep_c = onehot(S5_GROUP, S5_CB, lambda r: r, lambda c: c % S5_GROUP).astype(BF16)
    same_g = onehot(S5_CB, S5_CB, lambda r: r // S5_GROUP, lambda c: c // S5_GROUP)
    same_q = onehot(S5_BSTATE, S5_CB, lambda r: (r % half) // S5_STATE, lambda c: c // S5_GROUP)
    lag_blk = [(jnp.dot(t_ref[d], rep_c, preferred_element_type=F32) * same_g).astype(BF16)
               for d in range(S5_L)]
    zero_blk = jnp.zeros((S5_CB, S5_CB), BF16)
    for s in range(S5_L):
        rows = slice(s * S5_CB, (s + 1) * S5_CB)
        p_full = jnp.dot(p_ref[rows, :], rep, preferred_element_type=F32) * same
        pt_ref[rows, :S5_BSTATE] = p_full.astype(BF16)
        for t in range(S5_L):
            cols = slice(S5_BSTATE + t * S5_CB, S5_BSTATE + (t + 1) * S5_CB)
            pt_ref[rows, cols] = lag_blk[t - s] if t >= s else zero_blk
    for t in range(S5_L):
        q_full = lax.dot_general(q_ref[t], rep_c, (((0,), (0,)), ((), ())),
                                 preferred_element_type=F32) * same_q
        qm_ref[:, t * S5_CB:(t + 1) * S5_CB] = q_full.astype(BF16)


def _s5_expand(p_blk, t_blk, q_blk):
    def blk(shape):
        return pl.BlockSpec((None,) + shape, lambda cb: (cb,) + (0,) * len(shape))
    return pl.pallas_call(
        _s5_expand_kernel,
        out_shape=(jax.ShapeDtypeStruct((S5_NCB, S5_BROW, S5_BSTATE + S5_BROW), BF16),
                   jax.ShapeDtypeStruct((S5_NCB, S5_BSTATE, S5_BROW), BF16)),
        grid=(S5_NCB,),
        in_specs=[blk((S5_BROW, 2 * S5_STATE)), blk((S5_L, S5_CB, S5_GROUP)),
                  blk((S5_L, S5_GROUP, S5_BSTATE))],
        out_specs=(blk((S5_BROW, S5_BSTATE + S5_BROW)), blk((S5_BSTATE, S5_BROW))),
        compiler_params=pltpu.CompilerParams(
            dimension_semantics=("arbitrary",), vmem_limit_bytes=VMEM_LIMIT),
        name="s5_tables",
    )(p_blk, t_blk, q_blk)


def _s5_kernel(ua_ref, ub_ref, pt_ref, q_ref, ar_ref, ai_ref, d_ref, wg_ref, o_ref,
               sr_ref, si_ref, u8_ref, es_ref, ot_ref):
    nb = ua_ref.shape[0]
    ncl = S5_TOK // S5_L
    nlb = S5_BSTATE // LANES
    hlb = nlb // 2
    u_refs = (ua_ref, ub_ref)
    low = lax.broadcasted_iota(jnp.int32, (1, LANES), 1) < S5_CB

    def interleave(a, b):
        return (jnp.where(low, a, pltpu.roll(b, S5_CB, axis=1)),
                jnp.where(low, pltpu.roll(a, S5_CB, axis=1), b))

    @pl.when(pl.program_id(0) == 0)
    def _():
        sr_ref[...] = jnp.zeros_like(sr_ref)
        si_ref[...] = jnp.zeros_like(si_ref)

    sub = 8
    ncg = ncl // sub
    grp = nb * sub
    for hh, u_ref in enumerate(u_refs):
        for t2 in range(S5_L // 2):
            for cg in range(ncg):
                tok = [u_ref[:, pl.ds(cg * sub * S5_L + 2 * t2 + k, sub, stride=S5_L), :]
                       .reshape(grp, LANES) for k in range(2)]
                for k, blk in enumerate(interleave(*tok)):
                    u8_ref[2 * hh + k, cg * grp:(cg + 1) * grp, t2 * LANES:(t2 + 1) * LANES] = blk

    for cb in range(S5_NCB):
        e = jnp.dot(u8_ref[cb].astype(BF16), pt_ref[cb, :, :S5_BSTATE],
                    preferred_element_type=F32)
        for j in range(nlb):
            es_ref[cb, j] = e[:, j * LANES:(j + 1) * LANES]

    a_r = ar_ref[...]
    a_i = ai_ref[...]
    sr = sr_ref[...]
    si = si_ref[...]
    for cl in range(ncl):
        step = pl.ds((cl // sub) * grp + cl % sub, nb, stride=sub)
        e_r = es_ref[:, :hlb, step, :]
        e_i = es_ref[:, hlb:, step, :]
        es_ref[:, :hlb, step, :] = sr
        es_ref[:, hlb:, step, :] = si
        sr, si = a_r * sr - a_i * si + e_r, a_r * si + a_i * sr + e_i
    sr_ref[...] = sr
    si_ref[...] = si

    ys = []
    for cb in range(S5_NCB):
        u8 = u8_ref[cb]
        s_all = jnp.concatenate([es_ref[cb, j] for j in range(nlb)], axis=1).astype(BF16)
        y = (jnp.dot(u8.astype(BF16), pt_ref[cb, :, S5_BSTATE:], preferred_element_type=F32)
             + jnp.dot(s_all, q_ref[cb], preferred_element_type=F32) + d_ref[cb] * u8)
        ys.append(jax.nn.gelu(y))
    for t2 in range(S5_L // 2):
        lanes = slice(t2 * LANES, (t2 + 1) * LANES)
        halves = [interleave(ys[2 * hh][:, lanes], ys[2 * hh + 1][:, lanes]) for hh in range(2)]
        for k in range(2):
            t = 2 * t2 + k
            yt = jnp.concatenate([halves[0][k], halves[1][k]], axis=1).astype(BF16)
            ag = jnp.dot(yt, wg_ref[...], preferred_element_type=F32)
            out = ag[:, :D_S5] * _sigmoid(ag[:, D_S5:])
            for hh in range(D_S5 // LANES):
                for cg in range(ncg):
                    ot_ref[hh, :, pl.ds(cg * sub * S5_L + t, sub, stride=S5_L), :] = (
                        out[cg * grp:(cg + 1) * grp, hh * LANES:(hh + 1) * LANES].reshape(nb, sub, LANES))
    o_ref[...] = jnp.concatenate([ot_ref[0], ot_ref[1]], axis=-1).astype(BF16)


def _s5(u, pt, q_mat, a_r, a_i, d_row, w_glu):
    nb, seq, _ = u.shape
    rows = nb * (S5_TOK // S5_L)
    nlb = S5_BSTATE // LANES
    assert 2 * S5_CB == LANES and D_S5 == 2 * LANES
    return pl.pallas_call(
        _s5_kernel,
        out_shape=jax.ShapeDtypeStruct((nb, seq, D_S5), BF16),
        grid=(seq // S5_TOK,),
        in_specs=[pl.BlockSpec((nb, S5_TOK, LANES), lambda i: (0, i, 0)),
                  pl.BlockSpec((nb, S5_TOK, LANES), lambda i: (0, i, 1)),
                  _resident(pt.shape), _resident(q_mat.shape), _resident(a_r.shape),
                  _resident(a_i.shape), _resident(d_row.shape), _resident(w_glu.shape)],
        out_specs=pl.BlockSpec((nb, S5_TOK, D_S5), lambda i: (0, i, 0)),
        scratch_shapes=[pltpu.VMEM((S5_NCB, nlb // 2, nb, LANES), F32),
                        pltpu.VMEM((S5_NCB, nlb // 2, nb, LANES), F32),
                        pltpu.VMEM((S5_NCB, rows, S5_BROW), F32),
                        pltpu.VMEM((S5_NCB, nlb, rows, LANES), F32),
                        pltpu.VMEM((D_S5 // LANES, nb, S5_TOK, LANES), F32)],
        compiler_params=pltpu.CompilerParams(
            dimension_semantics=("arbitrary",), vmem_limit_bytes=VMEM_LIMIT),
        name="s5",
    )(u, u, pt, q_mat, a_r, a_i, d_row, w_glu)


def _attn_bias_vec(rel_bias):
    rb = rel_bias.astype(F32) * LOG2E
    far_past, far_future = rb[:, 2 * MAX_REL:], rb[:, :1]
    n_const = N_LEFT * CHUNK - MAX_REL
    return jnp.concatenate([
        jnp.broadcast_to(far_past, (HEADS, n_const)),
        rb[:, ::-1],
        jnp.broadcast_to(far_future, (HEADS, ATT_W - n_const - 2 * MAX_REL - 1)),
        jnp.broadcast_to(far_past, (HEADS, ATT_PERIOD - ATT_W)),
    ], axis=1)


def _attn_build_bias(vec_ref, bm_ref):
    r = lax.broadcasted_iota(jnp.int32, (ATT_QB, ATT_W), 0)
    c = lax.broadcasted_iota(jnp.int32, (ATT_QB, ATT_W), 1)
    dchunk = r // CHUNK + N_LEFT - c // CHUNK
    ok = (dchunk >= 0) & (dchunk <= N_LEFT)
    for h in range(HEADS):
        base = jnp.broadcast_to(vec_ref[h:h + 1, :], (ATT_QB, ATT_PERIOD))
        toep = pltpu.roll(base, 0, 1, stride=1, stride_axis=0)[:, :ATT_W]
        bm_ref[h // 2, h % 2] = jnp.where(ok, toep, -1e30)


def _attn_kernel(q_ref, kt_ref, v_ref, vec_ref, o_ref, bm_ref):
    seq = q_ref.shape[1]
    lane = lax.broadcasted_iota(jnp.int32, (1, LANES), 1)
    first = lane < HEAD_DIM

    @pl.when(pl.program_id(0) == 0)
    def _():
        _attn_build_bias(vec_ref, bm_ref)

    def block(q0, k0, width):
        npair = D_ATT // LANES
        outs = [[None, None] for _ in range(npair)]
        for hh in range(2):
            sel = first if hh == 0 else jnp.logical_not(first)
            for pr in range(npair):
                cols = slice(pr * LANES, (pr + 1) * LANES)
                q2 = q_ref[0, pl.ds(q0, ATT_QB), cols]
                qh = jnp.where(sel, q2, jnp.zeros_like(q2))
                for c0 in range(0, width, ATT_KT):
                    kt = kt_ref[0, cols, pl.ds(k0 + c0, ATT_KT)]
                    vw = v_ref[0, pl.ds(k0 + c0, ATT_KT), cols]
                    s = jnp.dot(qh, kt, preferred_element_type=F32)
                    b0 = ATT_W - width + c0
                    s = s + bm_ref[pr, hh, :, b0:b0 + ATT_KT]
                    vh = jnp.where(sel, vw, jnp.ones_like(vw))
                    if c0 == 0:
                        mx = jnp.max(s, axis=-1, keepdims=True)
                        acc = jnp.dot(jnp.exp2(s - mx).astype(BF16), vh, preferred_element_type=F32)
                    else:
                        mx_new = jnp.maximum(mx, jnp.max(s, axis=-1, keepdims=True))
                        p = jnp.exp2(s - mx_new).astype(BF16)
                        acc = acc * jnp.exp2(mx - mx_new) + jnp.dot(p, vh, preferred_element_type=F32)
                        mx = mx_new
                outs[pr][hh] = acc
        for pr, (acc_a, acc_b) in enumerate(outs):
            num = jnp.where(first, acc_a, acc_b)
            den = pltpu.roll(jnp.where(first, acc_b, acc_a), HEAD_DIM, axis=1)
            o_ref[0, pl.ds(q0, ATT_QB), pr * LANES:(pr + 1) * LANES] = (num / den).astype(BF16)

    n_short = N_LEFT * CHUNK // ATT_QB
    for qb in range(n_short):
        block(qb * ATT_QB, 0, (qb + 1) * ATT_QB)

    def body(qb, carry):
        q0 = pl.multiple_of(qb * ATT_QB, ATT_QB)
        k0 = pl.multiple_of(q0 - N_LEFT * CHUNK, ATT_QB)
        block(q0, k0, ATT_W)
        return carry
    lax.fori_loop(n_short, seq // ATT_QB, body, 0)


def _attn(q, kt, v, bias_vec):
    nb, seq, _ = q.shape
    seqblk = pl.BlockSpec((1, seq, D_ATT), lambda b: (b, 0, 0))
    return pl.pallas_call(
        _attn_kernel,
        out_shape=jax.ShapeDtypeStruct((nb, seq, D_ATT), BF16),
        grid=(nb,),
        in_specs=[seqblk, pl.BlockSpec((1, D_ATT, seq), lambda b: (b, 0, 0)), seqblk,
                  _resident(bias_vec.shape)],
        out_specs=seqblk,
        scratch_shapes=[pltpu.VMEM((HEADS // 2, 2, ATT_QB, ATT_W), F32)],
        compiler_params=pltpu.CompilerParams(
            dimension_semantics=("arbitrary",), vmem_limit_bytes=VMEM_LIMIT),
        name="attn",
    )(q, kt, v, bias_vec)


def _conv_taps(hc_ref, cw_ref, cb_ref, cbuf_ref, first_tile):
    nh = D_CONV // LANES
    tm = hc_ref.shape[0]

    @pl.when(first_tile)
    def _():
        cbuf_ref[:, :CONV_PAD, :] = jnp.zeros((nh, CONV_PAD, LANES), F32)
    for hh in range(nh):
        cbuf_ref[hh, CONV_PAD:, :] = hc_ref[:, hh * LANES:(hh + 1) * LANES]
    shift = CONV_PAD - (CONV_WIDTH - 1)
    halves = []
    for hh in range(nh):
        tiles = []
        for r0 in range(0, tm, CONV_TR):
            acc = jnp.zeros((CONV_TR, LANES), F32) + cb_ref[hh]
            for j in range(CONV_WIDTH):
                lo = r0 + shift + j
                acc = acc + cw_ref[hh, j:j + 1, :] * cbuf_ref[hh, lo:lo + CONV_TR, :]
            tiles.append(acc)
        halves.append(jnp.concatenate(tiles, axis=0))
    for hh in range(nh):
        cbuf_ref[hh, :CONV_PAD, :] = cbuf_ref[hh, tm:tm + CONV_PAD, :]
    return halves


def _conv_norm(halves, lg_ref, lb_ref):
    acc = jnp.concatenate(halves, axis=-1)
    mu = jnp.mean(acc, axis=-1, keepdims=True)
    cen = acc - mu
    var = jnp.mean(cen * cen, axis=-1, keepdims=True)
    y = (cen * lax.rsqrt(var + EPS)) * lg_ref[...] + lb_ref[...]
    return (y * _sigmoid(y)).astype(BF16)


def _after(v, zero_ref, width):
    bits = pltpu.bitcast(v, jnp.uint32)
    acc = bits[0:8, :]
    for r0 in range(8, v.shape[0], 8):
        acc = acc | bits[r0:r0 + 8, :]
    zero = pltpu.bitcast(acc[0:1, :] & zero_ref[...], F32)
    return jnp.tile(zero, (1, width // LANES)).astype(BF16)


def _merge_kernel(x_ref, s5_ref, at_ref, hc_ref, g_ref, wg_ref, bg_ref, ws_ref, wa_ref, wc_ref, wo_ref,
                  cw_ref, cb_ref, lg_ref, lb_ref, zero_ref, *refs, tiles_per_seq, cast_nblks):
    n_cast = len(cast_nblks)
    o_ref, cbuf_ref = refs[n_cast], refs[-1]
    _Casts.run_predicated(pl.program_id(0), cast_nblks, refs[:n_cast], refs[n_cast + 1:-1])
    halves = _conv_taps(hc_ref, cw_ref, cb_ref, cbuf_ref, pl.program_id(0) % tiles_per_seq == 0)
    _Casts.run_every_step(cast_nblks, refs[:n_cast], refs[n_cast + 1:-1])
    x = x_ref[...]
    h = _rmsnorm_bf16(x, g_ref[...])
    lhs = (h, h + _after(halves[0], zero_ref, D_MODEL), h + _after(halves[1], zero_ref, D_MODEL))
    branches = (lambda: s5_ref[...], lambda: at_ref[...], lambda: _conv_norm(halves, lg_ref, lb_ref))
    merged = jnp.zeros_like(x)
    for i, (branch, w_ref) in enumerate(zip(branches, (ws_ref, wa_ref, wc_ref))):
        cols = slice(i * D_MODEL, (i + 1) * D_MODEL)
        gcols = slice(N_SMALL + i * D_MODEL, N_SMALL + (i + 1) * D_MODEL)
        logits = jnp.dot(lhs[i], wg_ref[:, gcols], preferred_element_type=F32) + bg_ref[:, cols]
        y = jnp.dot(branch(), w_ref[...], preferred_element_type=F32)
        merged = merged + _sigmoid(logits) * y
    o_ref[...] = x + jnp.dot(merged.astype(BF16), wo_ref[...], preferred_element_type=F32)


def _merge(x, s5o, ato, hc, g, w_in, b_gate, w_s5, w_at, w_cv, w_out, w_dw, b_dw, ln_g, ln_b, seq,
           cast_items):
    n = x.shape[0]
    nh = D_CONV // LANES
    zero = jnp.zeros((1, LANES), jnp.uint32)
    assert seq % TM == 0
    steps = n // TM
    casts = _Casts(cast_items, steps)
    def tok(width):
        return pl.BlockSpec((TM, width), lambda i: (i, 0))
    return pl.pallas_call(
        functools.partial(_merge_kernel, tiles_per_seq=seq // TM, cast_nblks=tuple(casts.nblks)),
        out_shape=(jax.ShapeDtypeStruct((n, D_MODEL), F32), *casts.out_shape),
        grid=(steps,),
        in_specs=[tok(D_MODEL), tok(D_S5), tok(D_ATT), tok(D_CONV),
                  _resident((1, D_MODEL)), _resident(w_in.shape),
                  _resident((1, 3 * D_MODEL)), _resident((D_S5, D_MODEL)),
                  _resident((D_ATT, D_MODEL)), _resident((D_CONV, D_MODEL)),
                  _resident((D_MODEL, D_MODEL)), _resident((nh, CONV_WIDTH, LANES)),
                  _resident((nh, 1, LANES)), _resident((1, D_CONV)), _resident((1, D_CONV)),
                  _resident((1, LANES)), *casts.in_specs],
        out_specs=(tok(D_MODEL), *casts.out_specs),
        scratch_shapes=[pltpu.VMEM((nh, CONV_PAD + TM, LANES), F32)],
        compiler_params=pltpu.CompilerParams(
            dimension_semantics=("arbitrary",), vmem_limit_bytes=VMEM_LIMIT),
        name="merge",
    )(x, s5o, ato, hc, g, w_in, b_gate, w_s5, w_at, w_cv, w_out, w_dw, b_dw, ln_g, ln_b, zero,
      *casts.args)


def kernel(x, ffn1_norm, ffn1_w_up, ffn1_w_down, mix_norm, w_in, b_gate, s5_lambda_re, s5_lambda_im, s5_log_dt, s5_b_re, s5_b_im, s5_c_re, s5_c_im, s5_d, s5_w_glu, w_br_s5, attn_q_gain, attn_k_gain, attn_rel_bias, w_br_attn, conv_w_dw, conv_b_dw, conv_ln_g, conv_ln_b, w_br_conv, w_out, ffn2_norm, ffn2_w_up, ffn2_w_down):
    nb, seq, d = x.shape
    n = nb * seq
    depth = ffn1_norm.shape[0]
    xt = x.reshape(n, d)
    row = lambda v: v.reshape(1, -1).astype(F32)
    f1_up, f1_down, w_in_l = (ffn1_w_up[0].astype(BF16), ffn1_w_down[0].astype(BF16),
                              w_in[0].astype(BF16))
    for l in range(depth):
        xt = _ffn(xt, row(ffn1_norm[l]), f1_up, f1_down)

        qg2 = jnp.tile(row(attn_q_gain[l]), (1, LANES // HEAD_DIM))
        kg2 = jnp.tile(row(attn_k_gain[l]), (1, LANES // HEAD_DIM))
        own = (ffn2_w_up, ffn2_w_down, w_br_s5, w_br_attn, w_br_conv, w_out, s5_w_glu)
        u, q, kt, v, hc, f2_up, f2_down, w_s5, w_at, w_cv, w_o, w_glu = _inproj(
            xt, row(mix_norm[l]), w_in_l, qg2, kg2, seq, [(w, l) for w in own])

        pt, q_mat, a_r, a_i, d_row = _s5_tables(
            s5_lambda_re[l], s5_lambda_im[l], s5_log_dt[l], s5_b_re[l], s5_b_im[l],
            s5_c_re[l], s5_c_im[l], s5_d[l])
        s5o = _s5(u.reshape(nb, seq, D_S5), pt, q_mat, a_r, a_i, d_row, w_glu)

        ato = _attn(q.reshape(nb, seq, D_ATT), kt, v.reshape(nb, seq, D_ATT),
                    _attn_bias_vec(attn_rel_bias[l]))

        nh = D_CONV // LANES
        w_dw = conv_w_dw[l].astype(F32).reshape(CONV_WIDTH, nh, LANES).transpose(1, 0, 2)
        nxt = (ffn1_w_up, ffn1_w_down, w_in) if l + 1 < depth else ()
        xt, *nxt_bf16 = _merge(
            xt, s5o.reshape(n, D_S5), ato.reshape(n, D_ATT), hc, row(mix_norm[l]), w_in_l,
            row(b_gate[l]), w_s5, w_at, w_cv, w_o, w_dw,
            conv_b_dw[l].astype(F32).reshape(nh, 1, LANES), row(conv_ln_g[l]), row(conv_ln_b[l]),
            seq, [(w, l + 1) for w in nxt])
        if nxt_bf16:
            f1_up, f1_down, w_in_l = nxt_bf16

        xt = _ffn(xt, row(ffn2_norm[l]), f2_up, f2_down)
    return xt.reshape(nb, seq, d)
```

```python
import functools
import math

import jax
import jax.numpy as jnp
from jax import lax
from jax.experimental import pallas as pl
from jax.experimental.pallas import tpu as pltpu

F32 = jnp.float32
BF16 = jnp.bfloat16

D_MODEL = 1024
D_S5 = 256
S5_GROUP = 16
S5_GROUPS = 16
S5_STATE = 64
D_ATT = 512
HEAD_DIM = 64
HEADS = 8
CHUNK = 64
N_LEFT = 8
MAX_REL = 128
D_CONV = 256
CONV_WIDTH = 31
D_FF = 2816
EPS = 1e-6
LOG2E = math.log2(math.e)
N_SMALL = D_S5 + 3 * D_ATT + 2 * D_CONV

LANES = 128
MXU_DIM = 256
VMEM_LIMIT = 56 * 1024 * 1024

TM = 1024
TM_WIDE = 1024
CAST_MIN_ROWS = 16
CAST_ROWS = 64
S5_L = 8
S5_CB = LANES // 2
S5_GB = S5_CB // S5_GROUP
S5_NCB = D_S5 // S5_CB
S5_BROW = S5_L * S5_CB
S5_BSTATE = 2 * S5_GB * S5_STATE
S5_TOK = 256
ATT_QB = 4 * CHUNK
ATT_W = ATT_QB + N_LEFT * CHUNK
ATT_KT = MXU_DIM
ATT_PERIOD = ATT_QB + ATT_W
CONV_TR = 64
CONV_PAD = 32


def _resident(shape):
    nd = len(shape)
    return pl.BlockSpec(shape, lambda *_: (0,) * nd, pipeline_mode=pl.Buffered(1))


def _rmsnorm_bf16(x, g):
    ms = jnp.mean(x * x, axis=-1, keepdims=True)
    return ((x * lax.rsqrt(ms + EPS)) * g).astype(BF16)


def _sigmoid(x):
    return jax.nn.sigmoid(x)


class _Casts:
    def __init__(self, items, steps):
        self.in_specs, self.out_specs, self.out_shape, self.args, self.nblks = [], [], [], [], []
        for stacked, layer in items:
            _, rows, cols = stacked.shape
            every_step = rows % CAST_MIN_ROWS == 0 and rows // CAST_MIN_ROWS <= steps
            rb = CAST_MIN_ROWS if every_step else CAST_ROWS
            nblk = rows // rb
            assert nblk * rb == rows and nblk <= steps
            self.nblks.append(None if every_step else nblk)
            self.in_specs.append(pl.BlockSpec(
                (None, rb, cols),
                lambda i, layer=layer, nblk=nblk: (layer, jnp.minimum(i, nblk - 1), 0)))
            self.out_specs.append(pl.BlockSpec(
                (rb, cols), lambda i, nblk=nblk: (jnp.minimum(i, nblk - 1), 0)))
            self.out_shape.append(jax.ShapeDtypeStruct((rows, cols), BF16))
            self.args.append(stacked)

    @staticmethod
    def run_predicated(step, nblks, src_refs, dst_refs):
        for nblk in sorted({n for n in nblks if n is not None}):
            @pl.when(step < nblk)
            def _():
                for n, src, dst in zip(nblks, src_refs, dst_refs):
                    if n == nblk:
                        dst[...] = src[...].astype(BF16)

    @staticmethod
    def run_every_step(nblks, src_refs, dst_refs):
        for n, src, dst in zip(nblks, src_refs, dst_refs):
            if n is None:
                dst[...] = src[...].astype(BF16)


def _ffn_kernel(x_ref, g_ref, wa_ref, wb_ref, wd_ref, o_ref, *, bounds):
    x = x_ref[...]
    h = _rmsnorm_bf16(x, g_ref[...])
    y = jnp.zeros_like(x)
    for f0, f1 in zip(bounds[:-1], bounds[1:]):
        a = jnp.dot(h, wa_ref[:, f0:f1], preferred_element_type=F32)
        b = jnp.dot(h, wb_ref[:, f0:f1], preferred_element_type=F32)
        act = ((a * _sigmoid(a)) * b).astype(BF16)
        y = y + jnp.dot(act, wd_ref[f0:f1, :], preferred_element_type=F32)
    o_ref[...] = x + 0.5 * y


def _ffn_bounds():
    tiles = D_FF // MXU_DIM
    assert tiles * MXU_DIM == D_FF
    return (0, (tiles + 1) // 2 * MXU_DIM, D_FF)


def _ffn(x, g, w_up, w_down):
    n = x.shape[0]
    tok = pl.BlockSpec((TM_WIDE, D_MODEL), lambda i: (i, 0))
    def up_half(j):
        return pl.BlockSpec((D_MODEL, D_FF), lambda i: (0, j), pipeline_mode=pl.Buffered(1))
    return pl.pallas_call(
        functools.partial(_ffn_kernel, bounds=_ffn_bounds()),
        out_shape=jax.ShapeDtypeStruct((n, D_MODEL), F32),
        grid=(n // TM_WIDE,),
        in_specs=[tok, _resident((1, D_MODEL)), up_half(0), up_half(1), _resident((D_FF, D_MODEL))],
        out_specs=tok,
        compiler_params=pltpu.CompilerParams(
            dimension_semantics=("arbitrary",), vmem_limit_bytes=VMEM_LIMIT),
        name="ffn",
    )(x, g, w_up, w_up, w_down)


def _head_pair_norm(x2, gain2):
    lane = lax.broadcasted_iota(jnp.int32, (1, LANES), 1)
    first = lane < HEAD_DIM
    sq = x2 * x2
    s_a = jnp.sum(jnp.where(first, sq, 0.0), axis=-1, keepdims=True)
    s_b = jnp.sum(jnp.where(first, 0.0, sq), axis=-1, keepdims=True)
    ms = jnp.where(first, s_a, s_b) * (1.0 / HEAD_DIM)
    return (x2 * lax.rsqrt(ms + EPS)) * gain2


def _inproj_kernel(x_ref, g_ref, w_ref, qg_ref, kg_ref, *refs, cast_nblks):
    n_cast = len(cast_nblks)
    u_ref, q_ref, kt_ref, v_ref, hc_ref = refs[n_cast:n_cast + 5]
    _Casts.run_predicated(pl.program_id(0), cast_nblks, refs[:n_cast], refs[n_cast + 5:])
    _Casts.run_every_step(cast_nblks, refs[:n_cast], refs[n_cast + 5:])
    h = _rmsnorm_bf16(x_ref[...], g_ref[...])
    proj = jnp.dot(h, w_ref[...], preferred_element_type=F32)
    u_ref[...] = proj[:, :D_S5]
    q0, k0, v0, z0 = D_S5, D_S5 + D_ATT, D_S5 + 2 * D_ATT, D_S5 + 3 * D_ATT
    scale = HEAD_DIM ** -0.5 * LOG2E
    for p in range(D_ATT // LANES):
        lo = p * LANES
        qn = _head_pair_norm(proj[:, q0 + lo:q0 + lo + LANES], qg_ref[...])
        kn = _head_pair_norm(proj[:, k0 + lo:k0 + lo + LANES], kg_ref[...])
        q_ref[:, lo:lo + LANES] = (qn * scale).astype(BF16)
        kt_ref[lo:lo + LANES, :] = kn.T.astype(BF16)
    v_ref[...] = proj[:, v0:v0 + D_ATT].astype(BF16)
    a = proj[:, z0:z0 + D_CONV]
    gt = proj[:, z0 + D_CONV:z0 + 2 * D_CONV]
    hc_ref[...] = a * _sigmoid(gt)


def _inproj(x, g, w_in, qg2, kg2, seq, cast_items):
    n = x.shape[0]
    steps = n // TM_WIDE
    tiles_per_seq = seq // TM_WIDE
    assert tiles_per_seq * TM_WIDE == seq
    casts = _Casts(cast_items, steps)
    def tok(width):
        return pl.BlockSpec((TM_WIDE, width), lambda i: (i, 0))
    w_small = pl.BlockSpec((D_MODEL, N_SMALL), lambda i: (0, 0), pipeline_mode=pl.Buffered(1))
    kt_spec = pl.BlockSpec((None, D_ATT, TM_WIDE),
                           lambda i: (i // tiles_per_seq, 0, i % tiles_per_seq))
    return pl.pallas_call(
        functools.partial(_inproj_kernel, cast_nblks=tuple(casts.nblks)),
        out_shape=(jax.ShapeDtypeStruct((n, D_S5), F32),
                   jax.ShapeDtypeStruct((n, D_ATT), BF16),
                   jax.ShapeDtypeStruct((n // seq, D_ATT, seq), BF16),
                   jax.ShapeDtypeStruct((n, D_ATT), BF16),
                   jax.ShapeDtypeStruct((n, D_CONV), F32), *casts.out_shape),
        grid=(steps,),
        in_specs=[tok(D_MODEL), _resident((1, D_MODEL)), w_small,
                  _resident((1, LANES)), _resident((1, LANES)), *casts.in_specs],
        out_specs=(tok(D_S5), tok(D_ATT), kt_spec, tok(D_ATT), tok(D_CONV), *casts.out_specs),
        compiler_params=pltpu.CompilerParams(
            dimension_semantics=("arbitrary",), vmem_limit_bytes=VMEM_LIMIT),
        name="inproj",
    )(x, g, w_in, qg2, kg2, *casts.args)


def _s5_tables(lambda_re, lambda_im, log_dt, b_re, b_im, c_re, c_im, d_skip):
    lr = jnp.minimum(lambda_re.astype(F32), -1e-4)
    li = lambda_im.astype(F32)
    dt = jnp.exp(log_dt.astype(F32))[:, None]
    mag = jnp.exp(lr * dt)
    ar = mag * jnp.cos(li * dt)
    ai = mag * jnp.sin(li * dt)
    den = lr * lr + li * li
    coef_r = ((ar - 1.0) * lr + ai * li) / den
    coef_i = (ai * lr - (ar - 1.0) * li) / den
    br = b_re.astype(F32)
    bi = b_im.astype(F32)
    bbar_r = coef_r[..., None] * br - coef_i[..., None] * bi
    bbar_i = coef_r[..., None] * bi + coef_i[..., None] * br
    cr = c_re.astype(F32)
    ci = c_im.astype(F32)

    pr, pi = [jnp.ones_like(ar)], [jnp.zeros_like(ai)]
    for _ in range(S5_L):
        r, i = pr[-1], pi[-1]
        pr.append(r * ar - i * ai)
        pi.append(r * ai + i * ar)
    pr = jnp.stack(pr)
    pi = jnp.stack(pi)

    ncb, gb = S5_NCB, S5_GB
    pw_r = pr[S5_L - 1::-1]
    pw_i = pi[S5_L - 1::-1]
    e_r = pw_r[..., None] * bbar_r[None] - pw_i[..., None] * bbar_i[None]
    e_i = pw_r[..., None] * bbar_i[None] + pw_i[..., None] * bbar_r[None]
    p_blk = jnp.stack([jnp.transpose(e_r, (0, 1, 3, 2)), jnp.transpose(e_i, (0, 1, 3, 2))], axis=3)
    p_blk = p_blk.reshape(S5_L, ncb, gb * S5_GROUP, 2 * S5_STATE)
    p_blk = jnp.transpose(p_blk, (1, 0, 2, 3)).reshape(ncb, S5_BROW, 2 * S5_STATE)

    m_r = pr[:S5_L, :, None, :] * cr[None] - pi[:S5_L, :, None, :] * ci[None]
    m_i = pr[:S5_L, :, None, :] * ci[None] + pi[:S5_L, :, None, :] * cr[None]
    kern = (jnp.sum(m_r[:, :, :, :, None] * bbar_r[None, :, None, :, :], axis=3)
            - jnp.sum(m_i[:, :, :, :, None] * bbar_i[None, :, None, :, :], axis=3))
    t_blk = jnp.transpose(kern, (0, 1, 3, 2)).reshape(S5_L, ncb, S5_CB, S5_GROUP)
    t_blk = jnp.transpose(t_blk, (1, 0, 2, 3))

    q_r = pr[1:, :, None, :] * cr[None] - pi[1:, :, None, :] * ci[None]
    q_i = -(pr[1:, :, None, :] * ci[None] + pi[1:, :, None, :] * cr[None])
    q_blk = jnp.stack([q_r, q_i], axis=0).reshape(2, S5_L, ncb, gb, S5_GROUP, S5_STATE)
    q_blk = jnp.transpose(q_blk, (2, 1, 4, 0, 3, 5))
    q_blk = q_blk.reshape(ncb, S5_L, S5_GROUP, S5_BSTATE)

    half_blocks = S5_BSTATE // 2 // LANES
    a_r = pr[S5_L].reshape(ncb, half_blocks, 1, LANES)
    a_i = pi[S5_L].reshape(ncb, half_blocks, 1, LANES)
    d_row = jnp.tile(d_skip.astype(F32).reshape(ncb, 1, S5_CB), (1, 1, S5_L))
    return (p_blk.astype(BF16), t_blk.astype(BF16), q_blk.astype(BF16)), a_r, a_i, d_row


def _s5_expand_block(p_ref, t_ref, q_ref, pt_ref, qm_ref):
    def iota(shape, dim):
        return lax.broadcasted_iota(jnp.int32, shape, dim)

    def onehot(rows, cols, row_key, col_key):
        r, c = iota((rows, cols), 0), iota((rows, cols), 1)
        return jnp.where(row_key(r) == col_key(c), 1.0, 0.0)

    half = S5_BSTATE // 2
    rep = onehot(2 * S5_STATE, S5_BSTATE, lambda r: r,
                 lambda c: (c // half) * S5_STATE + c % S5_STATE).astype(BF16)
    same = onehot(S5_CB, S5_BSTATE, lambda r: r // S5_GROUP, lambda c: (c % half) // S5_STATE)
    rep_c = onehot(S5_GROUP, S5_CB, lambda r: r, lambda c: c % S5_GROUP).astype(BF16)
    same_g = onehot(S5_CB, S5_CB, lambda r: r // S5_GROUP, lambda c: c // S5_GROUP)
    same_q = onehot(S5_BSTATE, S5_CB, lambda r: (r % half) // S5_STATE, lambda c: c // S5_GROUP)
    lag_blk = [(jnp.dot(t_ref[d], rep_c, preferred_element_type=F32) * same_g).astype(BF16)
               for d in range(S5_L)]
    zero_blk = jnp.zeros((S5_CB, S5_CB), BF16)
    for s in range(S5_L):
        rows = slice(s * S5_CB, (s + 1) * S5_CB)
        p_full = jnp.dot(p_ref[rows, :], rep, preferred_element_type=F32) * same
        pt_ref[rows, :S5_BSTATE] = p_full.astype(BF16)
        for t in range(S5_L):
            cols = slice(S5_BSTATE + t * S5_CB, S5_BSTATE + (t + 1) * S5_CB)
            pt_ref[rows, cols] = lag_blk[t - s] if t >= s else zero_blk
    for t in range(S5_L):
        q_full = lax.dot_general(q_ref[t], rep_c, (((0,), (0,)), ((), ())),
                                 preferred_element_type=F32) * same_q
        qm_ref[:, t * S5_CB:(t + 1) * S5_CB] = q_full.astype(BF16)


def _s5_kernel(ua_ref, ub_ref, pb_ref, tb_ref, qb_ref, ar_ref, ai_ref, d_ref, wg_ref, o_ref,
               pt_ref, q_ref, sr_ref, si_ref, u8_ref, es_ref, ot_ref):
    nb = ua_ref.shape[0]
    ncl = S5_TOK // S5_L
    nlb = S5_BSTATE // LANES
    hlb = nlb // 2
    u_refs = (ua_ref, ub_ref)
    low = lax.broadcasted_iota(jnp.int32, (1, LANES), 1) < S5_CB

    def interleave(a, b):
        return (jnp.where(low, a, pltpu.roll(b, S5_CB, axis=1)),
                jnp.where(low, pltpu.roll(a, S5_CB, axis=1), b))

    @pl.when(pl.program_id(0) == 0)
    def _():
        sr_ref[...] = jnp.zeros_like(sr_ref)
        si_ref[...] = jnp.zeros_like(si_ref)
        for cb in range(S5_NCB):
            _s5_expand_block(pb_ref.at[cb], tb_ref.at[cb], qb_ref.at[cb], pt_ref.at[cb], q_ref.at[cb])

    sub = 8
    ncg = ncl // sub
    grp = nb * sub
    for hh, u_ref in enumerate(u_refs):
        for t2 in range(S5_L // 2):
            for cg in range(ncg):
                tok = [u_ref[:, pl.ds(cg * sub * S5_L + 2 * t2 + k, sub, stride=S5_L), :]
                       .reshape(grp, LANES) for k in range(2)]
                for k, blk in enumerate(interleave(*tok)):
                    u8_ref[2 * hh + k, cg * grp:(cg + 1) * grp, t2 * LANES:(t2 + 1) * LANES] = blk

    for cb in range(S5_NCB):
        e = jnp.dot(u8_ref[cb].astype(BF16), pt_ref[cb, :, :S5_BSTATE],
                    preferred_element_type=F32)
        for j in range(nlb):
            es_ref[cb, j] = e[:, j * LANES:(j + 1) * LANES]

    a_r = ar_ref[...]
    a_i = ai_ref[...]
    sr = sr_ref[...]
    si = si_ref[...]
    for cl in range(ncl):
        step = pl.ds((cl // sub) * grp + cl % sub, nb, stride=sub)
        e_r = es_ref[:, :hlb, step, :]
        e_i = es_ref[:, hlb:, step, :]
        es_ref[:, :hlb, step, :] = sr
        es_ref[:, hlb:, step, :] = si
        sr, si = a_r * sr - a_i * si + e_r, a_r * si + a_i * sr + e_i
    sr_ref[...] = sr
    si_ref[...] = si

    ys = []
    for cb in range(S5_NCB):
        u8 = u8_ref[cb]
        s_all = jnp.concatenate([es_ref[cb, j] for j in range(nlb)], axis=1).astype(BF16)
        y = (jnp.dot(u8.astype(BF16), pt_ref[cb, :, S5_BSTATE:], preferred_element_type=F32)
             + jnp.dot(s_all, q_ref[cb], preferred_element_type=F32) + d_ref[cb] * u8)
        ys.append(jax.nn.gelu(y))
    for t2 in range(S5_L // 2):
        lanes = slice(t2 * LANES, (t2 + 1) * LANES)
        halves = [interleave(ys[2 * hh][:, lanes], ys[2 * hh + 1][:, lanes]) for hh in range(2)]
        for k in range(2):
            t = 2 * t2 + k
            yt = jnp.concatenate([halves[0][k], halves[1][k]], axis=1).astype(BF16)
            ag = jnp.dot(yt, wg_ref[...], preferred_element_type=F32)
            out = ag[:, :D_S5] * _sigmoid(ag[:, D_S5:])
            for hh in range(D_S5 // LANES):
                for cg in range(ncg):
                    ot_ref[hh, :, pl.ds(cg * sub * S5_L + t, sub, stride=S5_L), :] = (
                        out[cg * grp:(cg + 1) * grp, hh * LANES:(hh + 1) * LANES].reshape(nb, sub, LANES))
    o_ref[...] = jnp.concatenate([ot_ref[0], ot_ref[1]], axis=-1).astype(BF16)


def _s5(u, blocks, a_r, a_i, d_row, w_glu):
    nb, seq, _ = u.shape
    rows = nb * (S5_TOK // S5_L)
    nlb = S5_BSTATE // LANES
    assert 2 * S5_CB == LANES and D_S5 == 2 * LANES
    return pl.pallas_call(
        _s5_kernel,
        out_shape=jax.ShapeDtypeStruct((nb, seq, D_S5), BF16),
        grid=(seq // S5_TOK,),
        in_specs=[pl.BlockSpec((nb, S5_TOK, LANES), lambda i: (0, i, 0)),
                  pl.BlockSpec((nb, S5_TOK, LANES), lambda i: (0, i, 1)),
                  *[_resident(b.shape) for b in blocks], _resident(a_r.shape),
                  _resident(a_i.shape), _resident(d_row.shape), _resident(w_glu.shape)],
        out_specs=pl.BlockSpec((nb, S5_TOK, D_S5), lambda i: (0, i, 0)),
        scratch_shapes=[pltpu.VMEM((S5_NCB, S5_BROW, S5_BSTATE + S5_BROW), BF16),
                        pltpu.VMEM((S5_NCB, S5_BSTATE, S5_BROW), BF16),
                        pltpu.VMEM((S5_NCB, nlb // 2, nb, LANES), F32),
                        pltpu.VMEM((S5_NCB, nlb // 2, nb, LANES), F32),
                        pltpu.VMEM((S5_NCB, rows, S5_BROW), F32),
                        pltpu.VMEM((S5_NCB, nlb, rows, LANES), F32),
                        pltpu.VMEM((D_S5 // LANES, nb, S5_TOK, LANES), F32)],
        compiler_params=pltpu.CompilerParams(
            dimension_semantics=("arbitrary",), vmem_limit_bytes=VMEM_LIMIT),
        name="s5",
    )(u, u, *blocks, a_r, a_i, d_row, w_glu)


def _attn_bias_vec(rel_bias):
    rb = rel_bias.astype(F32) * LOG2E
    far_past, far_future = rb[:, 2 * MAX_REL:], rb[:, :1]
    n_const = N_LEFT * CHUNK - MAX_REL
    return jnp.concatenate([
        jnp.broadcast_to(far_past, (HEADS, n_const)),
        rb[:, ::-1],
        jnp.broadcast_to(far_future, (HEADS, ATT_W - n_const - 2 * MAX_REL - 1)),
        jnp.broadcast_to(far_past, (HEADS, ATT_PERIOD - ATT_W)),
    ], axis=1)


def _attn_build_bias(vec_ref, bm_ref):
    r = lax.broadcasted_iota(jnp.int32, (ATT_QB, ATT_W), 0)
    c = lax.broadcasted_iota(jnp.int32, (ATT_QB, ATT_W), 1)
    dchunk = r // CHUNK + N_LEFT - c // CHUNK
    ok = (dchunk >= 0) & (dchunk <= N_LEFT)
    for h in range(HEADS):
        base = jnp.broadcast_to(vec_ref[h:h + 1, :], (ATT_QB, ATT_PERIOD))
        toep = pltpu.roll(base, 0, 1, stride=1, stride_axis=0)[:, :ATT_W]
        bm_ref[h // 2, h % 2] = jnp.where(ok, toep, -1e30)


def _attn_kernel(q_ref, kt_ref, v_ref, vec_ref, o_ref, bm_ref):
    seq = q_ref.shape[1]
    lane = lax.broadcasted_iota(jnp.int32, (1, LANES), 1)
    first = lane < HEAD_DIM

    @pl.when(pl.program_id(0) == 0)
    def _():
        _attn_build_bias(vec_ref, bm_ref)

    def block(q0, k0, width):
        npair = D_ATT // LANES
        outs = [[None, None] for _ in range(npair)]
        for hh in range(2):
            sel = first if hh == 0 else jnp.logical_not(first)
            for pr in range(npair):
                cols = slice(pr * LANES, (pr + 1) * LANES)
                q2 = q_ref[0, pl.ds(q0, ATT_QB), cols]
                qh = jnp.where(sel, q2, jnp.zeros_like(q2))
                for c0 in range(0, width, ATT_KT):
                    kt = kt_ref[0, cols, pl.ds(k0 + c0, ATT_KT)]
                    vw = v_ref[0, pl.ds(k0 + c0, ATT_KT), cols]
                    s = jnp.dot(qh, kt, preferred_element_type=F32)
                    b0 = ATT_W - width + c0
                    s = s + bm_ref[pr, hh, :, b0:b0 + ATT_KT]
                    vh = jnp.where(sel, vw, jnp.ones_like(vw))
                    if c0 == 0:
                        mx = jnp.max(s, axis=-1, keepdims=True)
                        acc = jnp.dot(jnp.exp2(s - mx).astype(BF16), vh, preferred_element_type=F32)
                    else:
                        mx_new = jnp.maximum(mx, jnp.max(s, axis=-1, keepdims=True))
                        p = jnp.exp2(s - mx_new).astype(BF16)
                        acc = acc * jnp.exp2(mx - mx_new) + jnp.dot(p, vh, preferred_element_type=F32)
                        mx = mx_new
                outs[pr][hh] = acc
        for pr, (acc_a, acc_b) in enumerate(outs):
            num = jnp.where(first, acc_a, acc_b)
            den = pltpu.roll(jnp.where(first, acc_b, acc_a), HEAD_DIM, axis=1)
            o_ref[0, pl.ds(q0, ATT_QB), pr * LANES:(pr + 1) * LANES] = (num / den).astype(BF16)

    n_short = N_LEFT * CHUNK // ATT_QB
    for qb in range(n_short):
        block(qb * ATT_QB, 0, (qb + 1) * ATT_QB)

    def body(qb, carry):
        q0 = pl.multiple_of(qb * ATT_QB, ATT_QB)
        k0 = pl.multiple_of(q0 - N_LEFT * CHUNK, ATT_QB)
        block(q0, k0, ATT_W)
        return carry
    lax.fori_loop(n_short, seq // ATT_QB, body, 0)


def _attn(q, kt, v, bias_vec):
    nb, seq, _ = q.shape
    seqblk = pl.BlockSpec((1, seq, D_ATT), lambda b: (b, 0, 0))
    return pl.pallas_call(
        _attn_kernel,
        out_shape=jax.ShapeDtypeStruct((nb, seq, D_ATT), BF16),
        grid=(nb,),
        in_specs=[seqblk, pl.BlockSpec((1, D_ATT, seq), lambda b: (b, 0, 0)), seqblk,
                  _resident(bias_vec.shape)],
        out_specs=seqblk,
        scratch_shapes=[pltpu.VMEM((HEADS // 2, 2, ATT_QB, ATT_W), F32)],
        compiler_params=pltpu.CompilerParams(
            dimension_semantics=("arbitrary",), vmem_limit_bytes=VMEM_LIMIT),
        name="attn",
    )(q, kt, v, bias_vec)


def _conv_taps(hc_ref, cw_ref, cb_ref, cbuf_ref, first_tile):
    nh = D_CONV // LANES
    tm = hc_ref.shape[0]

    @pl.when(first_tile)
    def _():
        cbuf_ref[:, :CONV_PAD, :] = jnp.zeros((nh, CONV_PAD, LANES), F32)
    for hh in range(nh):
        cbuf_ref[hh, CONV_PAD:, :] = hc_ref[:, hh * LANES:(hh + 1) * LANES]
    shift = CONV_PAD - (CONV_WIDTH - 1)
    halves = []
    for hh in range(nh):
        tiles = []
        for r0 in range(0, tm, CONV_TR):
            acc = jnp.zeros((CONV_TR, LANES), F32) + cb_ref[hh]
            for j in range(CONV_WIDTH):
                lo = r0 + shift + j
                acc = acc + cw_ref[hh, j:j + 1, :] * cbuf_ref[hh, lo:lo + CONV_TR, :]
            tiles.append(acc)
        halves.append(jnp.concatenate(tiles, axis=0))
    for hh in range(nh):
        cbuf_ref[hh, :CONV_PAD, :] = cbuf_ref[hh, tm:tm + CONV_PAD, :]
    return halves


def _conv_norm(halves, lg_ref, lb_ref):
    acc = jnp.concatenate(halves, axis=-1)
    mu = jnp.mean(acc, axis=-1, keepdims=True)
    cen = acc - mu
    var = jnp.mean(cen * cen, axis=-1, keepdims=True)
    y = (cen * lax.rsqrt(var + EPS)) * lg_ref[...] + lb_ref[...]
    return (y * _sigmoid(y)).astype(BF16)


def _after(v, zero_ref, width):
    bits = pltpu.bitcast(v, jnp.uint32)
    acc = bits[0:8, :]
    for r0 in range(8, v.shape[0], 8):
        acc = acc | bits[r0:r0 + 8, :]
    zero = pltpu.bitcast(acc[0:1, :] & zero_ref[...], F32)
    return jnp.tile(zero, (1, width // LANES)).astype(BF16)


def _merge_kernel(x_ref, s5_ref, at_ref, hc_ref, g_ref, wg_ref, bg_ref, ws_ref, wa_ref, wc_ref, wo_ref,
                  cw_ref, cb_ref, lg_ref, lb_ref, zero_ref, *refs, tiles_per_seq, cast_nblks):
    n_cast = len(cast_nblks)
    o_ref, cbuf_ref = refs[n_cast], refs[-1]
    _Casts.run_predicated(pl.program_id(0), cast_nblks, refs[:n_cast], refs[n_cast + 1:-1])
    halves = _conv_taps(hc_ref, cw_ref, cb_ref, cbuf_ref, pl.program_id(0) % tiles_per_seq == 0)
    _Casts.run_every_step(cast_nblks, refs[:n_cast], refs[n_cast + 1:-1])
    x = x_ref[...]
    h = _rmsnorm_bf16(x, g_ref[...])
    lhs = (h, h + _after(halves[0], zero_ref, D_MODEL), h + _after(halves[1], zero_ref, D_MODEL))
    branches = (lambda: s5_ref[...], lambda: at_ref[...], lambda: _conv_norm(halves, lg_ref, lb_ref))
    merged = jnp.zeros_like(x)
    for i, (branch, w_ref) in enumerate(zip(branches, (ws_ref, wa_ref, wc_ref))):
        cols = slice(i * D_MODEL, (i + 1) * D_MODEL)
        gcols = slice(N_SMALL + i * D_MODEL, N_SMALL + (i + 1) * D_MODEL)
        logits = jnp.dot(lhs[i], wg_ref[:, gcols], preferred_element_type=F32) + bg_ref[:, cols]
        y = jnp.dot(branch(), w_ref[...], preferred_element_type=F32)
        merged = merged + _sigmoid(logits) * y
    o_ref[...] = x + jnp.dot(merged.astype(BF16), wo_ref[...], preferred_element_type=F32)


def _merge(x, s5o, ato, hc, g, w_in, b_gate, w_s5, w_at, w_cv, w_out, w_dw, b_dw, ln_g, ln_b, seq,
           cast_items):
    n = x.shape[0]
    nh = D_CONV // LANES
    zero = jnp.zeros((1, LANES), jnp.uint32)
    assert seq % TM == 0
    steps = n // TM
    casts = _Casts(cast_items, steps)
    def tok(width):
        return pl.BlockSpec((TM, width), lambda i: (i, 0))
    return pl.pallas_call(
        functools.partial(_merge_kernel, tiles_per_seq=seq // TM, cast_nblks=tuple(casts.nblks)),
        out_shape=(jax.ShapeDtypeStruct((n, D_MODEL), F32), *casts.out_shape),
        grid=(steps,),
        in_specs=[tok(D_MODEL), tok(D_S5), tok(D_ATT), tok(D_CONV),
                  _resident((1, D_MODEL)), _resident(w_in.shape),
                  _resident((1, 3 * D_MODEL)), _resident((D_S5, D_MODEL)),
                  _resident((D_ATT, D_MODEL)), _resident((D_CONV, D_MODEL)),
                  _resident((D_MODEL, D_MODEL)), _resident((nh, CONV_WIDTH, LANES)),
                  _resident((nh, 1, LANES)), _resident((1, D_CONV)), _resident((1, D_CONV)),
                  _resident((1, LANES)), *casts.in_specs],
        out_specs=(tok(D_MODEL), *casts.out_specs),
        scratch_shapes=[pltpu.VMEM((nh, CONV_PAD + TM, LANES), F32)],
        compiler_params=pltpu.CompilerParams(
            dimension_semantics=("arbitrary",), vmem_limit_bytes=VMEM_LIMIT),
        name="merge",
    )(x, s5o, ato, hc, g, w_in, b_gate, w_s5, w_at, w_cv, w_out, w_dw, b_dw, ln_g, ln_b, zero,
      *casts.args)


def kernel(x, ffn1_norm, ffn1_w_up, ffn1_w_down, mix_norm, w_in, b_gate, s5_lambda_re, s5_lambda_im, s5_log_dt, s5_b_re, s5_b_im, s5_c_re, s5_c_im, s5_d, s5_w_glu, w_br_s5, attn_q_gain, attn_k_gain, attn_rel_bias, w_br_attn, conv_w_dw, conv_b_dw, conv_ln_g, conv_ln_b, w_br_conv, w_out, ffn2_norm, ffn2_w_up, ffn2_w_down):
    nb, seq, d = x.shape
    n = nb * seq
    depth = ffn1_norm.shape[0]
    xt = x.reshape(n, d)
    row = lambda v: v.reshape(1, -1).astype(F32)
    f1_up, f1_down, w_in_l = (ffn1_w_up[0].astype(BF16), ffn1_w_down[0].astype(BF16),
                              w_in[0].astype(BF16))
    for l in range(depth):
        xt = _ffn(xt, row(ffn1_norm[l]), f1_up, f1_down)

        qg2 = jnp.tile(row(attn_q_gain[l]), (1, LANES // HEAD_DIM))
        kg2 = jnp.tile(row(attn_k_gain[l]), (1, LANES // HEAD_DIM))
        own = (ffn2_w_up, ffn2_w_down, w_br_s5, w_br_attn, w_br_conv, w_out, s5_w_glu)
        u, q, kt, v, hc, f2_up, f2_down, w_s5, w_at, w_cv, w_o, w_glu = _inproj(
            xt, row(mix_norm[l]), w_in_l, qg2, kg2, seq, [(w, l) for w in own])

        blocks, a_r, a_i, d_row = _s5_tables(
            s5_lambda_re[l], s5_lambda_im[l], s5_log_dt[l], s5_b_re[l], s5_b_im[l],
            s5_c_re[l], s5_c_im[l], s5_d[l])
        s5o = _s5(u.reshape(nb, seq, D_S5), blocks, a_r, a_i, d_row, w_glu)

        ato = _attn(q.reshape(nb, seq, D_ATT), kt, v.reshape(nb, seq, D_ATT),
                    _attn_bias_vec(attn_rel_bias[l]))

        nh = D_CONV // LANES
        w_dw = conv_w_dw[l].astype(F32).reshape(CONV_WIDTH, nh, LANES).transpose(1, 0, 2)
        nxt = (ffn1_w_up, ffn1_w_down, w_in) if l + 1 < depth else ()
        xt, *nxt_bf16 = _merge(
            xt, s5o.reshape(n, D_S5), ato.reshape(n, D_ATT), hc, row(mix_norm[l]), w_in_l,
            row(b_gate[l]), w_s5, w_at, w_cv, w_o, w_dw,
            conv_b_dw[l].astype(F32).reshape(nh, 1, LANES), row(conv_ln_g[l]), row(conv_ln_b[l]),
            seq, [(w, l + 1) for w in nxt])
        if nxt_bf16:
            f1_up, f1_down, w_in_l = nxt_bf16

        xt = _ffn(xt, row(ffn2_norm[l]), f2_up, f2_down)
    return xt.reshape(nb, seq, d)
```

```python
import functools
import math

import jax
import jax.numpy as jnp
from jax import lax
from jax.experimental import pallas as pl
from jax.experimental.pallas import tpu as pltpu

F32 = jnp.float32
BF16 = jnp.bfloat16

D_MODEL = 1024
D_S5 = 256
S5_GROUP = 16
S5_GROUPS = 16
S5_STATE = 64
D_ATT = 512
HEAD_DIM = 64
HEADS = 8
CHUNK = 64
N_LEFT = 8
MAX_REL = 128
D_CONV = 256
CONV_WIDTH = 31
D_FF = 2816
EPS = 1e-6
LOG2E = math.log2(math.e)
N_SMALL = D_S5 + 3 * D_ATT + 2 * D_CONV

LANES = 128
MXU_DIM = 256
VMEM_LIMIT = 56 * 1024 * 1024

TM = 1024
TM_WIDE = 1024
CAST_MIN_ROWS = 16
CAST_ROWS = 64
S5_L = 8
S5_CB = LANES // 2
S5_GB = S5_CB // S5_GROUP
S5_NCB = D_S5 // S5_CB
S5_BROW = S5_L * S5_CB
S5_BSTATE = 2 * S5_GB * S5_STATE
S5_TOK = 256
ATT_QB = 4 * CHUNK
ATT_W = ATT_QB + N_LEFT * CHUNK
ATT_KT = MXU_DIM
ATT_PERIOD = ATT_QB + ATT_W
CONV_TR = 64
CONV_PAD = 32


def _resident(shape):
    nd = len(shape)
    return pl.BlockSpec(shape, lambda *_: (0,) * nd, pipeline_mode=pl.Buffered(1))


def _rmsnorm_bf16(x, g):
    ms = jnp.mean(x * x, axis=-1, keepdims=True)
    return ((x * lax.rsqrt(ms + EPS)) * g).astype(BF16)


def _sigmoid(x):
    return jax.nn.sigmoid(x)


class _Casts:
    def __init__(self, items, steps):
        self.in_specs, self.out_specs, self.out_shape, self.args, self.nblks = [], [], [], [], []
        for stacked, layer in items:
            _, rows, cols = stacked.shape
            every_step = rows % CAST_MIN_ROWS == 0 and rows // CAST_MIN_ROWS <= steps
            rb = CAST_MIN_ROWS if every_step else CAST_ROWS
            nblk = rows // rb
            assert nblk * rb == rows and nblk <= steps
            self.nblks.append(None if every_step else nblk)
            self.in_specs.append(pl.BlockSpec(
                (None, rb, cols),
                lambda i, layer=layer, nblk=nblk: (layer, jnp.minimum(i, nblk - 1), 0)))
            self.out_specs.append(pl.BlockSpec(
                (rb, cols), lambda i, nblk=nblk: (jnp.minimum(i, nblk - 1), 0)))
            self.out_shape.append(jax.ShapeDtypeStruct((rows, cols), BF16))
            self.args.append(stacked)

    @staticmethod
    def run_predicated(step, nblks, src_refs, dst_refs):
        for nblk in sorted({n for n in nblks if n is not None}):
            @pl.when(step < nblk)
            def _():
                for n, src, dst in zip(nblks, src_refs, dst_refs):
                    if n == nblk:
                        dst[...] = src[...].astype(BF16)

    @staticmethod
    def run_every_step(nblks, src_refs, dst_refs):
        for n, src, dst in zip(nblks, src_refs, dst_refs):
            if n is None:
                dst[...] = src[...].astype(BF16)


def _ffn_kernel(x_hbm, g_ref, wa_ref, wb_ref, wd_ref, o_hbm, xbuf, obuf, in_sem, out_sem, *,
                bounds, steps):
    def rows(i):
        return pl.ds(pl.multiple_of(i * TM_WIDE, TM_WIDE), TM_WIDE)

    def in_copy(i, slot):
        return pltpu.make_async_copy(x_hbm.at[rows(i)], xbuf.at[slot], in_sem.at[slot])

    def out_copy(i, slot):
        return pltpu.make_async_copy(obuf.at[slot], o_hbm.at[rows(i)], out_sem.at[slot])

    in_copy(0, 0).start()

    def body(i, carry):
        slot = i % 2
        in_copy(i, slot).wait()

        @pl.when(i + 1 < steps)
        def _():
            in_copy(i + 1, 1 - slot).start()

        @pl.when(i >= 2)
        def _():
            out_copy(i - 2, slot).wait()

        x = xbuf[slot]
        h = _rmsnorm_bf16(x, g_ref[...])
        y = jnp.zeros_like(x)
        for f0, f1 in zip(bounds[:-1], bounds[1:]):
            a = jnp.dot(h, wa_ref[:, f0:f1], preferred_element_type=F32)
            b = jnp.dot(h, wb_ref[:, f0:f1], preferred_element_type=F32)
            act = ((a * _sigmoid(a)) * b).astype(BF16)
            y = y + jnp.dot(act, wd_ref[f0:f1, :], preferred_element_type=F32)
        obuf[slot] = x + 0.5 * y
        out_copy(i, slot).start()
        return carry
    lax.fori_loop(0, steps, body, 0)
    out_copy(steps - 2, steps % 2).wait()
    out_copy(steps - 1, (steps - 1) % 2).wait()


def _ffn_bounds():
    tiles = D_FF // MXU_DIM
    assert tiles * MXU_DIM == D_FF
    return (0, (tiles + 1) // 2 * MXU_DIM, D_FF)


def _ffn(x, g, w_up, w_down):
    n = x.shape[0]
    steps = n // TM_WIDE
    assert steps * TM_WIDE == n and steps >= 2
    def up_half(j):
        return pl.BlockSpec((D_MODEL, D_FF), lambda i: (0, j), pipeline_mode=pl.Buffered(1))
    tiles = pltpu.VMEM((2, TM_WIDE, D_MODEL), F32)
    return pl.pallas_call(
        functools.partial(_ffn_kernel, bounds=_ffn_bounds(), steps=steps),
        out_shape=jax.ShapeDtypeStruct((n, D_MODEL), F32),
        grid=(1,),
        in_specs=[pl.BlockSpec(memory_space=pl.ANY), _resident((1, D_MODEL)), up_half(0), up_half(1),
                  _resident((D_FF, D_MODEL))],
        out_specs=pl.BlockSpec(memory_space=pl.ANY),
        scratch_shapes=[tiles, tiles, pltpu.SemaphoreType.DMA((2,)), pltpu.SemaphoreType.DMA((2,))],
        compiler_params=pltpu.CompilerParams(
            dimension_semantics=("arbitrary",), vmem_limit_bytes=VMEM_LIMIT),
        name="ffn",
    )(x, g, w_up, w_up, w_down)


def _head_pair_norm(x2, gain2):
    lane = lax.broadcasted_iota(jnp.int32, (1, LANES), 1)
    first = lane < HEAD_DIM
    sq = x2 * x2
    s_a = jnp.sum(jnp.where(first, sq, 0.0), axis=-1, keepdims=True)
    s_b = jnp.sum(jnp.where(first, 0.0, sq), axis=-1, keepdims=True)
    ms = jnp.where(first, s_a, s_b) * (1.0 / HEAD_DIM)
    return (x2 * lax.rsqrt(ms + EPS)) * gain2


def _inproj_kernel(x_ref, g_ref, w_ref, qg_ref, kg_ref, *refs, cast_nblks):
    n_cast = len(cast_nblks)
    u_ref, q_ref, kt_ref, v_ref, hc_ref = refs[n_cast:n_cast + 5]
    _Casts.run_predicated(pl.program_id(0), cast_nblks, refs[:n_cast], refs[n_cast + 5:])
    _Casts.run_every_step(cast_nblks, refs[:n_cast], refs[n_cast + 5:])
    h = _rmsnorm_bf16(x_ref[...], g_ref[...])
    proj = jnp.dot(h, w_ref[...], preferred_element_type=F32)
    u_ref[...] = proj[:, :D_S5]
    q0, k0, v0, z0 = D_S5, D_S5 + D_ATT, D_S5 + 2 * D_ATT, D_S5 + 3 * D_ATT
    scale = HEAD_DIM ** -0.5 * LOG2E
    for p in range(D_ATT // LANES):
        lo = p * LANES
        qn = _head_pair_norm(proj[:, q0 + lo:q0 + lo + LANES], qg_ref[...])
        kn = _head_pair_norm(proj[:, k0 + lo:k0 + lo + LANES], kg_ref[...])
        q_ref[:, lo:lo + LANES] = (qn * scale).astype(BF16)
        kt_ref[lo:lo + LANES, :] = kn.T.astype(BF16)
    v_ref[...] = proj[:, v0:v0 + D_ATT].astype(BF16)
    a = proj[:, z0:z0 + D_CONV]
    gt = proj[:, z0 + D_CONV:z0 + 2 * D_CONV]
    hc_ref[...] = a * _sigmoid(gt)


def _inproj(x, g, w_in, qg2, kg2, seq, cast_items):
    n = x.shape[0]
    steps = n // TM_WIDE
    tiles_per_seq = seq // TM_WIDE
    assert tiles_per_seq * TM_WIDE == seq
    casts = _Casts(cast_items, steps)
    def tok(width):
        return pl.BlockSpec((TM_WIDE, width), lambda i: (i, 0))
    w_small = pl.BlockSpec((D_MODEL, N_SMALL), lambda i: (0, 0), pipeline_mode=pl.Buffered(1))
    kt_spec = pl.BlockSpec((None, D_ATT, TM_WIDE),
                           lambda i: (i // tiles_per_seq, 0, i % tiles_per_seq))
    return pl.pallas_call(
        functools.partial(_inproj_kernel, cast_nblks=tuple(casts.nblks)),
        out_shape=(jax.ShapeDtypeStruct((n, D_S5), F32),
                   jax.ShapeDtypeStruct((n, D_ATT), BF16),
                   jax.ShapeDtypeStruct((n // seq, D_ATT, seq), BF16),
                   jax.ShapeDtypeStruct((n, D_ATT), BF16),
                   jax.ShapeDtypeStruct((n, D_CONV), F32), *casts.out_shape),
        grid=(steps,),
        in_specs=[tok(D_MODEL), _resident((1, D_MODEL)), w_small,
                  _resident((1, LANES)), _resident((1, LANES)), *casts.in_specs],
        out_specs=(tok(D_S5), tok(D_ATT), kt_spec, tok(D_ATT), tok(D_CONV), *casts.out_specs),
        compiler_params=pltpu.CompilerParams(
            dimension_semantics=("arbitrary",), vmem_limit_bytes=VMEM_LIMIT),
        name="inproj",
    )(x, g, w_in, qg2, kg2, *casts.args)


def _s5_tables(lambda_re, lambda_im, log_dt, b_re, b_im, c_re, c_im, d_skip):
    lr = jnp.minimum(lambda_re.astype(F32), -1e-4)
    li = lambda_im.astype(F32)
    dt = jnp.exp(log_dt.astype(F32))[:, None]
    mag = jnp.exp(lr * dt)
    ar = mag * jnp.cos(li * dt)
    ai = mag * jnp.sin(li * dt)
    den = lr * lr + li * li
    coef_r = ((ar - 1.0) * lr + ai * li) / den
    coef_i = (ai * lr - (ar - 1.0) * li) / den
    br = b_re.astype(F32)
    bi = b_im.astype(F32)
    bbar_r = coef_r[..., None] * br - coef_i[..., None] * bi
    bbar_i = coef_r[..., None] * bi + coef_i[..., None] * br
    cr = c_re.astype(F32)
    ci = c_im.astype(F32)

    pr, pi = [jnp.ones_like(ar)], [jnp.zeros_like(ai)]
    for _ in range(S5_L):
        r, i = pr[-1], pi[-1]
        pr.append(r * ar - i * ai)
        pi.append(r * ai + i * ar)
    pr = jnp.stack(pr)
    pi = jnp.stack(pi)

    ncb, gb = S5_NCB, S5_GB
    pw_r = pr[S5_L - 1::-1]
    pw_i = pi[S5_L - 1::-1]
    e_r = pw_r[..., None] * bbar_r[None] - pw_i[..., None] * bbar_i[None]
    e_i = pw_r[..., None] * bbar_i[None] + pw_i[..., None] * bbar_r[None]
    p_blk = jnp.stack([jnp.transpose(e_r, (0, 1, 3, 2)), jnp.transpose(e_i, (0, 1, 3, 2))], axis=3)
    p_blk = p_blk.reshape(S5_L, ncb, gb * S5_GROUP, 2 * S5_STATE)
    p_blk = jnp.transpose(p_blk, (1, 0, 2, 3)).reshape(ncb, S5_BROW, 2 * S5_STATE)

    m_r = pr[:S5_L, :, None, :] * cr[None] - pi[:S5_L, :, None, :] * ci[None]
    m_i = pr[:S5_L, :, None, :] * ci[None] + pi[:S5_L, :, None, :] * cr[None]
    kern = (jnp.sum(m_r[:, :, :, :, None] * bbar_r[None, :, None, :, :], axis=3)
            - jnp.sum(m_i[:, :, :, :, None] * bbar_i[None, :, None, :, :], axis=3))
    t_blk = jnp.transpose(kern, (0, 1, 3, 2)).reshape(S5_L, ncb, S5_CB, S5_GROUP)
    t_blk = jnp.transpose(t_blk, (1, 0, 2, 3))

    q_r = pr[1:, :, None, :] * cr[None] - pi[1:, :, None, :] * ci[None]
    q_i = -(pr[1:, :, None, :] * ci[None] + pi[1:, :, None, :] * cr[None])
    q_blk = jnp.stack([q_r, q_i], axis=0).reshape(2, S5_L, ncb, gb, S5_GROUP, S5_STATE)
    q_blk = jnp.transpose(q_blk, (2, 1, 4, 0, 3, 5))
    q_blk = q_blk.reshape(ncb, S5_L, S5_GROUP, S5_BSTATE)

    half_blocks = S5_BSTATE // 2 // LANES
    a_r = pr[S5_L].reshape(ncb, half_blocks, 1, LANES)
    a_i = pi[S5_L].reshape(ncb, half_blocks, 1, LANES)
    d_row = jnp.tile(d_skip.astype(F32).reshape(ncb, 1, S5_CB), (1, 1, S5_L))
    pt, q_mat = _s5_expand(p_blk.astype(BF16), t_blk.astype(BF16), q_blk.astype(BF16))
    return pt, q_mat, a_r, a_i, d_row


def _s5_expand_kernel(p_ref, t_ref, q_ref, pt_ref, qm_ref):
    def iota(shape, dim):
        return lax.broadcasted_iota(jnp.int32, shape, dim)

    def onehot(rows, cols, row_key, col_key):
        r, c = iota((rows, cols), 0), iota((rows, cols), 1)
        return jnp.where(row_key(r) == col_key(c), 1.0, 0.0)

    half = S5_BSTATE // 2
    rep = onehot(2 * S5_STATE, S5_BSTATE, lambda r: r,
                 lambda c: (c // half) * S5_STATE + c % S5_STATE).astype(BF16)
    same = onehot(S5_CB, S5_BSTATE, lambda r: r // S5_GROUP, lambda c: (c % half) // S5_STATE)
    rep_c = onehot(S5_GROUP, S5_CB, lambda r: r, lambda c: c % S5_GROUP).astype(BF16)
    same_g = onehot(S5_CB, S5_CB, lambda r: r // S5_GROUP, lambda c: c // S5_GROUP)
    same_q = onehot(S5_BSTATE, S5_CB, lambda r: (r % half) // S5_STATE, lambda c: c // S5_GROUP)
    lag_blk = [(jnp.dot(t_ref[d], rep_c, preferred_element_type=F32) * same_g).astype(BF16)
               for d in range(S5_L)]
    zero_blk = jnp.zeros((S5_CB, S5_CB), BF16)
    for s in range(S5_L):
        rows = slice(s * S5_CB, (s + 1) * S5_CB)
        p_full = jnp.dot(p_ref[rows, :], rep, preferred_element_type=F32) * same
        pt_ref[rows, :S5_BSTATE] = p_full.astype(BF16)
        for t in range(S5_L):
            cols = slice(S5_BSTATE + t * S5_CB, S5_BSTATE + (t + 1) * S5_CB)
            pt_ref[rows, cols] = lag_blk[t - s] if t >= s else zero_blk
    for t in range(S5_L):
        q_full = lax.dot_general(q_ref[t], rep_c, (((0,), (0,)), ((), ())),
                                 preferred_element_type=F32) * same_q
        qm_ref[:, t * S5_CB:(t + 1) * S5_CB] = q_full.astype(BF16)


def _s5_expand(p_blk, t_blk, q_blk):
    def blk(shape):
        return pl.BlockSpec((None,) + shape, lambda cb: (cb,) + (0,) * len(shape))
    return pl.pallas_call(
        _s5_expand_kernel,
        out_shape=(jax.ShapeDtypeStruct((S5_NCB, S5_BROW, S5_BSTATE + S5_BROW), BF16),
                   jax.ShapeDtypeStruct((S5_NCB, S5_BSTATE, S5_BROW), BF16)),
        grid=(S5_NCB,),
        in_specs=[blk((S5_BROW, 2 * S5_STATE)), blk((S5_L, S5_CB, S5_GROUP)),
                  blk((S5_L, S5_GROUP, S5_BSTATE))],
        out_specs=(blk((S5_BROW, S5_BSTATE + S5_BROW)), blk((S5_BSTATE, S5_BROW))),
        compiler_params=pltpu.CompilerParams(
            dimension_semantics=("arbitrary",), vmem_limit_bytes=VMEM_LIMIT),
        name="s5_tables",
    )(p_blk, t_blk, q_blk)


def _s5_kernel(ua_ref, ub_ref, pt_ref, q_ref, ar_ref, ai_ref, d_ref, wg_ref, o_ref,
               sr_ref, si_ref, u8_ref, es_ref, ot_ref):
    nb = ua_ref.shape[0]
    ncl = S5_TOK // S5_L
    nlb = S5_BSTATE // LANES
    hlb = nlb // 2
    u_refs = (ua_ref, ub_ref)
    low = lax.broadcasted_iota(jnp.int32, (1, LANES), 1) < S5_CB

    def interleave(a, b):
        return (jnp.where(low, a, pltpu.roll(b, S5_CB, axis=1)),
                jnp.where(low, pltpu.roll(a, S5_CB, axis=1), b))

    @pl.when(pl.program_id(0) == 0)
    def _():
        sr_ref[...] = jnp.zeros_like(sr_ref)
        si_ref[...] = jnp.zeros_like(si_ref)

    sub = 8
    ncg = ncl // sub
    grp = nb * sub
    for hh, u_ref in enumerate(u_refs):
        for t2 in range(S5_L // 2):
            for cg in range(ncg):
                tok = [u_ref[:, pl.ds(cg * sub * S5_L + 2 * t2 + k, sub, stride=S5_L), :]
                       .reshape(grp, LANES) for k in range(2)]
                for k, blk in enumerate(interleave(*tok)):
                    u8_ref[2 * hh + k, cg * grp:(cg + 1) * grp, t2 * LANES:(t2 + 1) * LANES] = blk

    for cb in range(S5_NCB):
        e = jnp.dot(u8_ref[cb].astype(BF16), pt_ref[cb, :, :S5_BSTATE],
                    preferred_element_type=F32)
        for j in range(nlb):
            es_ref[cb, j] = e[:, j * LANES:(j + 1) * LANES]

    a_r = ar_ref[...]
    a_i = ai_ref[...]
    sr = sr_ref[...]
    si = si_ref[...]
    for cl in range(ncl):
        step = pl.ds((cl // sub) * grp + cl % sub, nb, stride=sub)
        e_r = es_ref[:, :hlb, step, :]
        e_i = es_ref[:, hlb:, step, :]
        es_ref[:, :hlb, step, :] = sr
        es_ref[:, hlb:, step, :] = si
        sr, si = a_r * sr - a_i * si + e_r, a_r * si + a_i * sr + e_i
    sr_ref[...] = sr
    si_ref[...] = si

    ys = []
    for cb in range(S5_NCB):
        u8 = u8_ref[cb]
        s_all = jnp.concatenate([es_ref[cb, j] for j in range(nlb)], axis=1).astype(BF16)
        y = (jnp.dot(u8.astype(BF16), pt_ref[cb, :, S5_BSTATE:], preferred_element_type=F32)
             + jnp.dot(s_all, q_ref[cb], preferred_element_type=F32) + d_ref[cb] * u8)
        ys.append(jax.nn.gelu(y))
    for t2 in range(S5_L // 2):
        lanes = slice(t2 * LANES, (t2 + 1) * LANES)
        halves = [interleave(ys[2 * hh][:, lanes], ys[2 * hh + 1][:, lanes]) for hh in range(2)]
        for k in range(2):
            t = 2 * t2 + k
            yt = jnp.concatenate([halves[0][k], halves[1][k]], axis=1).astype(BF16)
            ag = jnp.dot(yt, wg_ref[...], preferred_element_type=F32)
            out = ag[:, :D_S5] * _sigmoid(ag[:, D_S5:])
            for hh in range(D_S5 // LANES):
                for cg in range(ncg):
                    ot_ref[hh, :, pl.ds(cg * sub * S5_L + t, sub, stride=S5_L), :] = (
                        out[cg * grp:(cg + 1) * grp, hh * LANES:(hh + 1) * LANES].reshape(nb, sub, LANES))
    o_ref[...] = jnp.concatenate([ot_ref[0], ot_ref[1]], axis=-1).astype(BF16)


def _s5(u, pt, q_mat, a_r, a_i, d_row, w_glu):
    nb, seq, _ = u.shape
    rows = nb * (S5_TOK // S5_L)
    nlb = S5_BSTATE // LANES
    assert 2 * S5_CB == LANES and D_S5 == 2 * LANES
    return pl.pallas_call(
        _s5_kernel,
        out_shape=jax.ShapeDtypeStruct((nb, seq, D_S5), BF16),
        grid=(seq // S5_TOK,),
        in_specs=[pl.BlockSpec((nb, S5_TOK, LANES), lambda i: (0, i, 0)),
                  pl.BlockSpec((nb, S5_TOK, LANES), lambda i: (0, i, 1)),
                  _resident(pt.shape), _resident(q_mat.shape), _resident(a_r.shape),
                  _resident(a_i.shape), _resident(d_row.shape), _resident(w_glu.shape)],
        out_specs=pl.BlockSpec((nb, S5_TOK, D_S5), lambda i: (0, i, 0)),
        scratch_shapes=[pltpu.VMEM((S5_NCB, nlb // 2, nb, LANES), F32),
                        pltpu.VMEM((S5_NCB, nlb // 2, nb, LANES), F32),
                        pltpu.VMEM((S5_NCB, rows, S5_BROW), F32),
                        pltpu.VMEM((S5_NCB, nlb, rows, LANES), F32),
                        pltpu.VMEM((D_S5 // LANES, nb, S5_TOK, LANES), F32)],
        compiler_params=pltpu.CompilerParams(
            dimension_semantics=("arbitrary",), vmem_limit_bytes=VMEM_LIMIT),
        name="s5",
    )(u, u, pt, q_mat, a_r, a_i, d_row, w_glu)


def _attn_bias_vec(rel_bias):
    rb = rel_bias.astype(F32) * LOG2E
    far_past, far_future = rb[:, 2 * MAX_REL:], rb[:, :1]
    n_const = N_LEFT * CHUNK - MAX_REL
    return jnp.concatenate([
        jnp.broadcast_to(far_past, (HEADS, n_const)),
        rb[:, ::-1],
        jnp.broadcast_to(far_future, (HEADS, ATT_W - n_const - 2 * MAX_REL - 1)),
        jnp.broadcast_to(far_past, (HEADS, ATT_PERIOD - ATT_W)),
    ], axis=1)


def _attn_build_bias(vec_ref, bm_ref):
    r = lax.broadcasted_iota(jnp.int32, (ATT_QB, ATT_W), 0)
    c = lax.broadcasted_iota(jnp.int32, (ATT_QB, ATT_W), 1)
    dchunk = r // CHUNK + N_LEFT - c // CHUNK
    ok = (dchunk >= 0) & (dchunk <= N_LEFT)
    for h in range(HEADS):
        base = jnp.broadcast_to(vec_ref[h:h + 1, :], (ATT_QB, ATT_PERIOD))
        toep = pltpu.roll(base, 0, 1, stride=1, stride_axis=0)[:, :ATT_W]
        bm_ref[h // 2, h % 2] = jnp.where(ok, toep, -1e30)


def _attn_kernel(q_ref, kt_ref, v_ref, vec_ref, o_ref, bm_ref):
    seq = q_ref.shape[1]
    lane = lax.broadcasted_iota(jnp.int32, (1, LANES), 1)
    first = lane < HEAD_DIM

    @pl.when(pl.program_id(0) == 0)
    def _():
        _attn_build_bias(vec_ref, bm_ref)

    def block(q0, k0, width):
        npair = D_ATT // LANES
        outs = [[None, None] for _ in range(npair)]
        for hh in range(2):
            sel = first if hh == 0 else jnp.logical_not(first)
            for pr in range(npair):
                cols = slice(pr * LANES, (pr + 1) * LANES)
                q2 = q_ref[0, pl.ds(q0, ATT_QB), cols]
                qh = jnp.where(sel, q2, jnp.zeros_like(q2))
                for c0 in range(0, width, ATT_KT):
                    kt = kt_ref[0, cols, pl.ds(k0 + c0, ATT_KT)]
                    vw = v_ref[0, pl.ds(k0 + c0, ATT_KT), cols]
                    s = jnp.dot(qh, kt, preferred_element_type=F32)
                    b0 = ATT_W - width + c0
                    s = s + bm_ref[pr, hh, :, b0:b0 + ATT_KT]
                    vh = jnp.where(sel, vw, jnp.ones_like(vw))
                    if c0 == 0:
                        mx = jnp.max(s, axis=-1, keepdims=True)
                        acc = jnp.dot(jnp.exp2(s - mx).astype(BF16), vh, preferred_element_type=F32)
                    else:
                        mx_new = jnp.maximum(mx, jnp.max(s, axis=-1, keepdims=True))
                        p = jnp.exp2(s - mx_new).astype(BF16)
                        acc = acc * jnp.exp2(mx - mx_new) + jnp.dot(p, vh, preferred_element_type=F32)
                        mx = mx_new
                outs[pr][hh] = acc
        for pr, (acc_a, acc_b) in enumerate(outs):
            num = jnp.where(first, acc_a, acc_b)
            den = pltpu.roll(jnp.where(first, acc_b, acc_a), HEAD_DIM, axis=1)
            o_ref[0, pl.ds(q0, ATT_QB), pr * LANES:(pr + 1) * LANES] = (num / den).astype(BF16)

    n_short = N_LEFT * CHUNK // ATT_QB
    for qb in range(n_short):
        block(qb * ATT_QB, 0, (qb + 1) * ATT_QB)

    def body(qb, carry):
        q0 = pl.multiple_of(qb * ATT_QB, ATT_QB)
        k0 = pl.multiple_of(q0 - N_LEFT * CHUNK, ATT_QB)
        block(q0, k0, ATT_W)
        return carry
    lax.fori_loop(n_short, seq // ATT_QB, body, 0)


def _attn(q, kt, v, bias_vec):
    nb, seq, _ = q.shape
    seqblk = pl.BlockSpec((1, seq, D_ATT), lambda b: (b, 0, 0))
    return pl.pallas_call(
        _attn_kernel,
        out_shape=jax.ShapeDtypeStruct((nb, seq, D_ATT), BF16),
        grid=(nb,),
        in_specs=[seqblk, pl.BlockSpec((1, D_ATT, seq), lambda b: (b, 0, 0)), seqblk,
                  _resident(bias_vec.shape)],
        out_specs=seqblk,
        scratch_shapes=[pltpu.VMEM((HEADS // 2, 2, ATT_QB, ATT_W), F32)],
        compiler_params=pltpu.CompilerParams(
            dimension_semantics=("arbitrary",), vmem_limit_bytes=VMEM_LIMIT),
        name="attn",
    )(q, kt, v, bias_vec)


def _conv_taps(hc_ref, cw_ref, cb_ref, cbuf_ref, first_tile):
    nh = D_CONV // LANES
    tm = hc_ref.shape[0]

    @pl.when(first_tile)
    def _():
        cbuf_ref[:, :CONV_PAD, :] = jnp.zeros((nh, CONV_PAD, LANES), F32)
    for hh in range(nh):
        cbuf_ref[hh, CONV_PAD:, :] = hc_ref[:, hh * LANES:(hh + 1) * LANES]
    shift = CONV_PAD - (CONV_WIDTH - 1)
    halves = []
    for hh in range(nh):
        tiles = []
        for r0 in range(0, tm, CONV_TR):
            acc = jnp.zeros((CONV_TR, LANES), F32) + cb_ref[hh]
            for j in range(CONV_WIDTH):
                lo = r0 + shift + j
                acc = acc + cw_ref[hh, j:j + 1, :] * cbuf_ref[hh, lo:lo + CONV_TR, :]
            tiles.append(acc)
        halves.append(jnp.concatenate(tiles, axis=0))
    for hh in range(nh):
        cbuf_ref[hh, :CONV_PAD, :] = cbuf_ref[hh, tm:tm + CONV_PAD, :]
    return halves


def _conv_norm(halves, lg_ref, lb_ref):
    acc = jnp.concatenate(halves, axis=-1)
    mu = jnp.mean(acc, axis=-1, keepdims=True)
    cen = acc - mu
    var = jnp.mean(cen * cen, axis=-1, keepdims=True)
    y = (cen * lax.rsqrt(var + EPS)) * lg_ref[...] + lb_ref[...]
    return (y * _sigmoid(y)).astype(BF16)


def _after(v, zero_ref, width):
    bits = pltpu.bitcast(v, jnp.uint32)
    acc = bits[0:8, :]
    for r0 in range(8, v.shape[0], 8):
        acc = acc | bits[r0:r0 + 8, :]
    zero = pltpu.bitcast(acc[0:1, :] & zero_ref[...], F32)
    return jnp.tile(zero, (1, width // LANES)).astype(BF16)


def _merge_kernel(x_ref, s5_ref, at_ref, hc_ref, g_ref, wg_ref, bg_ref, ws_ref, wa_ref, wc_ref, wo_ref,
                  cw_ref, cb_ref, lg_ref, lb_ref, zero_ref, *refs, tiles_per_seq, cast_nblks):
    n_cast = len(cast_nblks)
    o_ref, cbuf_ref = refs[n_cast], refs[-1]
    _Casts.run_predicated(pl.program_id(0), cast_nblks, refs[:n_cast], refs[n_cast + 1:-1])
    halves = _conv_taps(hc_ref, cw_ref, cb_ref, cbuf_ref, pl.program_id(0) % tiles_per_seq == 0)
    _Casts.run_every_step(cast_nblks, refs[:n_cast], refs[n_cast + 1:-1])
    x = x_ref[...]
    h = _rmsnorm_bf16(x, g_ref[...])
    lhs = (h, h + _after(halves[0], zero_ref, D_MODEL), h + _after(halves[1], zero_ref, D_MODEL))
    branches = (lambda: s5_ref[...], lambda: at_ref[...], lambda: _conv_norm(halves, lg_ref, lb_ref))
    merged = jnp.zeros_like(x)
    for i, (branch, w_ref) in enumerate(zip(branches, (ws_ref, wa_ref, wc_ref))):
        cols = slice(i * D_MODEL, (i + 1) * D_MODEL)
        gcols = slice(N_SMALL + i * D_MODEL, N_SMALL + (i + 1) * D_MODEL)
        logits = jnp.dot(lhs[i], wg_ref[:, gcols], preferred_element_type=F32) + bg_ref[:, cols]
        y = jnp.dot(branch(), w_ref[...], preferred_element_type=F32)
        merged = merged + _sigmoid(logits) * y
    o_ref[...] = x + jnp.dot(merged.astype(BF16), wo_ref[...], preferred_element_type=F32)


def _merge(x, s5o, ato, hc, g, w_in, b_gate, w_s5, w_at, w_cv, w_out, w_dw, b_dw, ln_g, ln_b, seq,
           cast_items):
    n = x.shape[0]
    nh = D_CONV // LANES
    zero = jnp.zeros((1, LANES), jnp.uint32)
    assert seq % TM == 0
    steps = n // TM
    casts = _Casts(cast_items, steps)
    def tok(width):
        return pl.BlockSpec((TM, width), lambda i: (i, 0))
    return pl.pallas_call(
        functools.partial(_merge_kernel, tiles_per_seq=seq // TM, cast_nblks=tuple(casts.nblks)),
        out_shape=(jax.ShapeDtypeStruct((n, D_MODEL), F32), *casts.out_shape),
        grid=(steps,),
        in_specs=[tok(D_MODEL), tok(D_S5), tok(D_ATT), tok(D_CONV),
                  _resident((1, D_MODEL)), _resident(w_in.shape),
                  _resident((1, 3 * D_MODEL)), _resident((D_S5, D_MODEL)),
                  _resident((D_ATT, D_MODEL)), _resident((D_CONV, D_MODEL)),
                  _resident((D_MODEL, D_MODEL)), _resident((nh, CONV_WIDTH, LANES)),
                  _resident((nh, 1, LANES)), _resident((1, D_CONV)), _resident((1, D_CONV)),
                  _resident((1, LANES)), *casts.in_specs],
        out_specs=(tok(D_MODEL), *casts.out_specs),
        scratch_shapes=[pltpu.VMEM((nh, CONV_PAD + TM, LANES), F32)],
        compiler_params=pltpu.CompilerParams(
            dimension_semantics=("arbitrary",), vmem_limit_bytes=VMEM_LIMIT),
        name="merge",
    )(x, s5o, ato, hc, g, w_in, b_gate, w_s5, w_at, w_cv, w_out, w_dw, b_dw, ln_g, ln_b, zero,
      *casts.args)


def kernel(x, ffn1_norm, ffn1_w_up, ffn1_w_down, mix_norm, w_in, b_gate, s5_lambda_re, s5_lambda_im, s5_log_dt, s5_b_re, s5_b_im, s5_c_re, s5_c_im, s5_d, s5_w_glu, w_br_s5, attn_q_gain, attn_k_gain, attn_rel_bias, w_br_attn, conv_w_dw, conv_b_dw, conv_ln_g, conv_ln_b, w_br_conv, w_out, ffn2_norm, ffn2_w_up, ffn2_w_down):
    nb, seq, d = x.shape
    n = nb * seq
    depth = ffn1_norm.shape[0]
    xt = x.reshape(n, d)
    row = lambda v: v.reshape(1, -1).astype(F32)
    f1_up, f1_down, w_in_l = (ffn1_w_up[0].astype(BF16), ffn1_w_down[0].astype(BF16),
                              w_in[0].astype(BF16))
    for l in range(depth):
        xt = _ffn(xt, row(ffn1_norm[l]), f1_up, f1_down)

        qg2 = jnp.tile(row(attn_q_gain[l]), (1, LANES // HEAD_DIM))
        kg2 = jnp.tile(row(attn_k_gain[l]), (1, LANES // HEAD_DIM))
        own = (ffn2_w_up, ffn2_w_down, w_br_s5, w_br_attn, w_br_conv, w_out, s5_w_glu)
        u, q, kt, v, hc, f2_up, f2_down, w_s5, w_at, w_cv, w_o, w_glu = _inproj(
            xt, row(mix_norm[l]), w_in_l, qg2, kg2, seq, [(w, l) for w in own])

        pt, q_mat, a_r, a_i, d_row = _s5_tables(
            s5_lambda_re[l], s5_lambda_im[l], s5_log_dt[l], s5_b_re[l], s5_b_im[l],
            s5_c_re[l], s5_c_im[l], s5_d[l])
        s5o = _s5(u.reshape(nb, seq, D_S5), pt, q_mat, a_r, a_i, d_row, w_glu)

        ato = _attn(q.reshape(nb, seq, D_ATT), kt, v.reshape(nb, seq, D_ATT),
                    _attn_bias_vec(attn_rel_bias[l]))

        nh = D_CONV // LANES
        w_dw = conv_w_dw[l].astype(F32).reshape(CONV_WIDTH, nh, LANES).transpose(1, 0, 2)
        nxt = (ffn1_w_up, ffn1_w_down, w_in) if l + 1 < depth else ()
        xt, *nxt_bf16 = _merge(
            xt, s5o.reshape(n, D_S5), ato.reshape(n, D_ATT), hc, row(mix_norm[l]), w_in_l,
            row(b_gate[l]), w_s5, w_at, w_cv, w_o, w_dw,
            conv_b_dw[l].astype(F32).reshape(nh, 1, LANES), row(conv_ln_g[l]), row(conv_ln_b[l]),
            seq, [(w, l + 1) for w in nxt])
        if nxt_bf16:
            f1_up, f1_down, w_in_l = nxt_bf16

        xt = _ffn(xt, row(ffn2_norm[l]), f2_up, f2_down)
    return xt.reshape(nb, seq, d)
```

```python
import functools
import math

import jax
import jax.numpy as jnp
from jax import lax
from jax.experimental import pallas as pl
from jax.experimental.pallas import tpu as pltpu

F32 = jnp.float32
BF16 = jnp.bfloat16

D_MODEL = 1024
D_S5 = 256
S5_GROUP = 16
S5_GROUPS = 16
S5_STATE = 64
D_ATT = 512
HEAD_DIM = 64
HEADS = 8
CHUNK = 64
N_LEFT = 8
MAX_REL = 128
D_CONV = 256
CONV_WIDTH = 31
D_FF = 2816
EPS = 1e-6
LOG2E = math.log2(math.e)
N_SMALL = D_S5 + 3 * D_ATT + 2 * D_CONV

LANES = 128
MXU_DIM = 256
VMEM_LIMIT = 56 * 1024 * 1024

TM = 1024
TM_WIDE = 1024
CAST_MIN_ROWS = 16
CAST_ROWS = 64
S5_L = 8
S5_CB = LANES // 2
S5_GB = S5_CB // S5_GROUP
S5_NCB = D_S5 // S5_CB
S5_BROW = S5_L * S5_CB
S5_BSTATE = 2 * S5_GB * S5_STATE
S5_TOK = 256
ATT_QB = 4 * CHUNK
ATT_W = ATT_QB + N_LEFT * CHUNK
ATT_KT = MXU_DIM
ATT_PERIOD = ATT_QB + ATT_W
CONV_TR = 64
CONV_PAD = 32


def _resident(shape):
    nd = len(shape)
    return pl.BlockSpec(shape, lambda *_: (0,) * nd, pipeline_mode=pl.Buffered(1))


def _rmsnorm_bf16(x, g):
    ms = jnp.mean(x * x, axis=-1, keepdims=True)
    return ((x * lax.rsqrt(ms + EPS)) * g).astype(BF16)


def _sigmoid(x):
    return jax.nn.sigmoid(x)


class _Casts:
    def __init__(self, items, steps):
        self.in_specs, self.out_specs, self.out_shape, self.args, self.nblks = [], [], [], [], []
        for stacked, layer in items:
            _, rows, cols = stacked.shape
            every_step = rows % CAST_MIN_ROWS == 0 and rows // CAST_MIN_ROWS <= steps
            rb = CAST_MIN_ROWS if every_step else CAST_ROWS
            nblk = rows // rb
            assert nblk * rb == rows and nblk <= steps
            self.nblks.append(None if every_step else nblk)
            self.in_specs.append(pl.BlockSpec(
                (None, rb, cols),
                lambda i, layer=layer, nblk=nblk: (layer, jnp.minimum(i, nblk - 1), 0)))
            self.out_specs.append(pl.BlockSpec(
                (rb, cols), lambda i, nblk=nblk: (jnp.minimum(i, nblk - 1), 0)))
            self.out_shape.append(jax.ShapeDtypeStruct((rows, cols), BF16))
            self.args.append(stacked)

    @staticmethod
    def run_predicated(step, nblks, src_refs, dst_refs):
        for nblk in sorted({n for n in nblks if n is not None}):
            @pl.when(step < nblk)
            def _():
                for n, src, dst in zip(nblks, src_refs, dst_refs):
                    if n == nblk:
                        dst[...] = src[...].astype(BF16)

    @staticmethod
    def run_every_step(nblks, src_refs, dst_refs):
        for n, src, dst in zip(nblks, src_refs, dst_refs):
            if n is None:
                dst[...] = src[...].astype(BF16)


def _ffn_kernel(x_hbm, g_ref, wu_hbm, wd_hbm, o_hbm, xbuf, obuf, wa_buf, wb_buf, wd_buf,
                in_sem, out_sem, w_sem, *, bounds, steps):
    chunks = tuple(zip(bounds[:-1], bounds[1:]))

    def rows(i):
        return pl.ds(pl.multiple_of(i * TM_WIDE, TM_WIDE), TM_WIDE)

    def in_copy(i, slot):
        return pltpu.make_async_copy(x_hbm.at[rows(i)], xbuf.at[slot], in_sem.at[slot])

    def out_copy(i, slot):
        return pltpu.make_async_copy(obuf.at[slot], o_hbm.at[rows(i)], out_sem.at[slot])

    def w_copies(c):
        f0, f1 = chunks[c]
        return (pltpu.make_async_copy(wu_hbm.at[:, f0:f1], wa_buf.at[:, f0:f1], w_sem.at[0, c]),
                pltpu.make_async_copy(wu_hbm.at[:, D_FF + f0:D_FF + f1], wb_buf.at[:, f0:f1], w_sem.at[1, c]),
                pltpu.make_async_copy(wd_hbm.at[f0:f1, :], wd_buf.at[f0:f1, :], w_sem.at[2, c]))

    def tile(i, slot, first):
        x = xbuf[slot]
        h = _rmsnorm_bf16(x, g_ref[...])
        y = jnp.zeros_like(x)
        for c, (f0, f1) in enumerate(chunks):
            if first:
                for cp in w_copies(c):
                    cp.wait()
            a = jnp.dot(h, wa_buf[:, f0:f1], preferred_element_type=F32)
            b = jnp.dot(h, wb_buf[:, f0:f1], preferred_element_type=F32)
            act = ((a * _sigmoid(a)) * b).astype(BF16)
            y = y + jnp.dot(act, wd_buf[f0:f1, :], preferred_element_type=F32)
        obuf[slot] = x + 0.5 * y
        out_copy(i, slot).start()

    in_copy(0, 0).start()
    for c in range(len(chunks)):
        for cp in w_copies(c):
            cp.start()
    in_copy(0, 0).wait()
    in_copy(1, 1).start()
    tile(0, 0, True)

    def body(i, carry):
        slot = i % 2
        in_copy(i, slot).wait()

        @pl.when(i + 1 < steps)
        def _():
            in_copy(i + 1, 1 - slot).start()

        @pl.when(i >= 2)
        def _():
            out_copy(i - 2, slot).wait()

        tile(i, slot, False)
        return carry
    lax.fori_loop(1, steps, body, 0)
    out_copy(steps - 2, steps % 2).wait()
    out_copy(steps - 1, (steps - 1) % 2).wait()


FFN_VMEM_LIMIT = 62 * 1024 * 1024


def _ffn_bounds():
    tiles = D_FF // MXU_DIM
    assert tiles * MXU_DIM == D_FF
    return (0, (tiles + 1) // 2 * MXU_DIM, D_FF)


def _ffn(x, g, w_up, w_down):
    n = x.shape[0]
    steps = n // TM_WIDE
    assert steps * TM_WIDE == n and steps >= 2
    bounds = _ffn_bounds()
    tiles = pltpu.VMEM((2, TM_WIDE, D_MODEL), F32)
    w_half = pltpu.VMEM((D_MODEL, D_FF), BF16)
    hbm = pl.BlockSpec(memory_space=pl.ANY)
    return pl.pallas_call(
        functools.partial(_ffn_kernel, bounds=bounds, steps=steps),
        out_shape=jax.ShapeDtypeStruct((n, D_MODEL), F32),
        grid=(1,),
        in_specs=[hbm, _resident((1, D_MODEL)), hbm, hbm],
        out_specs=hbm,
        scratch_shapes=[tiles, tiles, w_half, w_half, pltpu.VMEM((D_FF, D_MODEL), BF16),
                        pltpu.SemaphoreType.DMA((2,)), pltpu.SemaphoreType.DMA((2,)),
                        pltpu.SemaphoreType.DMA((3, len(bounds) - 1))],
        compiler_params=pltpu.CompilerParams(
            dimension_semantics=("arbitrary",), vmem_limit_bytes=FFN_VMEM_LIMIT),
        name="ffn",
    )(x, g, w_up, w_down)


def _head_pair_norm(x2, gain2):
    lane = lax.broadcasted_iota(jnp.int32, (1, LANES), 1)
    first = lane < HEAD_DIM
    sq = x2 * x2
    s_a = jnp.sum(jnp.where(first, sq, 0.0), axis=-1, keepdims=True)
    s_b = jnp.sum(jnp.where(first, 0.0, sq), axis=-1, keepdims=True)
    ms = jnp.where(first, s_a, s_b) * (1.0 / HEAD_DIM)
    return (x2 * lax.rsqrt(ms + EPS)) * gain2


def _inproj_kernel(x_ref, g_ref, w_ref, qg_ref, kg_ref, *refs, cast_nblks):
    n_cast = len(cast_nblks)
    u_ref, q_ref, kt_ref, v_ref, hc_ref = refs[n_cast:n_cast + 5]
    _Casts.run_predicated(pl.program_id(0), cast_nblks, refs[:n_cast], refs[n_cast + 5:])
    _Casts.run_every_step(cast_nblks, refs[:n_cast], refs[n_cast + 5:])
    h = _rmsnorm_bf16(x_ref[...], g_ref[...])
    proj = jnp.dot(h, w_ref[...], preferred_element_type=F32)
    u_ref[...] = proj[:, :D_S5]
    q0, k0, v0, z0 = D_S5, D_S5 + D_ATT, D_S5 + 2 * D_ATT, D_S5 + 3 * D_ATT
    scale = HEAD_DIM ** -0.5 * LOG2E
    for p in range(D_ATT // LANES):
        lo = p * LANES
        qn = _head_pair_norm(proj[:, q0 + lo:q0 + lo + LANES], qg_ref[...])
        kn = _head_pair_norm(proj[:, k0 + lo:k0 + lo + LANES], kg_ref[...])
        q_ref[:, lo:lo + LANES] = (qn * scale).astype(BF16)
        kt_ref[lo:lo + LANES, :] = kn.T.astype(BF16)
    v_ref[...] = proj[:, v0:v0 + D_ATT].astype(BF16)
    a = proj[:, z0:z0 + D_CONV]
    gt = proj[:, z0 + D_CONV:z0 + 2 * D_CONV]
    hc_ref[...] = a * _sigmoid(gt)


def _inproj(x, g, w_in, qg2, kg2, seq, cast_items):
    n = x.shape[0]
    steps = n // TM_WIDE
    tiles_per_seq = seq // TM_WIDE
    assert tiles_per_seq * TM_WIDE == seq
    casts = _Casts(cast_items, steps)
    def tok(width):
        return pl.BlockSpec((TM_WIDE, width), lambda i: (i, 0))
    w_small = pl.BlockSpec((D_MODEL, N_SMALL), lambda i: (0, 0), pipeline_mode=pl.Buffered(1))
    kt_spec = pl.BlockSpec((None, D_ATT, TM_WIDE),
                           lambda i: (i // tiles_per_seq, 0, i % tiles_per_seq))
    return pl.pallas_call(
        functools.partial(_inproj_kernel, cast_nblks=tuple(casts.nblks)),
        out_shape=(jax.ShapeDtypeStruct((n, D_S5), F32),
                   jax.ShapeDtypeStruct((n, D_ATT), BF16),
                   jax.ShapeDtypeStruct((n // seq, D_ATT, seq), BF16),
                   jax.ShapeDtypeStruct((n, D_ATT), BF16),
                   jax.ShapeDtypeStruct((n, D_CONV), F32), *casts.out_shape),
        grid=(steps,),
        in_specs=[tok(D_MODEL), _resident((1, D_MODEL)), w_small,
                  _resident((1, LANES)), _resident((1, LANES)), *casts.in_specs],
        out_specs=(tok(D_S5), tok(D_ATT), kt_spec, tok(D_ATT), tok(D_CONV), *casts.out_specs),
        compiler_params=pltpu.CompilerParams(
            dimension_semantics=("arbitrary",), vmem_limit_bytes=VMEM_LIMIT),
        name="inproj",
    )(x, g, w_in, qg2, kg2, *casts.args)


def _s5_tables(lambda_re, lambda_im, log_dt, b_re, b_im, c_re, c_im, d_skip):
    lr = jnp.minimum(lambda_re.astype(F32), -1e-4)
    li = lambda_im.astype(F32)
    dt = jnp.exp(log_dt.astype(F32))[:, None]
    mag = jnp.exp(lr * dt)
    ar = mag * jnp.cos(li * dt)
    ai = mag * jnp.sin(li * dt)
    den = lr * lr + li * li
    coef_r = ((ar - 1.0) * lr + ai * li) / den
    coef_i = (ai * lr - (ar - 1.0) * li) / den
    br = b_re.astype(F32)
    bi = b_im.astype(F32)
    bbar_r = coef_r[..., None] * br - coef_i[..., None] * bi
    bbar_i = coef_r[..., None] * bi + coef_i[..., None] * br
    cr = c_re.astype(F32)
    ci = c_im.astype(F32)

    pr, pi = [jnp.ones_like(ar)], [jnp.zeros_like(ai)]
    for _ in range(S5_L):
        r, i = pr[-1], pi[-1]
        pr.append(r * ar - i * ai)
        pi.append(r * ai + i * ar)
    pr = jnp.stack(pr)
    pi = jnp.stack(pi)

    ncb, gb = S5_NCB, S5_GB
    pw_r = pr[S5_L - 1::-1]
    pw_i = pi[S5_L - 1::-1]
    e_r = pw_r[..., None] * bbar_r[None] - pw_i[..., None] * bbar_i[None]
    e_i = pw_r[..., None] * bbar_i[None] + pw_i[..., None] * bbar_r[None]
    p_blk = jnp.stack([jnp.transpose(e_r, (0, 1, 3, 2)), jnp.transpose(e_i, (0, 1, 3, 2))], axis=3)
    p_blk = p_blk.reshape(S5_L, ncb, gb * S5_GROUP, 2 * S5_STATE)
    p_blk = jnp.transpose(p_blk, (1, 0, 2, 3)).reshape(ncb, S5_BROW, 2 * S5_STATE)

    m_r = pr[:S5_L, :, None, :] * cr[None] - pi[:S5_L, :, None, :] * ci[None]
    m_i = pr[:S5_L, :, None, :] * ci[None] + pi[:S5_L, :, None, :] * cr[None]
    kern = (jnp.sum(m_r[:, :, :, :, None] * bbar_r[None, :, None, :, :], axis=3)
            - jnp.sum(m_i[:, :, :, :, None] * bbar_i[None, :, None, :, :], axis=3))
    t_blk = jnp.transpose(kern, (0, 1, 3, 2)).reshape(S5_L, ncb, S5_CB, S5_GROUP)
    t_blk = jnp.transpose(t_blk, (1, 0, 2, 3))

    q_r = pr[1:, :, None, :] * cr[None] - pi[1:, :, None, :] * ci[None]
    q_i = -(pr[1:, :, None, :] * ci[None] + pi[1:, :, None, :] * cr[None])
    q_blk = jnp.stack([q_r, q_i], axis=0).reshape(2, S5_L, ncb, gb, S5_GROUP, S5_STATE)
    q_blk = jnp.transpose(q_blk, (2, 1, 4, 0, 3, 5))
    q_blk = q_blk.reshape(ncb, S5_L, S5_GROUP, S5_BSTATE)

    half_blocks = S5_BSTATE // 2 // LANES
    a_r = pr[S5_L].reshape(ncb, half_blocks, 1, LANES)
    a_i = pi[S5_L].reshape(ncb, half_blocks, 1, LANES)
    d_row = jnp.tile(d_skip.astype(F32).reshape(ncb, 1, S5_CB), (1, 1, S5_L))
    pt, q_mat = _s5_expand(p_blk.astype(BF16), t_blk.astype(BF16), q_blk.astype(BF16))
    return pt, q_mat, a_r, a_i, d_row


def _s5_expand_kernel(p_ref, t_ref, q_ref, pt_ref, qm_ref):
    def iota(shape, dim):
        return lax.broadcasted_iota(jnp.int32, shape, dim)

    def onehot(rows, cols, row_key, col_key):
        r, c = iota((rows, cols), 0), iota((rows, cols), 1)
        return jnp.where(row_key(r) == col_key(c), 1.0, 0.0)

    half = S5_BSTATE // 2
    rep = onehot(2 * S5_STATE, S5_BSTATE, lambda r: r,
                 lambda c: (c // half) * S5_STATE + c % S5_STATE).astype(BF16)
    same = onehot(S5_CB, S5_BSTATE, lambda r: r // S5_GROUP, lambda c: (c % half) // S5_STATE)
    rep_c = onehot(S5_GROUP, S5_CB, lambda r: r, lambda c: c % S5_GROUP).astype(BF16)
    same_g = onehot(S5_CB, S5_CB, lambda r: r // S5_GROUP, lambda c: c // S5_GROUP)
    same_q = onehot(S5_BSTATE, S5_CB, lambda r: (r % half) // S5_STATE, lambda c: c // S5_GROUP)
    lag_blk = [(jnp.dot(t_ref[d], rep_c, preferred_element_type=F32) * same_g).astype(BF16)
               for d in range(S5_L)]
    zero_blk = jnp.zeros((S5_CB, S5_CB), BF16)
    for s in range(S5_L):
        rows = slice(s * S5_CB, (s + 1) * S5_CB)
        p_full = jnp.dot(p_ref[rows, :], rep, preferred_element_type=F32) * same
        pt_ref[rows, :S5_BSTATE] = p_full.astype(BF16)
        for t in range(S5_L):
            cols = slice(S5_BSTATE + t * S5_CB, S5_BSTATE + (t + 1) * S5_CB)
            pt_ref[rows, cols] = lag_blk[t - s] if t >= s else zero_blk
    for t in range(S5_L):
        q_full = lax.dot_general(q_ref[t], rep_c, (((0,), (0,)), ((), ())),
                                 preferred_element_type=F32) * same_q
        qm_ref[:, t * S5_CB:(t + 1) * S5_CB] = q_full.astype(BF16)


def _s5_expand(p_blk, t_blk, q_blk):
    def blk(shape):
        return pl.BlockSpec((None,) + shape, lambda cb: (cb,) + (0,) * len(shape))
    return pl.pallas_call(
        _s5_expand_kernel,
        out_shape=(jax.ShapeDtypeStruct((S5_NCB, S5_BROW, S5_BSTATE + S5_BROW), BF16),
                   jax.ShapeDtypeStruct((S5_NCB, S5_BSTATE, S5_BROW), BF16)),
        grid=(S5_NCB,),
        in_specs=[blk((S5_BROW, 2 * S5_STATE)), blk((S5_L, S5_CB, S5_GROUP)),
                  blk((S5_L, S5_GROUP, S5_BSTATE))],
        out_specs=(blk((S5_BROW, S5_BSTATE + S5_BROW)), blk((S5_BSTATE, S5_BROW))),
        compiler_params=pltpu.CompilerParams(
            dimension_semantics=("arbitrary",), vmem_limit_bytes=VMEM_LIMIT),
        name="s5_tables",
    )(p_blk, t_blk, q_blk)


def _s5_kernel(ua_ref, ub_ref, pt_ref, q_ref, ar_ref, ai_ref, d_ref, wg_ref, o_ref,
               sr_ref, si_ref, u8_ref, es_ref, ot_ref):
    nb = ua_ref.shape[0]
    ncl = S5_TOK // S5_L
    nlb = S5_BSTATE // LANES
    hlb = nlb // 2
    u_refs = (ua_ref, ub_ref)
    low = lax.broadcasted_iota(jnp.int32, (1, LANES), 1) < S5_CB

    def interleave(a, b):
        return (jnp.where(low, a, pltpu.roll(b, S5_CB, axis=1)),
                jnp.where(low, pltpu.roll(a, S5_CB, axis=1), b))

    @pl.when(pl.program_id(0) == 0)
    def _():
        sr_ref[...] = jnp.zeros_like(sr_ref)
        si_ref[...] = jnp.zeros_like(si_ref)

    sub = 8
    ncg = ncl // sub
    grp = nb * sub
    for hh, u_ref in enumerate(u_refs):
        for t2 in range(S5_L // 2):
            for cg in range(ncg):
                tok = [u_ref[:, pl.ds(cg * sub * S5_L + 2 * t2 + k, sub, stride=S5_L), :]
                       .reshape(grp, LANES) for k in range(2)]
                for k, blk in enumerate(interleave(*tok)):
                    u8_ref[2 * hh + k, cg * grp:(cg + 1) * grp, t2 * LANES:(t2 + 1) * LANES] = blk

    for cb in range(S5_NCB):
        e = jnp.dot(u8_ref[cb].astype(BF16), pt_ref[cb, :, :S5_BSTATE],
                    preferred_element_type=F32)
        for j in range(nlb):
            es_ref[cb, j] = e[:, j * LANES:(j + 1) * LANES]

    a_r = ar_ref[...]
    a_i = ai_ref[...]
    sr = sr_ref[...]
    si = si_ref[...]
    for cl in range(ncl):
        step = pl.ds((cl // sub) * grp + cl % sub, nb, stride=sub)
        e_r = es_ref[:, :hlb, step, :]
        e_i = es_ref[:, hlb:, step, :]
        es_ref[:, :hlb, step, :] = sr
        es_ref[:, hlb:, step, :] = si
        sr, si = a_r * sr - a_i * si + e_r, a_r * si + a_i * sr + e_i
    sr_ref[...] = sr
    si_ref[...] = si

    ys = []
    for cb in range(S5_NCB):
        u8 = u8_ref[cb]
        s_all = jnp.concatenate([es_ref[cb, j] for j in range(nlb)], axis=1).astype(BF16)
        y = (jnp.dot(u8.astype(BF16), pt_ref[cb, :, S5_BSTATE:], preferred_element_type=F32)
             + jnp.dot(s_all, q_ref[cb], preferred_element_type=F32) + d_ref[cb] * u8)
        ys.append(jax.nn.gelu(y))
    for t2 in range(S5_L // 2):
        lanes = slice(t2 * LANES, (t2 + 1) * LANES)
        halves = [interleave(ys[2 * hh][:, lanes], ys[2 * hh + 1][:, lanes]) for hh in range(2)]
        for k in range(2):
            t = 2 * t2 + k
            yt = jnp.concatenate([halves[0][k], halves[1][k]], axis=1).astype(BF16)
            ag = jnp.dot(yt, wg_ref[...], preferred_element_type=F32)
            out = ag[:, :D_S5] * _sigmoid(ag[:, D_S5:])
            for hh in range(D_S5 // LANES):
                for cg in range(ncg):
                    ot_ref[hh, :, pl.ds(cg * sub * S5_L + t, sub, stride=S5_L), :] = (
                        out[cg * grp:(cg + 1) * grp, hh * LANES:(hh + 1) * LANES].reshape(nb, sub, LANES))
    o_ref[...] = jnp.concatenate([ot_ref[0], ot_ref[1]], axis=-1).astype(BF16)


def _s5(u, pt, q_mat, a_r, a_i, d_row, w_glu):
    nb, seq, _ = u.shape
    rows = nb * (S5_TOK // S5_L)
    nlb = S5_BSTATE // LANES
    assert 2 * S5_CB == LANES and D_S5 == 2 * LANES
    return pl.pallas_call(
        _s5_kernel,
        out_shape=jax.ShapeDtypeStruct((nb, seq, D_S5), BF16),
        grid=(seq // S5_TOK,),
        in_specs=[pl.BlockSpec((nb, S5_TOK, LANES), lambda i: (0, i, 0)),
                  pl.BlockSpec((nb, S5_TOK, LANES), lambda i: (0, i, 1)),
                  _resident(pt.shape), _resident(q_mat.shape), _resident(a_r.shape),
                  _resident(a_i.shape), _resident(d_row.shape), _resident(w_glu.shape)],
        out_specs=pl.BlockSpec((nb, S5_TOK, D_S5), lambda i: (0, i, 0)),
        scratch_shapes=[pltpu.VMEM((S5_NCB, nlb // 2, nb, LANES), F32),
                        pltpu.VMEM((S5_NCB, nlb // 2, nb, LANES), F32),
                        pltpu.VMEM((S5_NCB, rows, S5_BROW), F32),
                        pltpu.VMEM((S5_NCB, nlb, rows, LANES), F32),
                        pltpu.VMEM((D_S5 // LANES, nb, S5_TOK, LANES), F32)],
        compiler_params=pltpu.CompilerParams(
            dimension_semantics=("arbitrary",), vmem_limit_bytes=VMEM_LIMIT),
        name="s5",
    )(u, u, pt, q_mat, a_r, a_i, d_row, w_glu)


def _attn_bias_vec(rel_bias):
    rb = rel_bias.astype(F32) * LOG2E
    far_past, far_future = rb[:, 2 * MAX_REL:], rb[:, :1]
    n_const = N_LEFT * CHUNK - MAX_REL
    return jnp.concatenate([
        jnp.broadcast_to(far_past, (HEADS, n_const)),
        rb[:, ::-1],
        jnp.broadcast_to(far_future, (HEADS, ATT_W - n_const - 2 * MAX_REL - 1)),
        jnp.broadcast_to(far_past, (HEADS, ATT_PERIOD - ATT_W)),
    ], axis=1)


def _attn_build_bias(vec_ref, bm_ref):
    r = lax.broadcasted_iota(jnp.int32, (ATT_QB, ATT_W), 0)
    c = lax.broadcasted_iota(jnp.int32, (ATT_QB, ATT_W), 1)
    dchunk = r // CHUNK + N_LEFT - c // CHUNK
    ok = (dchunk >= 0) & (dchunk <= N_LEFT)
    for h in range(HEADS):
        base = jnp.broadcast_to(vec_ref[h:h + 1, :], (ATT_QB, ATT_PERIOD))
        toep = pltpu.roll(base, 0, 1, stride=1, stride_axis=0)[:, :ATT_W]
        bm_ref[h // 2, h % 2] = jnp.where(ok, toep, -1e30)


def _attn_kernel(q_ref, kt_ref, v_ref, vec_ref, o_ref, bm_ref):
    seq = q_ref.shape[1]
    lane = lax.broadcasted_iota(jnp.int32, (1, LANES), 1)
    first = lane < HEAD_DIM

    @pl.when(pl.program_id(0) == 0)
    def _():
        _attn_build_bias(vec_ref, bm_ref)

    def block(q0, k0, width):
        npair = D_ATT // LANES
        outs = [[None, None] for _ in range(npair)]
        for hh in range(2):
            sel = first if hh == 0 else jnp.logical_not(first)
            for pr in range(npair):
                cols = slice(pr * LANES, (pr + 1) * LANES)
                q2 = q_ref[0, pl.ds(q0, ATT_QB), cols]
                qh = jnp.where(sel, q2, jnp.zeros_like(q2))
                for c0 in range(0, width, ATT_KT):
                    kt = kt_ref[0, cols, pl.ds(k0 + c0, ATT_KT)]
                    vw = v_ref[0, pl.ds(k0 + c0, ATT_KT), cols]
                    s = jnp.dot(qh, kt, preferred_element_type=F32)
                    b0 = ATT_W - width + c0
                    s = s + bm_ref[pr, hh, :, b0:b0 + ATT_KT]
                    vh = jnp.where(sel, vw, jnp.ones_like(vw))
                    if c0 == 0:
                        mx = jnp.max(s, axis=-1, keepdims=True)
                        acc = jnp.dot(jnp.exp2(s - mx).astype(BF16), vh, preferred_element_type=F32)
                    else:
                        mx_new = jnp.maximum(mx, jnp.max(s, axis=-1, keepdims=True))
                        p = jnp.exp2(s - mx_new).astype(BF16)
                        acc = acc * jnp.exp2(mx - mx_new) + jnp.dot(p, vh, preferred_element_type=F32)
                        mx = mx_new
                outs[pr][hh] = acc
        for pr, (acc_a, acc_b) in enumerate(outs):
            num = jnp.where(first, acc_a, acc_b)
            den = pltpu.roll(jnp.where(first, acc_b, acc_a), HEAD_DIM, axis=1)
            o_ref[0, pl.ds(q0, ATT_QB), pr * LANES:(pr + 1) * LANES] = (num / den).astype(BF16)

    n_short = N_LEFT * CHUNK // ATT_QB
    for qb in range(n_short):
        block(qb * ATT_QB, 0, (qb + 1) * ATT_QB)

    def body(qb, carry):
        q0 = pl.multiple_of(qb * ATT_QB, ATT_QB)
        k0 = pl.multiple_of(q0 - N_LEFT * CHUNK, ATT_QB)
        block(q0, k0, ATT_W)
        return carry
    lax.fori_loop(n_short, seq // ATT_QB, body, 0)


def _attn(q, kt, v, bias_vec):
    nb, seq, _ = q.shape
    seqblk = pl.BlockSpec((1, seq, D_ATT), lambda b: (b, 0, 0))
    return pl.pallas_call(
        _attn_kernel,
        out_shape=jax.ShapeDtypeStruct((nb, seq, D_ATT), BF16),
        grid=(nb,),
        in_specs=[seqblk, pl.BlockSpec((1, D_ATT, seq), lambda b: (b, 0, 0)), seqblk,
                  _resident(bias_vec.shape)],
        out_specs=seqblk,
        scratch_shapes=[pltpu.VMEM((HEADS // 2, 2, ATT_QB, ATT_W), F32)],
        compiler_params=pltpu.CompilerParams(
            dimension_semantics=("arbitrary",), vmem_limit_bytes=VMEM_LIMIT),
        name="attn",
    )(q, kt, v, bias_vec)


def _conv_taps(hc_ref, cw_ref, cb_ref, cbuf_ref, first_tile):
    nh = D_CONV // LANES
    tm = hc_ref.shape[0]

    @pl.when(first_tile)
    def _():
        cbuf_ref[:, :CONV_PAD, :] = jnp.zeros((nh, CONV_PAD, LANES), F32)
    for hh in range(nh):
        cbuf_ref[hh, CONV_PAD:, :] = hc_ref[:, hh * LANES:(hh + 1) * LANES]
    shift = CONV_PAD - (CONV_WIDTH - 1)
    halves = []
    for hh in range(nh):
        tiles = []
        for r0 in range(0, tm, CONV_TR):
            acc = jnp.zeros((CONV_TR, LANES), F32) + cb_ref[hh]
            for j in range(CONV_WIDTH):
                lo = r0 + shift + j
                acc = acc + cw_ref[hh, j:j + 1, :] * cbuf_ref[hh, lo:lo + CONV_TR, :]
            tiles.append(acc)
        halves.append(jnp.concatenate(tiles, axis=0))
    for hh in range(nh):
        cbuf_ref[hh, :CONV_PAD, :] = cbuf_ref[hh, tm:tm + CONV_PAD, :]
    return halves


def _conv_norm(halves, lg_ref, lb_ref):
    acc = jnp.concatenate(halves, axis=-1)
    mu = jnp.mean(acc, axis=-1, keepdims=True)
    cen = acc - mu
    var = jnp.mean(cen * cen, axis=-1, keepdims=True)
    y = (cen * lax.rsqrt(var + EPS)) * lg_ref[...] + lb_ref[...]
    return (y * _sigmoid(y)).astype(BF16)


def _after(v, zero_ref, width):
    bits = pltpu.bitcast(v, jnp.uint32)
    acc = bits[0:8, :]
    for r0 in range(8, v.shape[0], 8):
        acc = acc | bits[r0:r0 + 8, :]
    zero = pltpu.bitcast(acc[0:1, :] & zero_ref[...], F32)
    return jnp.tile(zero, (1, width // LANES)).astype(BF16)


def _merge_kernel(x_ref, s5_ref, at_ref, hc_ref, g_ref, wg_ref, bg_ref, ws_ref, wa_ref, wc_ref, wo_ref,
                  cw_ref, cb_ref, lg_ref, lb_ref, zero_ref, *refs, tiles_per_seq, cast_nblks):
    n_cast = len(cast_nblks)
    o_ref, cbuf_ref = refs[n_cast], refs[-1]
    _Casts.run_predicated(pl.program_id(0), cast_nblks, refs[:n_cast], refs[n_cast + 1:-1])
    halves = _conv_taps(hc_ref, cw_ref, cb_ref, cbuf_ref, pl.program_id(0) % tiles_per_seq == 0)
    _Casts.run_every_step(cast_nblks, refs[:n_cast], refs[n_cast + 1:-1])
    x = x_ref[...]
    h = _rmsnorm_bf16(x, g_ref[...])
    lhs = (h, h + _after(halves[0], zero_ref, D_MODEL), h + _after(halves[1], zero_ref, D_MODEL))
    branches = (lambda: s5_ref[...], lambda: at_ref[...], lambda: _conv_norm(halves, lg_ref, lb_ref))
    merged = jnp.zeros_like(x)
    for i, (branch, w_ref) in enumerate(zip(branches, (ws_ref, wa_ref, wc_ref))):
        cols = slice(i * D_MODEL, (i + 1) * D_MODEL)
        gcols = slice(N_SMALL + i * D_MODEL, N_SMALL + (i + 1) * D_MODEL)
        logits = jnp.dot(lhs[i], wg_ref[:, gcols], preferred_element_type=F32) + bg_ref[:, cols]
        y = jnp.dot(branch(), w_ref[...], preferred_element_type=F32)
        merged = merged + _sigmoid(logits) * y
    o_ref[...] = x + jnp.dot(merged.astype(BF16), wo_ref[...], preferred_element_type=F32)


def _merge(x, s5o, ato, hc, g, w_in, b_gate, w_s5, w_at, w_cv, w_out, w_dw, b_dw, ln_g, ln_b, seq,
           cast_items):
    n = x.shape[0]
    nh = D_CONV // LANES
    zero = jnp.zeros((1, LANES), jnp.uint32)
    assert seq % TM == 0
    steps = n // TM
    casts = _Casts(cast_items, steps)
    def tok(width):
        return pl.BlockSpec((TM, width), lambda i: (i, 0))
    return pl.pallas_call(
        functools.partial(_merge_kernel, tiles_per_seq=seq // TM, cast_nblks=tuple(casts.nblks)),
        out_shape=(jax.ShapeDtypeStruct((n, D_MODEL), F32), *casts.out_shape),
        grid=(steps,),
        in_specs=[tok(D_MODEL), tok(D_S5), tok(D_ATT), tok(D_CONV),
                  _resident((1, D_MODEL)), _resident(w_in.shape),
                  _resident((1, 3 * D_MODEL)), _resident((D_S5, D_MODEL)),
                  _resident((D_ATT, D_MODEL)), _resident((D_CONV, D_MODEL)),
                  _resident((D_MODEL, D_MODEL)), _resident((nh, CONV_WIDTH, LANES)),
                  _resident((nh, 1, LANES)), _resident((1, D_CONV)), _resident((1, D_CONV)),
                  _resident((1, LANES)), *casts.in_specs],
        out_specs=(tok(D_MODEL), *casts.out_specs),
        scratch_shapes=[pltpu.VMEM((nh, CONV_PAD + TM, LANES), F32)],
        compiler_params=pltpu.CompilerParams(
            dimension_semantics=("arbitrary",), vmem_limit_bytes=VMEM_LIMIT),
        name="merge",
    )(x, s5o, ato, hc, g, w_in, b_gate, w_s5, w_at, w_cv, w_out, w_dw, b_dw, ln_g, ln_b, zero,
      *casts.args)


def kernel(x, ffn1_norm, ffn1_w_up, ffn1_w_down, mix_norm, w_in, b_gate, s5_lambda_re, s5_lambda_im, s5_log_dt, s5_b_re, s5_b_im, s5_c_re, s5_c_im, s5_d, s5_w_glu, w_br_s5, attn_q_gain, attn_k_gain, attn_rel_bias, w_br_attn, conv_w_dw, conv_b_dw, conv_ln_g, conv_ln_b, w_br_conv, w_out, ffn2_norm, ffn2_w_up, ffn2_w_down):
    nb, seq, d = x.shape
    n = nb * seq
    depth = ffn1_norm.shape[0]
    xt = x.reshape(n, d)
    row = lambda v: v.reshape(1, -1).astype(F32)
    f1_up, f1_down, w_in_l = (ffn1_w_up[0].astype(BF16), ffn1_w_down[0].astype(BF16),
                              w_in[0].astype(BF16))
    for l in range(depth):
        xt = _ffn(xt, row(ffn1_norm[l]), f1_up, f1_down)

        qg2 = jnp.tile(row(attn_q_gain[l]), (1, LANES // HEAD_DIM))
        kg2 = jnp.tile(row(attn_k_gain[l]), (1, LANES // HEAD_DIM))
        own = (ffn2_w_up, ffn2_w_down, w_br_s5, w_br_attn, w_br_conv, w_out, s5_w_glu)
        u, q, kt, v, hc, f2_up, f2_down, w_s5, w_at, w_cv, w_o, w_glu = _inproj(
            xt, row(mix_norm[l]), w_in_l, qg2, kg2, seq, [(w, l) for w in own])

        pt, q_mat, a_r, a_i, d_row = _s5_tables(
            s5_lambda_re[l], s5_lambda_im[l], s5_log_dt[l], s5_b_re[l], s5_b_im[l],
            s5_c_re[l], s5_c_im[l], s5_d[l])
        s5o = _s5(u.reshape(nb, seq, D_S5), pt, q_mat, a_r, a_i, d_row, w_glu)

        ato = _attn(q.reshape(nb, seq, D_ATT), kt, v.reshape(nb, seq, D_ATT),
                    _attn_bias_vec(attn_rel_bias[l]))

        nh = D_CONV // LANES
        w_dw = conv_w_dw[l].astype(F32).reshape(CONV_WIDTH, nh, LANES).transpose(1, 0, 2)
        nxt = (ffn1_w_up, ffn1_w_down, w_in) if l + 1 < depth else ()
        xt, *nxt_bf16 = _merge(
            xt, s5o.reshape(n, D_S5), ato.reshape(n, D_ATT), hc, row(mix_norm[l]), w_in_l,
            row(b_gate[l]), w_s5, w_at, w_cv, w_o, w_dw,
            conv_b_dw[l].astype(F32).reshape(nh, 1, LANES), row(conv_ln_g[l]), row(conv_ln_b[l]),
            seq, [(w, l + 1) for w in nxt])
        if nxt_bf16:
            f1_up, f1_down, w_in_l = nxt_bf16

        xt = _ffn(xt, row(ffn2_norm[l]), f2_up, f2_down)
    return xt.reshape(nb, seq, d)
```
